```python
import math
import jax
import jax.numpy as jnp
from jax import lax
import numpy as np

D_MODEL = 2048
BATCH = 2
SEQ = 4096
DEPTH = 2

N_A_LAYERS = DEPTH // 2
N_B_LAYERS = DEPTH - N_A_LAYERS

GDN_HEAD_DIM = 128
GDN_QK_HEADS = D_MODEL // GDN_HEAD_DIM
GDN_V_HEADS = 2 * GDN_QK_HEADS
GDN_QK_WIDTH = GDN_QK_HEADS * GDN_HEAD_DIM
GDN_V_WIDTH = GDN_V_HEADS * GDN_HEAD_DIM
GDN_CONV_CH = 2 * GDN_QK_WIDTH + GDN_V_WIDTH
GDN_IN_WIDTH = GDN_CONV_CH + GDN_V_WIDTH + 2 * GDN_V_HEADS
GDN_CONV = 4
GDN_CHUNK = 64

NSA_HEAD_DIM = 128
NSA_HEADS = D_MODEL // NSA_HEAD_DIM
NSA_KV_GROUPS = 4
NSA_HEADS_PER_GROUP = NSA_HEADS // NSA_KV_GROUPS
NSA_Q_WIDTH = NSA_HEADS * NSA_HEAD_DIM
NSA_KV_WIDTH = 6 * NSA_KV_GROUPS * NSA_HEAD_DIM
CMP_BLOCK = 32
CMP_STRIDE = 16
CMP_HIDDEN = 512
SLC_BLOCK = 64
SLC_TOPK = 16
SLC_LOCAL = 2
WINDOW = 512
WIN_Q_BLOCK = 128
SLC_Q_BLOCK = 64
ROPE_THETA = 10000.0

N_EXPERTS = 32
N_GROUPS = 8
EXPERTS_PER_GROUP = N_EXPERTS // N_GROUPS
TOP_K = 2
D_EXPERT = D_MODEL // 4

DEEPNORM_ALPHA = (2 * DEPTH) ** 0.25
DEEPNORM_BETA = (8 * DEPTH) ** -0.25
LN_EPS = 1e-5
RMS_EPS = 1e-6
NEG_INF = -1e30
FORCE_SCORE = 1e6

kernel_name = 'yoco_gdn_nsa_group_moe_deepnorm'


def layer_norm(x, g, b):
    xf = x.astype(jnp.float32)
    mu = jnp.mean(xf, axis=-1, keepdims=True)
    var = jnp.mean(jnp.square(xf - mu), axis=-1, keepdims=True)
    return (xf - mu) * lax.rsqrt(var + LN_EPS) * g.astype(jnp.float32) + b.astype(jnp.float32)


def l2_normalize(a):
    return a * lax.rsqrt(jnp.sum(a * a, axis=-1, keepdims=True) + RMS_EPS)


def rope(x, pos):
    half = x.shape[-1] // 2
    inv_freq = ROPE_THETA ** (-jnp.arange(half, dtype=jnp.float32) / half)
    ang = pos.astype(jnp.float32)[:, None] * inv_freq[None, :]
    cos = jnp.cos(ang)[None, :, None, :]
    sin = jnp.sin(ang)[None, :, None, :]
    xf = x.astype(jnp.float32)
    x1, x2 = xf[..., :half], xf[..., half:]
    return jnp.concatenate([x1 * cos - x2 * sin, x2 * cos + x1 * sin], axis=-1).astype(x.dtype)


def causal_depthwise_conv(x, w):
    k_width, ch = w.shape
    return lax.conv_general_dilated(
        x, w[:, None, :].astype(x.dtype), window_strides=(1,), padding=[(k_width - 1, 0)],
        dimension_numbers=('NWC', 'WIO', 'NWC'), feature_group_count=ch)


def chunked_gated_delta_rule(q, k, v, g, beta):
    B, T, H, Dk = q.shape
    Dv = v.shape[-1]
    C = GDN_CHUNK
    N = T // C

    def to_chunks(a):
        return jnp.moveaxis(a.reshape((B, N, C, H) + a.shape[3:]), 3, 1)

    qc, kc, vc = to_chunks(q), to_chunks(k), to_chunks(v)
    bc = to_chunks(beta)
    gc = jnp.cumsum(to_chunks(g), axis=-1)
    idx = jnp.arange(C)
    causal = idx[:, None] >= idx[None, :]
    strict = idx[:, None] > idx[None, :]
    diff = gc[..., :, None] - gc[..., None, :]
    decay = jnp.where(causal, jnp.exp(jnp.where(causal, diff, 0.0)), 0.0)
    kb = kc * bc[..., None]
    m = jnp.where(strict, jnp.einsum('bhnid,bhnjd->bhnij', kb, kc) * decay, 0.0)
    lhs = m + jnp.eye(C, dtype=m.dtype)
    rhs = jnp.concatenate([vc * bc[..., None], kb * jnp.exp(gc)[..., None]], axis=-1)
    sol = lax.linalg.triangular_solve(lhs, rhs, left_side=True, lower=True)
    u, w = sol[..., :Dv], sol[..., Dv:]
    a_intra = jnp.einsum('bhnid,bhnjd->bhnij', qc, kc) * decay

    def step(state, inp):
        q_i, k_i, u_i, w_i, g_i, a_i = inp
        v_new = u_i - jnp.einsum('bhck,bhkv->bhcv', w_i, state)
        o_i = (jnp.einsum('bhck,bhkv->bhcv', q_i * jnp.exp(g_i)[..., None], state)
               + jnp.einsum('bhij,bhjv->bhiv', a_i, v_new))
        g_last = g_i[..., -1:]
        state = (state * jnp.exp(g_last)[..., None]
                 + jnp.einsum('bhck,bhcv->bhkv', k_i * jnp.exp(g_last - g_i)[..., None], v_new))
        return state, o_i

    xs = tuple(jnp.moveaxis(a, 2, 0) for a in (qc, kc, u, w, gc, a_intra))
    state0 = jnp.zeros((B, H, Dk, Dv), jnp.float32)
    _, o = lax.scan(step, state0, xs)
    o = jnp.moveaxis(o, 0, 2)
    return jnp.moveaxis(o, 1, 3).reshape(B, T, H, Dv)


def gdn_mixer(x, w_in, conv_w, a_log, dt_bias, norm_w, w_out):
    B, T, _ = x.shape
    Hk, Hv, Dh = GDN_QK_HEADS, GDN_V_HEADS, GDN_HEAD_DIM
    proj = x @ w_in
    qkv, z, a_raw, b_raw = jnp.split(
        proj, [GDN_CONV_CH, GDN_CONV_CH + GDN_V_WIDTH, GDN_CONV_CH + GDN_V_WIDTH + Hv], axis=-1)
    qkv = jax.nn.silu(causal_depthwise_conv(qkv, conv_w)).astype(jnp.float32)
    q, k, v = jnp.split(qkv, [GDN_QK_WIDTH, 2 * GDN_QK_WIDTH], axis=-1)
    rep = Hv // Hk
    q = jnp.repeat(l2_normalize(q.reshape(B, T, Hk, Dh)), rep, axis=2) * Dh ** -0.5
    k = jnp.repeat(l2_normalize(k.reshape(B, T, Hk, Dh)), rep, axis=2)
    v = v.reshape(B, T, Hv, Dh)
    g = -jnp.exp(a_log.astype(jnp.float32)) * jax.nn.softplus(a_raw.astype(jnp.float32) + dt_bias.astype(jnp.float32))
    beta = jax.nn.sigmoid(b_raw.astype(jnp.float32))
    o = chunked_gated_delta_rule(q, k, v, g, beta)
    o = (o * lax.rsqrt(jnp.mean(o * o, axis=-1, keepdims=True) + RMS_EPS) * norm_w.astype(jnp.float32)
         * jax.nn.silu(z.astype(jnp.float32).reshape(B, T, Hv, Dh)))
    return (o.reshape(B, T, GDN_V_WIDTH).astype(x.dtype) @ w_out).astype(x.dtype)


def compress_blocks(a, pe, w1, w2):
    B, T, G, D = a.shape
    n_cmp = (T - CMP_BLOCK) // CMP_STRIDE + 1
    idx = np.arange(n_cmp)[:, None] * CMP_STRIDE + np.arange(CMP_BLOCK)[None, :]
    blocks = a[:, idx] + pe[None, None, :, None, :]
    flat = jnp.moveaxis(blocks, 3, 2).reshape(B, n_cmp, G, CMP_BLOCK * D)
    return jax.nn.silu(flat @ w1) @ w2


def nsa_shared_kv(h, kv_w, cmp_pe, cmp_w1, cmp_w2):
    B, T, _ = h.shape
    kv = (h @ kv_w).reshape(B, T, 6, NSA_KV_GROUPS, NSA_HEAD_DIM)
    pos = jnp.arange(T)
    k_cmp = compress_blocks(kv[:, :, 0], cmp_pe[0], cmp_w1[0], cmp_w2[0])
    v_cmp = compress_blocks(kv[:, :, 1], cmp_pe[1], cmp_w1[1], cmp_w2[1])
    k_slc = rope(kv[:, :, 2], pos)
    v_slc = kv[:, :, 3]
    k_win = rope(kv[:, :, 4], pos)
    v_win = kv[:, :, 5]
    return (k_cmp, v_cmp, k_slc, v_slc, k_win, v_win)


def compression_overlap(n_cmp, n_slc):
    c0 = np.arange(n_cmp) * CMP_STRIDE
    s0 = np.arange(n_slc) * SLC_BLOCK
    ov = (np.minimum(c0[:, None] + CMP_BLOCK, s0[None, :] + SLC_BLOCK)
          - np.maximum(c0[:, None], s0[None, :]))
    return jnp.asarray(np.clip(ov, 0, None) / CMP_BLOCK, dtype=jnp.float32)


def nsa_compressed(q, k_cmp, v_cmp):
    B, T, G, R, D = q.shape
    n_cmp = k_cmp.shape[1]
    s = jnp.einsum('btgrd,bcgd->bgrtc', q, k_cmp).astype(jnp.float32) * D ** -0.5
    block_end = jnp.arange(n_cmp) * CMP_STRIDE + CMP_BLOCK - 1
    visible = block_end[None, :] <= jnp.arange(T)[:, None]
    p = jax.nn.softmax(jnp.where(visible, s, NEG_INF), axis=-1) * visible
    o = jnp.einsum('bgrtc,bcgd->btgrd', p, v_cmp)
    p_slc = jnp.einsum('bgtc,cs->bgts', p.sum(axis=2), compression_overlap(n_cmp, T // SLC_BLOCK))
    return o, p_slc


def nsa_selected(q, k, v, p_slc):
    B, T, G, R, D = q.shape
    L = SLC_BLOCK
    n_slc = T // L
    n_sel = min(SLC_TOPK, n_slc)
    cur = (jnp.arange(T) // L)[:, None]
    blk = jnp.arange(n_slc)[None, :]
    causal_blk = blk <= cur
    forced = (blk == 0) | (causal_blk & (blk > cur - SLC_LOCAL))
    score = jnp.where(causal_blk, jnp.where(forced, FORCE_SCORE, p_slc), -1.0)
    top_score, top_idx = lax.top_k(score, n_sel)
    top_valid = top_score >= 0.0
    kb = jnp.moveaxis(k.reshape(B, n_slc, L, G, D), 3, 1)
    vb = jnp.moveaxis(v.reshape(B, n_slc, L, G, D), 3, 1)
    nq = T // SLC_Q_BLOCK
    b_ix = jnp.arange(B)[:, None, None, None]
    g_ix = jnp.arange(G)[None, :, None, None]
    scale = D ** -0.5

    def block(inp):
        q_c, idx_c, valid_c, t0 = inp
        k_sel = kb[b_ix, g_ix, idx_c]
        v_sel = vb[b_ix, g_ix, idx_c]
        s = jnp.einsum('bqgrd,bgqnld->bgrqnl', q_c, k_sel).astype(jnp.float32) * scale
        t_c = t0 + jnp.arange(SLC_Q_BLOCK)
        key_pos = idx_c[..., None] * L + jnp.arange(L)
        ok = (key_pos <= t_c[None, None, :, None, None]) & valid_c[..., None]
        p = jax.nn.softmax(jnp.where(ok[:, :, None], s, NEG_INF), axis=(-2, -1))
        return jnp.einsum('bgrqnl,bgqnld->bqgrd', p, v_sel)

    q_blocks = jnp.moveaxis(q.reshape(B, nq, SLC_Q_BLOCK, G, R, D), 1, 0)
    idx_blocks = jnp.moveaxis(top_idx.reshape(B, G, nq, SLC_Q_BLOCK, n_sel), 2, 0)
    valid_blocks = jnp.moveaxis(top_valid.reshape(B, G, nq, SLC_Q_BLOCK, n_sel), 2, 0)
    starts = jnp.arange(nq) * SLC_Q_BLOCK
    o = lax.map(block, (q_blocks, idx_blocks, valid_blocks, starts))
    return jnp.moveaxis(o, 0, 1).reshape(B, T, G, R, D)


def nsa_window(q, k, v):
    B, T, G, R, D = q.shape
    QB = WIN_Q_BLOCK
    nb = T // QB
    n_prev = WINDOW // QB
    kw = (n_prev + 1) * QB

    def band(a):
        ap = jnp.pad(a, ((0, 0), (WINDOW, 0), (0, 0), (0, 0))).reshape(B, nb + n_prev, QB, G, D)
        return jnp.concatenate([ap[:, i:i + nb] for i in range(n_prev + 1)], axis=2)

    kband, vband = band(k), band(v)
    qb = q.reshape(B, nb, QB, G, R, D)
    s = jnp.einsum('bnqgrd,bnkgd->bngrqk', qb, kband).astype(jnp.float32) * D ** -0.5
    q_pos = (jnp.arange(nb)[:, None] * QB + jnp.arange(QB)[None, :])[:, :, None]
    k_pos = (jnp.arange(nb)[:, None] * QB - WINDOW + jnp.arange(kw)[None, :])[:, None, :]
    ok = (k_pos <= q_pos) & (k_pos > q_pos - WINDOW) & (k_pos >= 0)
    p = jax.nn.softmax(jnp.where(ok[None, :, None, None], s, NEG_INF), axis=-1)
    o = jnp.einsum('bngrqk,bnkgd->bnqgrd', p, vband)
    return o.reshape(B, T, G, R, D)


def nsa_mixer(x, w_q, w_out, k_cmp, v_cmp, k_slc, v_slc, k_win, v_win):
    B, T, _ = x.shape
    G, R, D = NSA_KV_GROUPS, NSA_HEADS_PER_GROUP, NSA_HEAD_DIM
    proj = x @ w_q
    q = proj[..., :NSA_Q_WIDTH].reshape(B, T, NSA_HEADS, D)
    gates = jax.nn.sigmoid(proj[..., NSA_Q_WIDTH:].astype(jnp.float32)).reshape(B, T, 3, G, R, 1)
    q_rot = rope(q, jnp.arange(T)).reshape(B, T, G, R, D)
    o_cmp, p_slc = nsa_compressed(q.reshape(B, T, G, R, D), k_cmp, v_cmp)
    o_slc = nsa_selected(q_rot, k_slc, v_slc, p_slc)
    o_win = nsa_window(q_rot, k_win, v_win)
    o = gates[:, :, 0] * o_cmp + gates[:, :, 1] * o_slc + gates[:, :, 2] * o_win
    return (o.reshape(B, T, NSA_Q_WIDTH).astype(x.dtype) @ w_out).astype(x.dtype)


def moe_ffn(x, router_w, router_bias, w_gate, w_up, w_down):
    B, T, D = x.shape
    h = x.reshape(B * T, D)
    aff = jax.nn.sigmoid((h @ router_w).astype(jnp.float32))
    biased = (aff + router_bias.astype(jnp.float32)).reshape(-1, N_GROUPS, EXPERTS_PER_GROUP)
    group_score = lax.top_k(biased, TOP_K)[0].sum(axis=-1)
    best_group = jnp.argmax(group_score, axis=-1)
    in_group = jnp.arange(N_GROUPS)[None, :] == best_group[:, None]
    cand = jnp.where(in_group[:, :, None], biased, NEG_INF).reshape(-1, N_EXPERTS)
    _, top_idx = lax.top_k(cand, TOP_K)
    top_aff = jnp.take_along_axis(aff, top_idx, axis=-1)
    top_w = top_aff / jnp.sum(top_aff, axis=-1, keepdims=True)
    gate = jnp.einsum('nk,nke->ne', top_w, jax.nn.one_hot(top_idx, N_EXPERTS, dtype=jnp.float32))
    hid = jax.nn.silu(jnp.einsum('nd,edf->nef', h, w_gate)) * jnp.einsum('nd,edf->nef', h, w_up)
    y = jnp.einsum('nef,efd->nd', hid * gate[:, :, None], w_down)
    return y.reshape(B, T, D).astype(x.dtype)


def setup_inputs(seed: int = 0) -> dict:
    key = jax.random.key(seed)
    ks = jax.random.split(key, 20)
    f32 = jnp.float32

    def dense(k, shape, fan_in, gain=1.0):
        return jax.random.normal(k, shape, f32) * (gain * fan_in ** -0.5)

    la, lb = N_A_LAYERS, N_B_LAYERS
    dt = jnp.exp(jax.random.uniform(ks[4], (la, GDN_V_HEADS), f32, math.log(1e-3), math.log(1e-1)))
    return {
        'x': jax.random.normal(ks[0], (BATCH, SEQ, D_MODEL), f32),
        'a_w_in': dense(ks[1], (la, D_MODEL, GDN_IN_WIDTH), D_MODEL),
        'a_conv_w': dense(ks[2], (la, GDN_CONV, GDN_CONV_CH), GDN_CONV),
        'a_a_log': jnp.log(jax.random.uniform(ks[3], (la, GDN_V_HEADS), f32, 1.0, 16.0)),
        'a_dt_bias': dt + jnp.log(-jnp.expm1(-dt)),
        'a_norm_w': 1.0 + 0.01 * jax.random.normal(ks[5], (la, GDN_HEAD_DIM), f32),
        'a_w_out': dense(ks[6], (la, GDN_V_WIDTH, D_MODEL), GDN_V_WIDTH, DEEPNORM_BETA),
        'kv_w': dense(ks[7], (D_MODEL, NSA_KV_WIDTH), D_MODEL),
        'cmp_pe': 0.1 * jax.random.normal(ks[8], (2, CMP_BLOCK, NSA_HEAD_DIM), f32),
        'cmp_w1': dense(ks[9], (2, CMP_BLOCK * NSA_HEAD_DIM, CMP_HIDDEN), CMP_BLOCK * NSA_HEAD_DIM),
        'cmp_w2': dense(ks[10], (2, CMP_HIDDEN, NSA_HEAD_DIM), CMP_HIDDEN),
        'b_w_q': dense(ks[11], (lb, D_MODEL, NSA_Q_WIDTH + 3 * NSA_HEADS), D_MODEL),
        'b_w_out': dense(ks[12], (lb, NSA_Q_WIDTH, D_MODEL), NSA_Q_WIDTH, DEEPNORM_BETA),
        'router_w': dense(ks[13], (D_MODEL, N_EXPERTS), D_MODEL),
        'router_bias': 0.01 * jax.random.normal(ks[14], (N_EXPERTS,), f32),
        'moe_w_gate': dense(ks[15], (DEPTH, N_EXPERTS, D_MODEL, D_EXPERT), D_MODEL),
        'moe_w_up': dense(ks[16], (DEPTH, N_EXPERTS, D_MODEL, D_EXPERT), D_MODEL),
        'moe_w_down': dense(ks[17], (DEPTH, N_EXPERTS, D_EXPERT, D_MODEL), D_EXPERT, DEEPNORM_BETA),
        'ln_g': 1.0 + 0.01 * jax.random.normal(ks[18], (DEPTH, 2, D_MODEL), f32),
        'ln_b': 0.01 * jax.random.normal(ks[19], (DEPTH, 2, D_MODEL), f32),
    }


def reference(x, a_w_in, a_conv_w, a_a_log, a_dt_bias, a_norm_w, a_w_out, kv_w, cmp_pe, cmp_w1, cmp_w2,
              b_w_q, b_w_out, router_w, router_bias, moe_w_gate, moe_w_up, moe_w_down, ln_g, ln_b):
    shared_kv = None
    for layer in range(DEPTH):
        if layer < N_A_LAYERS:
            mix = gdn_mixer(x, a_w_in[layer], a_conv_w[layer], a_a_log[layer], a_dt_bias[layer],
                            a_norm_w[layer], a_w_out[layer])
        else:
            if shared_kv is None:
                shared_kv = nsa_shared_kv(x, kv_w, cmp_pe, cmp_w1, cmp_w2)
            j = layer - N_A_LAYERS
            mix = nsa_mixer(x, b_w_q[j], b_w_out[j], *shared_kv)
        x = layer_norm(DEEPNORM_ALPHA * x + mix, ln_g[layer, 0], ln_b[layer, 0]).astype(x.dtype)
        ffn = moe_ffn(x, router_w, router_bias, moe_w_gate[layer], moe_w_up[layer], moe_w_down[layer])
        x = layer_norm(DEEPNORM_ALPHA * x + ffn, ln_g[layer, 1], ln_b[layer, 1]).astype(x.dtype)
    return x
```

```python
import functools
import math

import jax
import jax.numpy as jnp
import numpy as np
from jax import lax
from jax.experimental import pallas as pl
from jax.experimental.pallas import tpu as pltpu

D_MODEL = 2048
BATCH = 2
SEQ = 4096
DEPTH = 2
N_A_LAYERS = DEPTH // 2

GDN_HEAD_DIM = 128
GDN_QK_HEADS = D_MODEL // GDN_HEAD_DIM
GDN_V_HEADS = 2 * GDN_QK_HEADS
GDN_QK_WIDTH = GDN_QK_HEADS * GDN_HEAD_DIM
GDN_V_WIDTH = GDN_V_HEADS * GDN_HEAD_DIM
GDN_CONV_CH = 2 * GDN_QK_WIDTH + GDN_V_WIDTH
GDN_CHUNK = 64

NSA_HEAD_DIM = 128
NSA_HEADS = D_MODEL // NSA_HEAD_DIM
NSA_KV_GROUPS = 4
NSA_HEADS_PER_GROUP = NSA_HEADS // NSA_KV_GROUPS
NSA_Q_WIDTH = NSA_HEADS * NSA_HEAD_DIM
CMP_BLOCK = 32
CMP_STRIDE = 16
SLC_BLOCK = 64
SLC_TOPK = 16
SLC_LOCAL = 2
WINDOW = 512
WIN_Q_BLOCK = 128
SLC_Q_BLOCK = 64
ROPE_THETA = 10000.0

N_EXPERTS = 32
N_GROUPS = 8
EXPERTS_PER_GROUP = N_EXPERTS // N_GROUPS
TOP_K = 2
D_EXPERT = D_MODEL // 4

DEEPNORM_ALPHA = (2 * DEPTH) ** 0.25
LN_EPS = 1e-5
RMS_EPS = 1e-6
NEG_INF = -1e30
FORCE_SCORE = 1e6

LANE = 128
VMEM_LIMIT = 48 * 1024 * 1024
MOE_TILE = 256


def _mm_kernel(a_ref, b_ref, o_ref):
    o_ref[...] = jnp.dot(a_ref[...], b_ref[...], preferred_element_type=jnp.float32)


def pmatmul(a, b, tm=512, tn=1024):
    m, k = a.shape
    n = b.shape[1]
    n_pad = -n % LANE
    if n_pad:
        b = jnp.pad(b, ((0, 0), (0, n_pad)))
    np_ = n + n_pad
    tn = math.gcd(np_, tn)
    tm = math.gcd(m, tm)
    out = pl.pallas_call(
        _mm_kernel,
        grid=(m // tm, np_ // tn),
        in_specs=[pl.BlockSpec((tm, k), lambda i, j: (i, 0)),
                  pl.BlockSpec((k, tn), lambda i, j: (0, j))],
        out_specs=pl.BlockSpec((tm, tn), lambda i, j: (i, j)),
        out_shape=jax.ShapeDtypeStruct((m, np_), jnp.float32),
        compiler_params=pltpu.CompilerParams(
            dimension_semantics=("parallel", "arbitrary"), vmem_limit_bytes=VMEM_LIMIT),
        name="dense_matmul",
    )(a.astype(jnp.bfloat16), b.astype(jnp.bfloat16))
    return out[:, :n] if n_pad else out


def dense(x, w):
    lead = x.shape[:-1]
    return pmatmul(x.reshape(-1, x.shape[-1]), w).reshape(lead + (w.shape[-1],))


def _moe_kernel(tile_expert_ref, n_tiles_ref, xs_ref, gate_ref, wg_ref, wu_ref, wd_ref, o_ref):
    del tile_expert_ref
    i = pl.program_id(0)

    @pl.when(i < n_tiles_ref[0])
    def _():
        x = xs_ref[...]
        g = jnp.dot(x, wg_ref[0], preferred_element_type=jnp.float32)
        u = jnp.dot(x, wu_ref[0], preferred_element_type=jnp.float32)
        h = (g * jax.nn.sigmoid(g)) * u * gate_ref[...]
        o_ref[...] = jnp.dot(h.astype(jnp.bfloat16), wd_ref[0], preferred_element_type=jnp.float32)

    @pl.when(i >= n_tiles_ref[0])
    def _():
        o_ref[...] = jnp.zeros_like(o_ref)


def moe_ffn(x, router_w, router_bias, w_gate, w_up, w_down):
    B, T, D = x.shape
    n_tok = B * T
    h = x.reshape(n_tok, D)
    aff = jax.nn.sigmoid(jnp.dot(h, router_w).astype(jnp.float32))
    biased = (aff + router_bias.astype(jnp.float32)).reshape(-1, N_GROUPS, EXPERTS_PER_GROUP)
    group_score = lax.top_k(biased, TOP_K)[0].sum(axis=-1)
    best_group = jnp.argmax(group_score, axis=-1)
    in_group = jnp.arange(N_GROUPS)[None, :] == best_group[:, None]
    cand = jnp.where(in_group[:, :, None], biased, NEG_INF).reshape(-1, N_EXPERTS)
    _, top_idx = lax.top_k(cand, TOP_K)
    top_aff = jnp.take_along_axis(aff, top_idx, axis=-1)
    top_w = top_aff / jnp.sum(top_aff, axis=-1, keepdims=True)

    n_asg = n_tok * TOP_K
    max_tiles = n_asg // MOE_TILE + N_EXPERTS
    n_rows = max_tiles * MOE_TILE
    e_flat = top_idx.reshape(-1).astype(jnp.int32)
    order = jnp.argsort(e_flat, stable=True).astype(jnp.int32)
    e_sorted = e_flat[order]
    counts = jnp.zeros((N_EXPERTS,), jnp.int32).at[e_flat].add(1)
    tiles_per = (counts + MOE_TILE - 1) // MOE_TILE
    tile_end = jnp.cumsum(tiles_per)
    row_start = (tile_end - tiles_per) * MOE_TILE
    first = jnp.cumsum(counts) - counts
    row_of_sorted = row_start[e_sorted] + (jnp.arange(n_asg, dtype=jnp.int32) - first[e_sorted])
    row_of_asg = jnp.zeros((n_asg,), jnp.int32).at[order].set(row_of_sorted)
    token_of_row = jnp.zeros((n_rows,), jnp.int32).at[row_of_sorted].set(order // TOP_K)
    gate_of_row = jnp.zeros((n_rows,), jnp.float32).at[row_of_asg].set(top_w.reshape(-1))
    n_tiles = tile_end[-1:].astype(jnp.int32)
    tile_expert = jnp.minimum(
        jnp.searchsorted(tile_end, jnp.arange(max_tiles, dtype=jnp.int32), side="right"),
        N_EXPERTS - 1).astype(jnp.int32)
    tile_expert = jnp.where(jnp.arange(max_tiles) < n_tiles[0], tile_expert,
                            tile_expert[jnp.maximum(n_tiles[0] - 1, 0)])

    xs = h.astype(jnp.bfloat16)[token_of_row]
    wspec_in = pl.BlockSpec((1, D, D_EXPERT), lambda i, te, nt: (te[i], 0, 0))
    y = pl.pallas_call(
        _moe_kernel,
        grid_spec=pltpu.PrefetchScalarGridSpec(
            num_scalar_prefetch=2,
            grid=(max_tiles,),
            in_specs=[pl.BlockSpec((MOE_TILE, D), lambda i, te, nt: (i, 0)),
                      pl.BlockSpec((MOE_TILE, 1), lambda i, te, nt: (i, 0)),
                      wspec_in, wspec_in,
                      pl.BlockSpec((1, D_EXPERT, D), lambda i, te, nt: (te[i], 0, 0))],
            out_specs=pl.BlockSpec((MOE_TILE, D), lambda i, te, nt: (i, 0)),
        ),
        out_shape=jax.ShapeDtypeStruct((n_rows, D), jnp.float32),
        compiler_params=pltpu.CompilerParams(
            dimension_semantics=("arbitrary",), vmem_limit_bytes=VMEM_LIMIT),
        name="routed_moe",
    )(tile_expert, n_tiles, xs, gate_of_row[:, None],
      w_gate.astype(jnp.bfloat16), w_up.astype(jnp.bfloat16), w_down.astype(jnp.bfloat16))
    rows = row_of_asg.reshape(n_tok, TOP_K)
    out = y[rows[:, 0]] + y[rows[:, 1]]
    return out.reshape(B, T, D).astype(x.dtype)


def layer_norm(x, g, b):
    xf = x.astype(jnp.float32)
    mu = jnp.mean(xf, axis=-1, keepdims=True)
    var = jnp.mean(jnp.square(xf - mu), axis=-1, keepdims=True)
    return (xf - mu) * lax.rsqrt(var + LN_EPS) * g.astype(jnp.float32) + b.astype(jnp.float32)


def l2_normalize(a):
    return a * lax.rsqrt(jnp.sum(a * a, axis=-1, keepdims=True) + RMS_EPS)


def rope(x, pos):
    half = x.shape[-1] // 2
    inv_freq = ROPE_THETA ** (-jnp.arange(half, dtype=jnp.float32) / half)
    ang = pos.astype(jnp.float32)[:, None] * inv_freq[None, :]
    cos = jnp.cos(ang)[None, :, None, :]
    sin = jnp.sin(ang)[None, :, None, :]
    xf = x.astype(jnp.float32)
    x1, x2 = xf[..., :half], xf[..., half:]
    return jnp.concatenate([x1 * cos - x2 * sin, x2 * cos + x1 * sin], axis=-1).astype(x.dtype)


def causal_depthwise_conv(x, w):
    k_width, ch = w.shape
    return lax.conv_general_dilated(
        x, w[:, None, :].astype(x.dtype), window_strides=(1,), padding=[(k_width - 1, 0)],
        dimension_numbers=('NWC', 'WIO', 'NWC'), feature_group_count=ch)


def chunked_gated_delta_rule(q, k, v, g, beta):
    B, T, H, Dk = q.shape
    Dv = v.shape[-1]
    C = GDN_CHUNK
    N = T // C

    def to_chunks(a):
        return jnp.moveaxis(a.reshape((B, N, C, H) + a.shape[3:]), 3, 1)

    qc, kc, vc = to_chunks(q), to_chunks(k), to_chunks(v)
    bc = to_chunks(beta)
    gc = jnp.cumsum(to_chunks(g), axis=-1)
    idx = jnp.arange(C)
    causal = idx[:, None] >= idx[None, :]
    strict = idx[:, None] > idx[None, :]
    diff = gc[..., :, None] - gc[..., None, :]
    decay = jnp.where(causal, jnp.exp(jnp.where(causal, diff, 0.0)), 0.0)
    kb = kc * bc[..., None]
    m = jnp.where(strict, jnp.einsum('bhnid,bhnjd->bhnij', kb, kc) * decay, 0.0)
    lhs = m + jnp.eye(C, dtype=m.dtype)
    rhs = jnp.concatenate([vc * bc[..., None], kb * jnp.exp(gc)[..., None]], axis=-1)
    sol = lax.linalg.triangular_solve(lhs, rhs, left_side=True, lower=True)
    u, w = sol[..., :Dv], sol[..., Dv:]
    a_intra = jnp.einsum('bhnid,bhnjd->bhnij', qc, kc) * decay

    def step(state, inp):
        q_i, k_i, u_i, w_i, g_i, a_i = inp
        v_new = u_i - jnp.einsum('bhck,bhkv->bhcv', w_i, state)
        o_i = (jnp.einsum('bhck,bhkv->bhcv', q_i * jnp.exp(g_i)[..., None], state)
               + jnp.einsum('bhij,bhjv->bhiv', a_i, v_new))
        g_last = g_i[..., -1:]
        state = (state * jnp.exp(g_last)[..., None]
                 + jnp.einsum('bhck,bhcv->bhkv', k_i * jnp.exp(g_last - g_i)[..., None], v_new))
        return state, o_i

    xs = tuple(jnp.moveaxis(a, 2, 0) for a in (qc, kc, u, w, gc, a_intra))
    state0 = jnp.zeros((B, H, Dk, Dv), jnp.float32)
    _, o = lax.scan(step, state0, xs)
    o = jnp.moveaxis(o, 0, 2)
    return jnp.moveaxis(o, 1, 3).reshape(B, T, H, Dv)


def gdn_mixer(x, w_in, conv_w, a_log, dt_bias, norm_w, w_out):
    B, T, _ = x.shape
    Hk, Hv, Dh = GDN_QK_HEADS, GDN_V_HEADS, GDN_HEAD_DIM
    qkv = dense(x, w_in[:, :GDN_CONV_CH])
    z = dense(x, w_in[:, GDN_CONV_CH:GDN_CONV_CH + GDN_V_WIDTH])
    a_raw, b_raw = jnp.split(dense(x, w_in[:, GDN_CONV_CH + GDN_V_WIDTH:]), [Hv], axis=-1)
    qkv = jax.nn.silu(causal_depthwise_conv(qkv, conv_w)).astype(jnp.float32)
    q, k, v = jnp.split(qkv, [GDN_QK_WIDTH, 2 * GDN_QK_WIDTH], axis=-1)
    rep = Hv // Hk
    q = jnp.repeat(l2_normalize(q.reshape(B, T, Hk, Dh)), rep, axis=2) * Dh ** -0.5
    k = jnp.repeat(l2_normalize(k.reshape(B, T, Hk, Dh)), rep, axis=2)
    v = v.reshape(B, T, Hv, Dh)
    g = -jnp.exp(a_log.astype(jnp.float32)) * jax.nn.softplus(
        a_raw.astype(jnp.float32) + dt_bias.astype(jnp.float32))
    beta = jax.nn.sigmoid(b_raw.astype(jnp.float32))
    o = chunked_gated_delta_rule(q, k, v, g, beta)
    o = (o * lax.rsqrt(jnp.mean(o * o, axis=-1, keepdims=True) + RMS_EPS) * norm_w.astype(jnp.float32)
         * jax.nn.silu(z.astype(jnp.float32).reshape(B, T, Hv, Dh)))
    return dense(o.reshape(B, T, GDN_V_WIDTH).astype(x.dtype), w_out).astype(x.dtype)


def compress_blocks(a, pe, w1, w2):
    B, T, G, D = a.shape
    n_cmp = (T - CMP_BLOCK) // CMP_STRIDE + 1
    idx = np.arange(n_cmp)[:, None] * CMP_STRIDE + np.arange(CMP_BLOCK)[None, :]
    blocks = a[:, idx] + pe[None, None, :, None, :]
    flat = jnp.moveaxis(blocks, 3, 2).reshape(B, n_cmp, G, CMP_BLOCK * D)
    return jax.nn.silu(flat @ w1) @ w2


def nsa_shared_kv(h, kv_w, cmp_pe, cmp_w1, cmp_w2):
    B, T, _ = h.shape
    kv = dense(h, kv_w).reshape(B, T, 6, NSA_KV_GROUPS, NSA_HEAD_DIM)
    pos = jnp.arange(T)
    k_cmp = compress_blocks(kv[:, :, 0], cmp_pe[0], cmp_w1[0], cmp_w2[0])
    v_cmp = compress_blocks(kv[:, :, 1], cmp_pe[1], cmp_w1[1], cmp_w2[1])
    k_slc = rope(kv[:, :, 2], pos)
    v_slc = kv[:, :, 3]
    k_win = rope(kv[:, :, 4], pos)
    v_win = kv[:, :, 5]
    return (k_cmp, v_cmp, k_slc, v_slc, k_win, v_win)


def compression_overlap(n_cmp, n_slc):
    c0 = np.arange(n_cmp) * CMP_STRIDE
    s0 = np.arange(n_slc) * SLC_BLOCK
    ov = (np.minimum(c0[:, None] + CMP_BLOCK, s0[None, :] + SLC_BLOCK)
          - np.maximum(c0[:, None], s0[None, :]))
    return jnp.asarray(np.clip(ov, 0, None) / CMP_BLOCK, dtype=jnp.float32)


def nsa_compressed(q, k_cmp, v_cmp):
    B, T, G, R, D = q.shape
    n_cmp = k_cmp.shape[1]
    s = jnp.einsum('btgrd,bcgd->bgrtc', q, k_cmp).astype(jnp.float32) * D ** -0.5
    block_end = jnp.arange(n_cmp) * CMP_STRIDE + CMP_BLOCK - 1
    visible = block_end[None, :] <= jnp.arange(T)[:, None]
    p = jax.nn.softmax(jnp.where(visible, s, NEG_INF), axis=-1) * visible
    o = jnp.einsum('bgrtc,bcgd->btgrd', p, v_cmp)
    p_slc = jnp.einsum('bgtc,cs->bgts', p.sum(axis=2), compression_overlap(n_cmp, T // SLC_BLOCK))
    return o, p_slc


def nsa_selected(q, k, v, p_slc):
    B, T, G, R, D = q.shape
    L = SLC_BLOCK
    n_slc = T // L
    n_sel = min(SLC_TOPK, n_slc)
    cur = (jnp.arange(T) // L)[:, None]
    blk = jnp.arange(n_slc)[None, :]
    causal_blk = blk <= cur
    forced = (blk == 0) | (causal_blk & (blk > cur - SLC_LOCAL))
    score = jnp.where(causal_blk, jnp.where(forced, FORCE_SCORE, p_slc), -1.0)
    top_score, top_idx = lax.top_k(score, n_sel)
    top_valid = top_score >= 0.0
    kb = jnp.moveaxis(k.reshape(B, n_slc, L, G, D), 3, 1)
    vb = jnp.moveaxis(v.reshape(B, n_slc, L, G, D), 3, 1)
    nq = T // SLC_Q_BLOCK
    b_ix = jnp.arange(B)[:, None, None, None]
    g_ix = jnp.arange(G)[None, :, None, None]
    scale = D ** -0.5

    def block(inp):
        q_c, idx_c, valid_c, t0 = inp
        k_sel = kb[b_ix, g_ix, idx_c]
        v_sel = vb[b_ix, g_ix, idx_c]
        s = jnp.einsum('bqgrd,bgqnld->bgrqnl', q_c, k_sel).astype(jnp.float32) * scale
        t_c = t0 + jnp.arange(SLC_Q_BLOCK)
        key_pos = idx_c[..., None] * L + jnp.arange(L)
        ok = (key_pos <= t_c[None, None, :, None, None]) & valid_c[..., None]
        p = jax.nn.softmax(jnp.where(ok[:, :, None], s, NEG_INF), axis=(-2, -1))
        return jnp.einsum('bgrqnl,bgqnld->bqgrd', p, v_sel)

    q_blocks = jnp.moveaxis(q.reshape(B, nq, SLC_Q_BLOCK, G, R, D), 1, 0)
    idx_blocks = jnp.moveaxis(top_idx.reshape(B, G, nq, SLC_Q_BLOCK, n_sel), 2, 0)
    valid_blocks = jnp.moveaxis(top_valid.reshape(B, G, nq, SLC_Q_BLOCK, n_sel), 2, 0)
    starts = jnp.arange(nq) * SLC_Q_BLOCK
    o = lax.map(block, (q_blocks, idx_blocks, valid_blocks, starts))
    return jnp.moveaxis(o, 0, 1).reshape(B, T, G, R, D)


def nsa_window(q, k, v):
    B, T, G, R, D = q.shape
    QB = WIN_Q_BLOCK
    nb = T // QB
    n_prev = WINDOW // QB
    kw = (n_prev + 1) * QB

    def band(a):
        ap = jnp.pad(a, ((0, 0), (WINDOW, 0), (0, 0), (0, 0))).reshape(B, nb + n_prev, QB, G, D)
        return jnp.concatenate([ap[:, i:i + nb] for i in range(n_prev + 1)], axis=2)

    kband, vband = band(k), band(v)
    qb = q.reshape(B, nb, QB, G, R, D)
    s = jnp.einsum('bnqgrd,bnkgd->bngrqk', qb, kband).astype(jnp.float32) * D ** -0.5
    q_pos = (jnp.arange(nb)[:, None] * QB + jnp.arange(QB)[None, :])[:, :, None]
    k_pos = (jnp.arange(nb)[:, None] * QB - WINDOW + jnp.arange(kw)[None, :])[:, None, :]
    ok = (k_pos <= q_pos) & (k_pos > q_pos - WINDOW) & (k_pos >= 0)
    p = jax.nn.softmax(jnp.where(ok[None, :, None, None], s, NEG_INF), axis=-1)
    o = jnp.einsum('bngrqk,bnkgd->bnqgrd', p, vband)
    return o.reshape(B, T, G, R, D)


def nsa_mixer(x, w_q, w_out, k_cmp, v_cmp, k_slc, v_slc, k_win, v_win):
    B, T, _ = x.shape
    G, R, D = NSA_KV_GROUPS, NSA_HEADS_PER_GROUP, NSA_HEAD_DIM
    q = dense(x, w_q[:, :NSA_Q_WIDTH]).reshape(B, T, NSA_HEADS, D)
    gates = jax.nn.sigmoid(dense(x, w_q[:, NSA_Q_WIDTH:])).reshape(B, T, 3, G, R, 1)
    q_rot = rope(q, jnp.arange(T)).reshape(B, T, G, R, D)
    o_cmp, p_slc = nsa_compressed(q.reshape(B, T, G, R, D), k_cmp, v_cmp)
    o_slc = nsa_selected(q_rot, k_slc, v_slc, p_slc)
    o_win = nsa_window(q_rot, k_win, v_win)
    o = gates[:, :, 0] * o_cmp + gates[:, :, 1] * o_slc + gates[:, :, 2] * o_win
    return dense(o.reshape(B, T, NSA_Q_WIDTH).astype(x.dtype), w_out).astype(x.dtype)


def kernel(x, a_w_in, a_conv_w, a_a_log, a_dt_bias, a_norm_w, a_w_out, kv_w, cmp_pe, cmp_w1, cmp_w2,
           b_w_q, b_w_out, router_w, router_bias, moe_w_gate, moe_w_up, moe_w_down, ln_g, ln_b):
    shared_kv = None
    for layer in range(DEPTH):
        if layer < N_A_LAYERS:
            mix = gdn_mixer(x, a_w_in[layer], a_conv_w[layer], a_a_log[layer], a_dt_bias[layer],
                            a_norm_w[layer], a_w_out[layer])
        else:
            if shared_kv is None:
                shared_kv = nsa_shared_kv(x, kv_w, cmp_pe, cmp_w1, cmp_w2)
            j = layer - N_A_LAYERS
            mix = nsa_mixer(x, b_w_q[j], b_w_out[j], *shared_kv)
        x = layer_norm(DEEPNORM_ALPHA * x + mix, ln_g[layer, 0], ln_b[layer, 0]).astype(x.dtype)
        ffn = moe_ffn(x, router_w, router_bias, moe_w_gate[layer], moe_w_up[layer], moe_w_down[layer])
        x = layer_norm(DEEPNORM_ALPHA * x + ffn, ln_g[layer, 1], ln_b[layer, 1]).astype(x.dtype)
    return x
```

```python
import functools
import math

import jax
import jax.numpy as jnp
import numpy as np
from jax import lax
from jax.experimental import pallas as pl
from jax.experimental.pallas import tpu as pltpu

D_MODEL = 2048
BATCH = 2
SEQ = 4096
DEPTH = 2
N_A_LAYERS = DEPTH // 2

GDN_HEAD_DIM = 128
GDN_QK_HEADS = D_MODEL // GDN_HEAD_DIM
GDN_V_HEADS = 2 * GDN_QK_HEADS
GDN_QK_WIDTH = GDN_QK_HEADS * GDN_HEAD_DIM
GDN_V_WIDTH = GDN_V_HEADS * GDN_HEAD_DIM
GDN_CONV_CH = 2 * GDN_QK_WIDTH + GDN_V_WIDTH
GDN_CHUNK = 64

NSA_HEAD_DIM = 128
NSA_HEADS = D_MODEL // NSA_HEAD_DIM
NSA_KV_GROUPS = 4
NSA_HEADS_PER_GROUP = NSA_HEADS // NSA_KV_GROUPS
NSA_Q_WIDTH = NSA_HEADS * NSA_HEAD_DIM
CMP_BLOCK = 32
CMP_STRIDE = 16
SLC_BLOCK = 64
SLC_TOPK = 16
SLC_LOCAL = 2
WINDOW = 512
WIN_Q_BLOCK = 128
SLC_Q_BLOCK = 64
ROPE_THETA = 10000.0

N_EXPERTS = 32
N_GROUPS = 8
EXPERTS_PER_GROUP = N_EXPERTS // N_GROUPS
TOP_K = 2
D_EXPERT = D_MODEL // 4

DEEPNORM_ALPHA = (2 * DEPTH) ** 0.25
LN_EPS = 1e-5
RMS_EPS = 1e-6
NEG_INF = -1e30
FORCE_SCORE = 1e6

LANE = 128
VMEM_LIMIT = 48 * 1024 * 1024
MOE_TILE = 256
ATTN_TILE = 256
T_BLOCKS = SEQ // SLC_BLOCK


def _mm_kernel(a_ref, b_ref, o_ref):
    o_ref[...] = jnp.dot(a_ref[...], b_ref[...], preferred_element_type=jnp.float32)


def pmatmul(a, b, tm=512, tn=1024):
    m, k = a.shape
    n = b.shape[1]
    n_pad = -n % LANE
    if n_pad:
        b = jnp.pad(b, ((0, 0), (0, n_pad)))
    np_ = n + n_pad
    tn = math.gcd(np_, tn)
    tm = math.gcd(m, tm)
    out = pl.pallas_call(
        _mm_kernel,
        grid=(m // tm, np_ // tn),
        in_specs=[pl.BlockSpec((tm, k), lambda i, j: (i, 0)),
                  pl.BlockSpec((k, tn), lambda i, j: (0, j))],
        out_specs=pl.BlockSpec((tm, tn), lambda i, j: (i, j)),
        out_shape=jax.ShapeDtypeStruct((m, np_), jnp.float32),
        compiler_params=pltpu.CompilerParams(
            dimension_semantics=("parallel", "arbitrary"), vmem_limit_bytes=VMEM_LIMIT),
        name="dense_matmul",
    )(a.astype(jnp.bfloat16), b.astype(jnp.bfloat16))
    return out[:, :n] if n_pad else out


def dense(x, w):
    lead = x.shape[:-1]
    return pmatmul(x.reshape(-1, x.shape[-1]), w).reshape(lead + (w.shape[-1],))


def _moe_kernel(tile_expert_ref, n_tiles_ref, xs_ref, gate_ref, wg_ref, wu_ref, wd_ref, o_ref):
    del tile_expert_ref
    i = pl.program_id(0)

    @pl.when(i < n_tiles_ref[0])
    def _():
        x = xs_ref[...]
        g = jnp.dot(x, wg_ref[0], preferred_element_type=jnp.float32)
        u = jnp.dot(x, wu_ref[0], preferred_element_type=jnp.float32)
        h = (g * jax.nn.sigmoid(g)) * u * gate_ref[...]
        o_ref[...] = jnp.dot(h.astype(jnp.bfloat16), wd_ref[0], preferred_element_type=jnp.float32)

    @pl.when(i >= n_tiles_ref[0])
    def _():
        o_ref[...] = jnp.zeros_like(o_ref)


def moe_ffn(x, router_w, router_bias, w_gate, w_up, w_down):
    B, T, D = x.shape
    n_tok = B * T
    h = x.reshape(n_tok, D)
    aff = jax.nn.sigmoid(jnp.dot(h, router_w).astype(jnp.float32))
    biased = (aff + router_bias.astype(jnp.float32)).reshape(-1, N_GROUPS, EXPERTS_PER_GROUP)
    group_score = lax.top_k(biased, TOP_K)[0].sum(axis=-1)
    best_group = jnp.argmax(group_score, axis=-1)
    in_group = jnp.arange(N_GROUPS)[None, :] == best_group[:, None]
    cand = jnp.where(in_group[:, :, None], biased, NEG_INF).reshape(-1, N_EXPERTS)
    _, top_idx = lax.top_k(cand, TOP_K)
    top_aff = jnp.take_along_axis(aff, top_idx, axis=-1)
    top_w = top_aff / jnp.sum(top_aff, axis=-1, keepdims=True)

    n_asg = n_tok * TOP_K
    max_tiles = n_asg // MOE_TILE + N_EXPERTS
    n_rows = max_tiles * MOE_TILE
    e_flat = top_idx.reshape(-1).astype(jnp.int32)
    order = jnp.argsort(e_flat, stable=True).astype(jnp.int32)
    e_sorted = e_flat[order]
    counts = jnp.zeros((N_EXPERTS,), jnp.int32).at[e_flat].add(1)
    tiles_per = (counts + MOE_TILE - 1) // MOE_TILE
    tile_end = jnp.cumsum(tiles_per)
    row_start = (tile_end - tiles_per) * MOE_TILE
    first = jnp.cumsum(counts) - counts
    row_of_sorted = row_start[e_sorted] + (jnp.arange(n_asg, dtype=jnp.int32) - first[e_sorted])
    row_of_asg = jnp.zeros((n_asg,), jnp.int32).at[order].set(row_of_sorted)
    token_of_row = jnp.zeros((n_rows,), jnp.int32).at[row_of_sorted].set(order // TOP_K)
    gate_of_row = jnp.zeros((n_rows,), jnp.float32).at[row_of_asg].set(top_w.reshape(-1))
    n_tiles = tile_end[-1:].astype(jnp.int32)
    tile_expert = jnp.minimum(
        jnp.searchsorted(tile_end, jnp.arange(max_tiles, dtype=jnp.int32), side="right"),
        N_EXPERTS - 1).astype(jnp.int32)
    tile_expert = jnp.where(jnp.arange(max_tiles) < n_tiles[0], tile_expert,
                            tile_expert[jnp.maximum(n_tiles[0] - 1, 0)])

    xs = h.astype(jnp.bfloat16)[token_of_row]
    wspec_in = pl.BlockSpec((1, D, D_EXPERT), lambda i, te, nt: (te[i], 0, 0))
    y = pl.pallas_call(
        _moe_kernel,
        grid_spec=pltpu.PrefetchScalarGridSpec(
            num_scalar_prefetch=2,
            grid=(max_tiles,),
            in_specs=[pl.BlockSpec((MOE_TILE, D), lambda i, te, nt: (i, 0)),
                      pl.BlockSpec((MOE_TILE, 1), lambda i, te, nt: (i, 0)),
                      wspec_in, wspec_in,
                      pl.BlockSpec((1, D_EXPERT, D), lambda i, te, nt: (te[i], 0, 0))],
            out_specs=pl.BlockSpec((MOE_TILE, D), lambda i, te, nt: (i, 0)),
        ),
        out_shape=jax.ShapeDtypeStruct((n_rows, D), jnp.float32),
        compiler_params=pltpu.CompilerParams(
            dimension_semantics=("arbitrary",), vmem_limit_bytes=VMEM_LIMIT),
        name="routed_moe",
    )(tile_expert, n_tiles, xs, gate_of_row[:, None],
      w_gate.astype(jnp.bfloat16), w_up.astype(jnp.bfloat16), w_down.astype(jnp.bfloat16))
    rows = row_of_asg.reshape(n_tok, TOP_K)
    out = y[rows[:, 0]] + y[rows[:, 1]]
    return out.reshape(B, T, D).astype(x.dtype)


def layer_norm(x, g, b):
    xf = x.astype(jnp.float32)
    mu = jnp.mean(xf, axis=-1, keepdims=True)
    var = jnp.mean(jnp.square(xf - mu), axis=-1, keepdims=True)
    return (xf - mu) * lax.rsqrt(var + LN_EPS) * g.astype(jnp.float32) + b.astype(jnp.float32)


def l2_normalize(a):
    return a * lax.rsqrt(jnp.sum(a * a, axis=-1, keepdims=True) + RMS_EPS)


def rope(x, pos):
    half = x.shape[-1] // 2
    inv_freq = ROPE_THETA ** (-jnp.arange(half, dtype=jnp.float32) / half)
    ang = pos.astype(jnp.float32)[:, None] * inv_freq[None, :]
    cos = jnp.cos(ang)[None, :, None, :]
    sin = jnp.sin(ang)[None, :, None, :]
    xf = x.astype(jnp.float32)
    x1, x2 = xf[..., :half], xf[..., half:]
    return jnp.concatenate([x1 * cos - x2 * sin, x2 * cos + x1 * sin], axis=-1).astype(x.dtype)


def causal_depthwise_conv(x, w):
    k_width, ch = w.shape
    return lax.conv_general_dilated(
        x, w[:, None, :].astype(x.dtype), window_strides=(1,), padding=[(k_width - 1, 0)],
        dimension_numbers=('NWC', 'WIO', 'NWC'), feature_group_count=ch)


def chunked_gated_delta_rule(q, k, v, g, beta):
    B, T, H, Dk = q.shape
    Dv = v.shape[-1]
    C = GDN_CHUNK
    N = T // C

    def to_chunks(a):
        return jnp.moveaxis(a.reshape((B, N, C, H) + a.shape[3:]), 3, 1)

    qc, kc, vc = to_chunks(q), to_chunks(k), to_chunks(v)
    bc = to_chunks(beta)
    gc = jnp.cumsum(to_chunks(g), axis=-1)
    idx = jnp.arange(C)
    causal = idx[:, None] >= idx[None, :]
    strict = idx[:, None] > idx[None, :]
    diff = gc[..., :, None] - gc[..., None, :]
    decay = jnp.where(causal, jnp.exp(jnp.where(causal, diff, 0.0)), 0.0)
    kb = kc * bc[..., None]
    m = jnp.where(strict, jnp.einsum('bhnid,bhnjd->bhnij', kb, kc) * decay, 0.0)
    lhs = m + jnp.eye(C, dtype=m.dtype)
    rhs = jnp.concatenate([vc * bc[..., None], kb * jnp.exp(gc)[..., None]], axis=-1)
    sol = lax.linalg.triangular_solve(lhs, rhs, left_side=True, lower=True)
    u, w = sol[..., :Dv], sol[..., Dv:]
    a_intra = jnp.einsum('bhnid,bhnjd->bhnij', qc, kc) * decay

    def step(state, inp):
        q_i, k_i, u_i, w_i, g_i, a_i = inp
        v_new = u_i - jnp.einsum('bhck,bhkv->bhcv', w_i, state)
        o_i = (jnp.einsum('bhck,bhkv->bhcv', q_i * jnp.exp(g_i)[..., None], state)
               + jnp.einsum('bhij,bhjv->bhiv', a_i, v_new))
        g_last = g_i[..., -1:]
        state = (state * jnp.exp(g_last)[..., None]
                 + jnp.einsum('bhck,bhcv->bhkv', k_i * jnp.exp(g_last - g_i)[..., None], v_new))
        return state, o_i

    xs = tuple(jnp.moveaxis(a, 2, 0) for a in (qc, kc, u, w, gc, a_intra))
    state0 = jnp.zeros((B, H, Dk, Dv), jnp.float32)
    _, o = lax.scan(step, state0, xs)
    o = jnp.moveaxis(o, 0, 2)
    return jnp.moveaxis(o, 1, 3).reshape(B, T, H, Dv)


def gdn_mixer(x, w_in, conv_w, a_log, dt_bias, norm_w, w_out):
    B, T, _ = x.shape
    Hk, Hv, Dh = GDN_QK_HEADS, GDN_V_HEADS, GDN_HEAD_DIM
    qkv = dense(x, w_in[:, :GDN_CONV_CH])
    z = dense(x, w_in[:, GDN_CONV_CH:GDN_CONV_CH + GDN_V_WIDTH])
    a_raw, b_raw = jnp.split(dense(x, w_in[:, GDN_CONV_CH + GDN_V_WIDTH:]), [Hv], axis=-1)
    qkv = jax.nn.silu(causal_depthwise_conv(qkv, conv_w)).astype(jnp.float32)
    q, k, v = jnp.split(qkv, [GDN_QK_WIDTH, 2 * GDN_QK_WIDTH], axis=-1)
    rep = Hv // Hk
    q = jnp.repeat(l2_normalize(q.reshape(B, T, Hk, Dh)), rep, axis=2) * Dh ** -0.5
    k = jnp.repeat(l2_normalize(k.reshape(B, T, Hk, Dh)), rep, axis=2)
    v = v.reshape(B, T, Hv, Dh)
    g = -jnp.exp(a_log.astype(jnp.float32)) * jax.nn.softplus(
        a_raw.astype(jnp.float32) + dt_bias.astype(jnp.float32))
    beta = jax.nn.sigmoid(b_raw.astype(jnp.float32))
    o = chunked_gated_delta_rule(q, k, v, g, beta)
    o = (o * lax.rsqrt(jnp.mean(o * o, axis=-1, keepdims=True) + RMS_EPS) * norm_w.astype(jnp.float32)
         * jax.nn.silu(z.astype(jnp.float32).reshape(B, T, Hv, Dh)))
    return dense(o.reshape(B, T, GDN_V_WIDTH).astype(x.dtype), w_out).astype(x.dtype)


def compress_blocks(a, pe, w1, w2):
    B, T, G, D = a.shape
    n_cmp = (T - CMP_BLOCK) // CMP_STRIDE + 1
    idx = np.arange(n_cmp)[:, None] * CMP_STRIDE + np.arange(CMP_BLOCK)[None, :]
    blocks = a[:, idx] + pe[None, None, :, None, :]
    flat = jnp.moveaxis(blocks, 3, 2).reshape(B, n_cmp, G, CMP_BLOCK * D)
    return jax.nn.silu(flat @ w1) @ w2


def nsa_shared_kv(h, kv_w, cmp_pe, cmp_w1, cmp_w2):
    B, T, _ = h.shape
    kv = dense(h, kv_w).reshape(B, T, 6, NSA_KV_GROUPS, NSA_HEAD_DIM)
    pos = jnp.arange(T)
    k_cmp = compress_blocks(kv[:, :, 0], cmp_pe[0], cmp_w1[0], cmp_w2[0])
    v_cmp = compress_blocks(kv[:, :, 1], cmp_pe[1], cmp_w1[1], cmp_w2[1])
    k_slc = rope(kv[:, :, 2], pos)
    v_slc = kv[:, :, 3]
    k_win = rope(kv[:, :, 4], pos)
    v_win = kv[:, :, 5]
    return (k_cmp, v_cmp, k_slc, v_slc, k_win, v_win)


def compression_overlap(n_cmp, n_slc):
    c0 = np.arange(n_cmp) * CMP_STRIDE
    s0 = np.arange(n_slc) * SLC_BLOCK
    ov = (np.minimum(c0[:, None] + CMP_BLOCK, s0[None, :] + SLC_BLOCK)
          - np.maximum(c0[:, None], s0[None, :]))
    return jnp.asarray(np.clip(ov, 0, None) / CMP_BLOCK, dtype=jnp.float32)


def nsa_compressed(q, k_cmp, v_cmp):
    B, T, G, R, D = q.shape
    n_cmp = k_cmp.shape[1]
    s = jnp.einsum('btgrd,bcgd->bgrtc', q, k_cmp).astype(jnp.float32) * D ** -0.5
    block_end = jnp.arange(n_cmp) * CMP_STRIDE + CMP_BLOCK - 1
    visible = block_end[None, :] <= jnp.arange(T)[:, None]
    p = jax.nn.softmax(jnp.where(visible, s, NEG_INF), axis=-1) * visible
    o = jnp.einsum('bgrtc,bcgd->btgrd', p, v_cmp)
    p_slc = jnp.einsum('bgtc,cs->bgts', p.sum(axis=2), compression_overlap(n_cmp, T // SLC_BLOCK))
    return o, p_slc


def nsa_selected(q, k, v, p_slc):
    B, T, G, R, D = q.shape
    L = SLC_BLOCK
    n_slc = T // L
    n_sel = min(SLC_TOPK, n_slc)
    cur = (jnp.arange(T) // L)[:, None]
    blk = jnp.arange(n_slc)[None, :]
    causal_blk = blk <= cur
    forced = (blk == 0) | (causal_blk & (blk > cur - SLC_LOCAL))
    score = jnp.where(causal_blk, jnp.where(forced, FORCE_SCORE, p_slc), -1.0)
    top_score, top_idx = lax.top_k(score, n_sel)
    top_valid = top_score >= 0.0
    picked = jax.nn.one_hot(top_idx, n_slc, dtype=jnp.float32) * top_valid[..., None]
    sel = (picked.sum(axis=-2) > 0).astype(jnp.bfloat16)
    return masked_attention(q, k, v, sel)


def nsa_window(q, k, v):
    return masked_attention(q, k, v, None)


def _attn_kernel(*refs, windowed):
    if windowed:
        q_ref, k_ref, v_ref, o_ref, m_s, l_s, acc_s = refs
    else:
        q_ref, k_ref, v_ref, sel_ref, o_ref, m_s, l_s, acc_s = refs
    i = pl.program_id(2)
    tq, tk, D = ATTN_TILE, ATTN_TILE, NSA_HEAD_DIM
    m_s[...] = jnp.full(m_s.shape, NEG_INF, jnp.float32)
    l_s[...] = jnp.zeros(l_s.shape, jnp.float32)
    acc_s[...] = jnp.zeros(acc_s.shape, jnp.float32)
    tpos = i * tq + lax.broadcasted_iota(jnp.int32, (tq, tk), 0)

    def body(jj, carry):
        j = i - jj
        start = pl.multiple_of(j * tk, tk)
        kblk = k_ref[0, pl.ds(start, tk), :]
        vblk = v_ref[0, pl.ds(start, tk), :]
        kpos = j * tk + lax.broadcasted_iota(jnp.int32, (tq, tk), 1)
        ok = kpos <= tpos
        if windowed:
            ok = ok & (kpos > tpos - WINDOW)
        else:
            blk_of_key = j * (tk // SLC_BLOCK) + lax.broadcasted_iota(
                jnp.int32, (T_BLOCKS, tk), 1) // SLC_BLOCK
            expand = (lax.broadcasted_iota(jnp.int32, (T_BLOCKS, tk), 0) == blk_of_key)
            selm = jnp.dot(sel_ref[0, 0], expand.astype(jnp.bfloat16),
                           preferred_element_type=jnp.float32)
            ok = ok & (selm > 0.5)
        for r in range(NSA_HEADS_PER_GROUP):
            q_r = q_ref[0, :, r * D:(r + 1) * D]
            s = lax.dot_general(q_r, kblk, (((1,), (1,)), ((), ())),
                                preferred_element_type=jnp.float32)
            s = jnp.where(ok, s, NEG_INF)
            m_prev = m_s[r]
            m_new = jnp.maximum(m_prev, jnp.max(s, axis=-1, keepdims=True))
            alpha = jnp.exp(m_prev - m_new)
            p = jnp.exp(s - m_new)
            l_s[r] = alpha * l_s[r] + jnp.sum(p, axis=-1, keepdims=True)
            acc_s[r] = alpha * acc_s[r] + jnp.dot(p.astype(jnp.bfloat16), vblk,
                                                  preferred_element_type=jnp.float32)
            m_s[r] = m_new
        return carry

    n_tiles = jnp.minimum(i, WINDOW // tk) + 1 if windowed else i + 1
    lax.fori_loop(0, n_tiles, body, 0)
    for r in range(NSA_HEADS_PER_GROUP):
        o_ref[0, :, r * D:(r + 1) * D] = acc_s[r] / l_s[r]


def masked_attention(q, k, v, sel):
    B, T, G, R, D = q.shape
    tq = ATTN_TILE
    windowed = sel is None
    qs = (q * D ** -0.5).astype(jnp.bfloat16).reshape(B, T, G * R * D)
    kb = k.astype(jnp.bfloat16).reshape(B, T, G * D)
    vb = v.astype(jnp.bfloat16).reshape(B, T, G * D)
    in_specs = [pl.BlockSpec((1, tq, R * D), lambda b, g, i: (b, i, g)),
                pl.BlockSpec((1, T, D), lambda b, g, i: (b, 0, g)),
                pl.BlockSpec((1, T, D), lambda b, g, i: (b, 0, g))]
    args = [qs, kb, vb]
    if not windowed:
        in_specs.append(pl.BlockSpec((1, 1, tq, T // SLC_BLOCK), lambda b, g, i: (b, g, i, 0)))
        args.append(sel)
    o = pl.pallas_call(
        functools.partial(_attn_kernel, windowed=windowed),
        grid=(B, G, T // tq),
        in_specs=in_specs,
        out_specs=pl.BlockSpec((1, tq, R * D), lambda b, g, i: (b, i, g)),
        out_shape=jax.ShapeDtypeStruct((B, T, G * R * D), jnp.float32),
        scratch_shapes=[pltpu.VMEM((R, tq, 1), jnp.float32),
                        pltpu.VMEM((R, tq, 1), jnp.float32),
                        pltpu.VMEM((R, tq, D), jnp.float32)],
        compiler_params=pltpu.CompilerParams(
            dimension_semantics=("parallel", "parallel", "arbitrary"), vmem_limit_bytes=VMEM_LIMIT),
        name="window_attention" if windowed else "selected_attention",
    )(*args)
    return o.reshape(B, T, G, R, D)


def nsa_mixer(x, w_q, w_out, k_cmp, v_cmp, k_slc, v_slc, k_win, v_win):
    B, T, _ = x.shape
    G, R, D = NSA_KV_GROUPS, NSA_HEADS_PER_GROUP, NSA_HEAD_DIM
    q = dense(x, w_q[:, :NSA_Q_WIDTH]).reshape(B, T, NSA_HEADS, D)
    gates = jax.nn.sigmoid(dense(x, w_q[:, NSA_Q_WIDTH:])).reshape(B, T, 3, G, R, 1)
    q_rot = rope(q, jnp.arange(T)).reshape(B, T, G, R, D)
    o_cmp, p_slc = nsa_compressed(q.reshape(B, T, G, R, D), k_cmp, v_cmp)
    o_slc = nsa_selected(q_rot, k_slc, v_slc, p_slc)
    o_win = nsa_window(q_rot, k_win, v_win)
    o = gates[:, :, 0] * o_cmp + gates[:, :, 1] * o_slc + gates[:, :, 2] * o_win
    return dense(o.reshape(B, T, NSA_Q_WIDTH).astype(x.dtype), w_out).astype(x.dtype)


def kernel(x, a_w_in, a_conv_w, a_a_log, a_dt_bias, a_norm_w, a_w_out, kv_w, cmp_pe, cmp_w1, cmp_w2,
           b_w_q, b_w_out, router_w, router_bias, moe_w_gate, moe_w_up, moe_w_down, ln_g, ln_b):
    shared_kv = None
    for layer in range(DEPTH):
        if layer < N_A_LAYERS:
            mix = gdn_mixer(x, a_w_in[layer], a_conv_w[layer], a_a_log[layer], a_dt_bias[layer],
                            a_norm_w[layer], a_w_out[layer])
        else:
            if shared_kv is None:
                shared_kv = nsa_shared_kv(x, kv_w, cmp_pe, cmp_w1, cmp_w2)
            j = layer - N_A_LAYERS
            mix = nsa_mixer(x, b_w_q[j], b_w_out[j], *shared_kv)
        x = layer_norm(DEEPNORM_ALPHA * x + mix, ln_g[layer, 0], ln_b[layer, 0]).astype(x.dtype)
        ffn = moe_ffn(x, router_w, router_bias, moe_w_gate[layer], moe_w_up[layer], moe_w_down[layer])
        x = layer_norm(DEEPNORM_ALPHA * x + ffn, ln_g[layer, 1], ln_b[layer, 1]).astype(x.dtype)
    return x
```

```python
import functools
import math

import jax
import jax.numpy as jnp
import numpy as np
from jax import lax
from jax.experimental import pallas as pl
from jax.experimental.pallas import tpu as pltpu

D_MODEL = 2048
BATCH = 2
SEQ = 4096
DEPTH = 2
N_A_LAYERS = DEPTH // 2

GDN_HEAD_DIM = 128
GDN_QK_HEADS = D_MODEL // GDN_HEAD_DIM
GDN_V_HEADS = 2 * GDN_QK_HEADS
GDN_QK_WIDTH = GDN_QK_HEADS * GDN_HEAD_DIM
GDN_V_WIDTH = GDN_V_HEADS * GDN_HEAD_DIM
GDN_CONV_CH = 2 * GDN_QK_WIDTH + GDN_V_WIDTH
GDN_CHUNK = 64
GDN_CONV = 4
GDN_TILE = 256

NSA_HEAD_DIM = 128
NSA_HEADS = D_MODEL // NSA_HEAD_DIM
NSA_KV_GROUPS = 4
NSA_HEADS_PER_GROUP = NSA_HEADS // NSA_KV_GROUPS
NSA_Q_WIDTH = NSA_HEADS * NSA_HEAD_DIM
CMP_BLOCK = 32
CMP_STRIDE = 16
SLC_BLOCK = 64
SLC_TOPK = 16
SLC_LOCAL = 2
WINDOW = 512
WIN_Q_BLOCK = 128
SLC_Q_BLOCK = 64
ROPE_THETA = 10000.0

N_EXPERTS = 32
N_GROUPS = 8
EXPERTS_PER_GROUP = N_EXPERTS // N_GROUPS
TOP_K = 2
D_EXPERT = D_MODEL // 4

DEEPNORM_ALPHA = (2 * DEPTH) ** 0.25
LN_EPS = 1e-5
RMS_EPS = 1e-6
NEG_INF = -1e30
FORCE_SCORE = 1e6

LANE = 128
VMEM_LIMIT = 48 * 1024 * 1024
MOE_TILE = 256
ATTN_TILE = 256
T_BLOCKS = SEQ // SLC_BLOCK


def _mm_kernel(a_ref, b_ref, o_ref):
    o_ref[...] = jnp.dot(a_ref[...], b_ref[...], preferred_element_type=jnp.float32)


def pmatmul(a, b, tm=512, tn=1024):
    m, k = a.shape
    n = b.shape[1]
    n_pad = -n % LANE
    if n_pad:
        b = jnp.pad(b, ((0, 0), (0, n_pad)))
    np_ = n + n_pad
    tn = math.gcd(np_, tn)
    tm = math.gcd(m, tm)
    out = pl.pallas_call(
        _mm_kernel,
        grid=(m // tm, np_ // tn),
        in_specs=[pl.BlockSpec((tm, k), lambda i, j: (i, 0)),
                  pl.BlockSpec((k, tn), lambda i, j: (0, j))],
        out_specs=pl.BlockSpec((tm, tn), lambda i, j: (i, j)),
        out_shape=jax.ShapeDtypeStruct((m, np_), jnp.float32),
        compiler_params=pltpu.CompilerParams(
            dimension_semantics=("parallel", "arbitrary"), vmem_limit_bytes=VMEM_LIMIT),
        name="dense_matmul",
    )(a.astype(jnp.bfloat16), b.astype(jnp.bfloat16))
    return out[:, :n] if n_pad else out


def dense(x, w):
    lead = x.shape[:-1]
    return pmatmul(x.reshape(-1, x.shape[-1]), w).reshape(lead + (w.shape[-1],))


def _moe_kernel(tile_expert_ref, n_tiles_ref, tok_ref, dst_ref, x_hbm, gate_ref, wg_ref, wu_ref, wd_ref,
                out_hbm, xbuf, ybuf, sem_in, sem_out):
    del tile_expert_ref
    i = pl.program_id(0)

    def row_in(r, src_row):
        return pltpu.make_async_copy(x_hbm.at[pl.ds(src_row, 1)], xbuf.at[pl.ds(r, 1)], sem_in)

    def row_out(r, dst_row):
        return pltpu.make_async_copy(ybuf.at[pl.ds(r, 1)], out_hbm.at[pl.ds(dst_row, 1)], sem_out)

    @pl.when(i < n_tiles_ref[0])
    def _():
        def start_in(r, c):
            row_in(r, tok_ref[0, 0, r]).start()
            return c

        def wait_in(r, c):
            row_in(r, tok_ref[0, 0, r]).wait()
            return c

        lax.fori_loop(0, MOE_TILE, start_in, 0)
        lax.fori_loop(0, MOE_TILE, wait_in, 0)
        x = xbuf[...].astype(jnp.bfloat16)
        g = jnp.dot(x, wg_ref[0], preferred_element_type=jnp.float32)
        u = jnp.dot(x, wu_ref[0], preferred_element_type=jnp.float32)
        h = (g * jax.nn.sigmoid(g)) * u * gate_ref[...]
        ybuf[...] = jnp.dot(h.astype(jnp.bfloat16), wd_ref[0], preferred_element_type=jnp.float32)

        def start_out(r, c):
            d = dst_ref[0, 0, r]

            @pl.when(d >= 0)
            def _():
                row_out(r, d).start()
            return c

        def wait_out(r, c):
            d = dst_ref[0, 0, r]

            @pl.when(d >= 0)
            def _():
                row_out(r, d).wait()
            return c

        lax.fori_loop(0, MOE_TILE, start_out, 0)
        lax.fori_loop(0, MOE_TILE, wait_out, 0)


def moe_ffn(x, router_w, router_bias, w_gate, w_up, w_down):
    B, T, D = x.shape
    n_tok = B * T
    h = x.reshape(n_tok, D)
    aff = jax.nn.sigmoid(jnp.dot(h, router_w).astype(jnp.float32))
    biased = (aff + router_bias.astype(jnp.float32)).reshape(-1, N_GROUPS, EXPERTS_PER_GROUP)
    group_score = lax.top_k(biased, TOP_K)[0].sum(axis=-1)
    best_group = jnp.argmax(group_score, axis=-1)
    in_group = jnp.arange(N_GROUPS)[None, :] == best_group[:, None]
    cand = jnp.where(in_group[:, :, None], biased, NEG_INF).reshape(-1, N_EXPERTS)
    _, top_idx = lax.top_k(cand, TOP_K)
    top_aff = jnp.take_along_axis(aff, top_idx, axis=-1)
    top_w = top_aff / jnp.sum(top_aff, axis=-1, keepdims=True)

    n_asg = n_tok * TOP_K
    max_tiles = n_asg // MOE_TILE + N_EXPERTS
    n_rows = max_tiles * MOE_TILE
    e_flat = top_idx.reshape(-1).astype(jnp.int32)
    order = jnp.argsort(e_flat, stable=True).astype(jnp.int32)
    e_sorted = e_flat[order]
    counts = jnp.zeros((N_EXPERTS,), jnp.int32).at[e_flat].add(1)
    tiles_per = (counts + MOE_TILE - 1) // MOE_TILE
    tile_end = jnp.cumsum(tiles_per)
    row_start = (tile_end - tiles_per) * MOE_TILE
    first = jnp.cumsum(counts) - counts
    row_of_sorted = row_start[e_sorted] + (jnp.arange(n_asg, dtype=jnp.int32) - first[e_sorted])
    asg_of_row = jnp.full((n_rows,), -1, jnp.int32).at[row_of_sorted].set(order)
    token_of_row = jnp.maximum(asg_of_row, 0) // TOP_K
    gate_of_row = jnp.where(asg_of_row >= 0, top_w.reshape(-1)[jnp.maximum(asg_of_row, 0)], 0.0)
    n_tiles = tile_end[-1:].astype(jnp.int32)
    tile_expert = jnp.minimum(
        jnp.searchsorted(tile_end, jnp.arange(max_tiles, dtype=jnp.int32), side="right"),
        N_EXPERTS - 1).astype(jnp.int32)
    tile_expert = jnp.where(jnp.arange(max_tiles) < n_tiles[0], tile_expert,
                            tile_expert[jnp.maximum(n_tiles[0] - 1, 0)])

    wspec_in = pl.BlockSpec((1, D, D_EXPERT), lambda i, te, nt: (te[i], 0, 0))
    idx_spec = pl.BlockSpec((1, 1, MOE_TILE), lambda i, te, nt: (i, 0, 0), memory_space=pltpu.SMEM)
    y = pl.pallas_call(
        _moe_kernel,
        grid_spec=pltpu.PrefetchScalarGridSpec(
            num_scalar_prefetch=2,
            grid=(max_tiles,),
            in_specs=[idx_spec, idx_spec,
                      pl.BlockSpec(memory_space=pl.ANY),
                      pl.BlockSpec((MOE_TILE, 1), lambda i, te, nt: (i, 0)),
                      wspec_in, wspec_in,
                      pl.BlockSpec((1, D_EXPERT, D), lambda i, te, nt: (te[i], 0, 0))],
            out_specs=pl.BlockSpec(memory_space=pl.ANY),
            scratch_shapes=[pltpu.VMEM((MOE_TILE, D), jnp.float32),
                            pltpu.VMEM((MOE_TILE, D), jnp.float32),
                            pltpu.SemaphoreType.DMA(()), pltpu.SemaphoreType.DMA(())],
        ),
        out_shape=jax.ShapeDtypeStruct((n_asg, D), jnp.float32),
        compiler_params=pltpu.CompilerParams(
            dimension_semantics=("arbitrary",), vmem_limit_bytes=VMEM_LIMIT),
        name="routed_moe",
    )(tile_expert, n_tiles, token_of_row.reshape(max_tiles, 1, MOE_TILE),
      asg_of_row.reshape(max_tiles, 1, MOE_TILE), h, gate_of_row[:, None],
      w_gate.astype(jnp.bfloat16), w_up.astype(jnp.bfloat16), w_down.astype(jnp.bfloat16))
    return y.reshape(B, T, TOP_K, D).sum(axis=2).astype(x.dtype)


def layer_norm(x, g, b):
    xf = x.astype(jnp.float32)
    mu = jnp.mean(xf, axis=-1, keepdims=True)
    var = jnp.mean(jnp.square(xf - mu), axis=-1, keepdims=True)
    return (xf - mu) * lax.rsqrt(var + LN_EPS) * g.astype(jnp.float32) + b.astype(jnp.float32)


def l2_normalize(a):
    return a * lax.rsqrt(jnp.sum(a * a, axis=-1, keepdims=True) + RMS_EPS)


def rope(x, pos):
    half = x.shape[-1] // 2
    inv_freq = ROPE_THETA ** (-jnp.arange(half, dtype=jnp.float32) / half)
    ang = pos.astype(jnp.float32)[:, None] * inv_freq[None, :]
    cos = jnp.cos(ang)[None, :, None, :]
    sin = jnp.sin(ang)[None, :, None, :]
    xf = x.astype(jnp.float32)
    x1, x2 = xf[..., :half], xf[..., half:]
    return jnp.concatenate([x1 * cos - x2 * sin, x2 * cos + x1 * sin], axis=-1).astype(x.dtype)


def causal_depthwise_conv(x, w):
    k_width, ch = w.shape
    return lax.conv_general_dilated(
        x, w[:, None, :].astype(x.dtype), window_strides=(1,), padding=[(k_width - 1, 0)],
        dimension_numbers=('NWC', 'WIO', 'NWC'), feature_group_count=ch)


def _softplus(x):
    return jnp.maximum(x, 0.0) + jnp.log1p(jnp.exp(-jnp.abs(x)))


def _bdot(a, b):
    return jnp.dot(a.astype(jnp.bfloat16), b.astype(jnp.bfloat16), preferred_element_type=jnp.float32)


def _bdot_nt(a, b):
    return lax.dot_general(a.astype(jnp.bfloat16), b.astype(jnp.bfloat16), (((1,), (1,)), ((), ())),
                           preferred_element_type=jnp.float32)


def _bdot_tn(a, b):
    return lax.dot_general(a.astype(jnp.bfloat16), b.astype(jnp.bfloat16), (((0,), (0,)), ((), ())),
                           preferred_element_type=jnp.float32)


def _gdn_prep_kernel(x_ref, halo_ref, w_ref, o_ref):
    i = pl.program_id(1)
    j = pl.program_id(2)
    tt = x_ref.shape[1]
    x = x_ref[0]
    halo = jnp.where(i == 0, 0.0, halo_ref[0])
    xx = jnp.concatenate([halo, x], axis=0)
    w = w_ref[...]
    y = w[3:4] * x
    for tap in range(GDN_CONV - 1):
        lo = 8 - (GDN_CONV - 1) + tap
        y = y + w[tap:tap + 1] * xx[lo:lo + tt]
    y = y * jax.nn.sigmoid(y)
    q_blocks = GDN_QK_WIDTH // x_ref.shape[2]

    @pl.when(j >= 2 * q_blocks)
    def _():
        o_ref[0] = y

    @pl.when(j < 2 * q_blocks)
    def _():
        scale = jnp.where(j < q_blocks, GDN_HEAD_DIM ** -0.5, 1.0)
        for h in range(x_ref.shape[2] // GDN_HEAD_DIM):
            yh = y[:, h * GDN_HEAD_DIM:(h + 1) * GDN_HEAD_DIM]
            inv = lax.rsqrt(jnp.sum(yh * yh, axis=-1, keepdims=True) + RMS_EPS)
            o_ref[0, :, h * GDN_HEAD_DIM:(h + 1) * GDN_HEAD_DIM] = yh * inv * scale


def gdn_prep(qkv, conv_w):
    B, T, CH = qkv.shape
    tt, tc = GDN_TILE, 512
    return pl.pallas_call(
        _gdn_prep_kernel,
        grid=(B, T // tt, CH // tc),
        in_specs=[pl.BlockSpec((1, tt, tc), lambda b, i, j: (b, i, j)),
                  pl.BlockSpec((1, 8, tc), lambda b, i, j: (b, jnp.maximum(i * (tt // 8) - 1, 0), j)),
                  pl.BlockSpec((GDN_CONV, tc), lambda b, i, j: (0, j))],
        out_specs=pl.BlockSpec((1, tt, tc), lambda b, i, j: (b, i, j)),
        out_shape=jax.ShapeDtypeStruct((B, T, CH), jnp.float32),
        compiler_params=pltpu.CompilerParams(
            dimension_semantics=("parallel", "parallel", "parallel"), vmem_limit_bytes=VMEM_LIMIT),
        name="gdn_conv_silu_norm",
    )(qkv, qkv, conv_w)


def _gdn_kernel(q_ref, k_ref, v_ref, z_ref, acol_ref, arow_ref, bcol_ref, alog_ref, dt_ref, nw_ref,
                o_ref, state_s):
    C, Dh = GDN_CHUNK, GDN_HEAD_DIM

    @pl.when(pl.program_id(2) == 0)
    def _():
        state_s[...] = jnp.zeros(state_s.shape, jnp.float32)

    row = lax.broadcasted_iota(jnp.int32, (C, C), 0)
    col = lax.broadcasted_iota(jnp.int32, (C, C), 1)
    causal = row >= col
    strict = row > col
    nw = nw_ref[...]
    probs = []
    for n in range(GDN_TILE // C):
        sl = slice(n * C, (n + 1) * C)
        q = q_ref[0, sl, :]
        k = k_ref[0, sl, :]
        kk = _bdot_nt(k, k)
        qk = _bdot_nt(q, k)
        for hh in range(2):
            neg_a = -jnp.exp(alog_ref[0, :, hh:hh + 1])
            dt = dt_ref[0, :, hh:hh + 1]
            g_col = neg_a * _softplus(acol_ref[0, 0, sl, hh:hh + 1] + dt)
            g_row = neg_a * _softplus(arow_ref[0, 0, hh:hh + 1, sl] + dt)
            beta = jax.nn.sigmoid(bcol_ref[0, 0, sl, hh:hh + 1])
            gc_col = jnp.sum(jnp.where(causal, g_row, 0.0), axis=1, keepdims=True)
            gc_row = jnp.sum(jnp.where(row <= col, g_col, 0.0), axis=0, keepdims=True)
            g_last = jnp.sum(g_row, axis=1, keepdims=True)
            decay = jnp.where(causal, jnp.exp(jnp.where(causal, gc_col - gc_row, 0.0)), 0.0)
            m = jnp.where(strict, beta * kk * decay, 0.0)
            e_col = jnp.exp(gc_col)
            v = v_ref[0, sl, hh * Dh:(hh + 1) * Dh]
            probs.append(dict(
                sl=sl, hh=hh, x=-m, p=m, a=qk * decay,
                rhs=jnp.concatenate([beta * v, beta * e_col * k], axis=1),
                qe=q * e_col, ke=k * jnp.exp(g_last - gc_col), s_decay=jnp.exp(g_last)))
    for _ in range(int(math.log2(C)) - 1):
        for pr in probs:
            pr["p"] = _bdot(pr["p"], pr["p"])
        for pr in probs:
            pr["x"] = pr["x"] + pr["p"] + _bdot(pr["x"], pr["p"])
    for pr in probs:
        pr["sol"] = pr["rhs"] + _bdot(pr["x"], pr["rhs"])
    for pr in probs:
        sl, hh = pr["sl"], pr["hh"]
        u, w = pr["sol"][:, :Dh], pr["sol"][:, Dh:]
        state = state_s[hh]
        ws = _bdot(jnp.concatenate([w, pr["qe"]], axis=0), state)
        v_new = u - ws[:C]
        o = ws[C:] + _bdot(pr["a"], v_new)
        state_s[hh] = state * pr["s_decay"] + _bdot_tn(pr["ke"], v_new)
        z = z_ref[0, sl, hh * Dh:(hh + 1) * Dh]
        o = o * lax.rsqrt(jnp.mean(o * o, axis=-1, keepdims=True) + RMS_EPS) * nw
        o_ref[0, sl, hh * Dh:(hh + 1) * Dh] = (o * (z * jax.nn.sigmoid(z))).astype(o_ref.dtype)


def gdn_core(qkv, z, a_raw, b_raw, a_log, dt_bias, norm_w):
    B, T, _ = qkv.shape
    Hk, Dh, tt = GDN_QK_HEADS, GDN_HEAD_DIM, GDN_TILE
    a4 = a_raw.reshape(B, T, Hk, 2)
    a_col = a4.transpose(0, 2, 1, 3)
    a_row = a4.transpose(0, 2, 3, 1)
    b_col = b_raw.reshape(B, T, Hk, 2).transpose(0, 2, 1, 3)
    col_spec = pl.BlockSpec((1, 1, tt, 2), lambda b, h, c: (b, h, c, 0))
    head_spec = pl.BlockSpec((1, 1, 2), lambda b, h, c: (h, 0, 0))
    return pl.pallas_call(
        _gdn_kernel,
        grid=(B, Hk, T // tt),
        in_specs=[pl.BlockSpec((1, tt, Dh), lambda b, h, c: (b, c, h)),
                  pl.BlockSpec((1, tt, Dh), lambda b, h, c: (b, c, Hk + h)),
                  pl.BlockSpec((1, tt, 2 * Dh), lambda b, h, c: (b, c, Hk + h)),
                  pl.BlockSpec((1, tt, 2 * Dh), lambda b, h, c: (b, c, h)),
                  col_spec,
                  pl.BlockSpec((1, 1, 2, tt), lambda b, h, c: (b, h, 0, c)),
                  col_spec, head_spec, head_spec,
                  pl.BlockSpec((1, Dh), lambda b, h, c: (0, 0))],
        out_specs=pl.BlockSpec((1, tt, 2 * Dh), lambda b, h, c: (b, c, h)),
        out_shape=jax.ShapeDtypeStruct((B, T, GDN_V_WIDTH), jnp.bfloat16),
        scratch_shapes=[pltpu.VMEM((2, Dh, Dh), jnp.float32)],
        compiler_params=pltpu.CompilerParams(
            dimension_semantics=("parallel", "parallel", "arbitrary"), vmem_limit_bytes=VMEM_LIMIT),
        name="gated_delta_rule",
    )(qkv, qkv, qkv, z, a_col, a_row, b_col,
      a_log.astype(jnp.float32).reshape(Hk, 1, 2), dt_bias.astype(jnp.float32).reshape(Hk, 1, 2),
      norm_w.astype(jnp.float32).reshape(1, Dh))


def gdn_mixer(x, w_in, conv_w, a_log, dt_bias, norm_w, w_out):
    B, T, _ = x.shape
    Hk, Hv, Dh = GDN_QK_HEADS, GDN_V_HEADS, GDN_HEAD_DIM
    qkv = dense(x, w_in[:, :GDN_CONV_CH])
    z = dense(x, w_in[:, GDN_CONV_CH:GDN_CONV_CH + GDN_V_WIDTH])
    a_raw, b_raw = jnp.split(dense(x, w_in[:, GDN_CONV_CH + GDN_V_WIDTH:]), [Hv], axis=-1)
    o = gdn_core(gdn_prep(qkv, conv_w), z, a_raw, b_raw, a_log, dt_bias, norm_w)
    return dense(o, w_out).astype(x.dtype)


def compress_blocks(a, pe, w1, w2):
    B, T, G, D = a.shape
    n_cmp = (T - CMP_BLOCK) // CMP_STRIDE + 1
    idx = np.arange(n_cmp)[:, None] * CMP_STRIDE + np.arange(CMP_BLOCK)[None, :]
    blocks = a[:, idx] + pe[None, None, :, None, :]
    flat = jnp.moveaxis(blocks, 3, 2).reshape(B, n_cmp, G, CMP_BLOCK * D)
    return jax.nn.silu(flat @ w1) @ w2


def nsa_shared_kv(h, kv_w, cmp_pe, cmp_w1, cmp_w2):
    B, T, _ = h.shape
    kv = dense(h, kv_w).reshape(B, T, 6, NSA_KV_GROUPS, NSA_HEAD_DIM)
    pos = jnp.arange(T)
    k_cmp = compress_blocks(kv[:, :, 0], cmp_pe[0], cmp_w1[0], cmp_w2[0])
    v_cmp = compress_blocks(kv[:, :, 1], cmp_pe[1], cmp_w1[1], cmp_w2[1])
    k_slc = rope(kv[:, :, 2], pos)
    v_slc = kv[:, :, 3]
    k_win = rope(kv[:, :, 4], pos)
    v_win = kv[:, :, 5]
    return (k_cmp, v_cmp, k_slc, v_slc, k_win, v_win)


def compression_overlap(n_cmp, n_slc):
    c0 = np.arange(n_cmp) * CMP_STRIDE
    s0 = np.arange(n_slc) * SLC_BLOCK
    ov = (np.minimum(c0[:, None] + CMP_BLOCK, s0[None, :] + SLC_BLOCK)
          - np.maximum(c0[:, None], s0[None, :]))
    return jnp.asarray(np.clip(ov, 0, None) / CMP_BLOCK, dtype=jnp.float32)


def nsa_compressed(q, k_cmp, v_cmp):
    B, T, G, R, D = q.shape
    n_cmp = k_cmp.shape[1]
    s = jnp.einsum('btgrd,bcgd->bgrtc', q, k_cmp).astype(jnp.float32) * D ** -0.5
    block_end = jnp.arange(n_cmp) * CMP_STRIDE + CMP_BLOCK - 1
    visible = block_end[None, :] <= jnp.arange(T)[:, None]
    p = jax.nn.softmax(jnp.where(visible, s, NEG_INF), axis=-1) * visible
    o = jnp.einsum('bgrtc,bcgd->btgrd', p, v_cmp)
    p_slc = jnp.einsum('bgtc,cs->bgts', p.sum(axis=2), compression_overlap(n_cmp, T // SLC_BLOCK))
    return o, p_slc


def nsa_selected(q, k, v, p_slc):
    B, T, G, R, D = q.shape
    L = SLC_BLOCK
    n_slc = T // L
    n_sel = min(SLC_TOPK, n_slc)
    cur = (jnp.arange(T) // L)[:, None]
    blk = jnp.arange(n_slc)[None, :]
    causal_blk = blk <= cur
    forced = (blk == 0) | (causal_blk & (blk > cur - SLC_LOCAL))
    score = jnp.where(causal_blk, jnp.where(forced, FORCE_SCORE, p_slc), -1.0)
    top_score, top_idx = lax.top_k(score, n_sel)
    top_valid = top_score >= 0.0
    picked = jax.nn.one_hot(top_idx, n_slc, dtype=jnp.float32) * top_valid[..., None]
    sel = (picked.sum(axis=-2) > 0).astype(jnp.bfloat16)
    return masked_attention(q, k, v, sel)


def nsa_window(q, k, v):
    return masked_attention(q, k, v, None)


def _attn_kernel(*refs, windowed):
    if windowed:
        q_ref, k_ref, v_ref, o_ref, m_s, l_s, acc_s = refs
    else:
        q_ref, k_ref, v_ref, sel_ref, o_ref, m_s, l_s, acc_s = refs
    i = pl.program_id(2)
    tq, tk, D = ATTN_TILE, ATTN_TILE, NSA_HEAD_DIM
    m_s[...] = jnp.full(m_s.shape, NEG_INF, jnp.float32)
    l_s[...] = jnp.zeros(l_s.shape, jnp.float32)
    acc_s[...] = jnp.zeros(acc_s.shape, jnp.float32)
    tpos = i * tq + lax.broadcasted_iota(jnp.int32, (tq, tk), 0)

    def body(jj, carry):
        j = i - jj
        start = pl.multiple_of(j * tk, tk)
        kblk = k_ref[0, pl.ds(start, tk), :]
        vblk = v_ref[0, pl.ds(start, tk), :]
        kpos = j * tk + lax.broadcasted_iota(jnp.int32, (tq, tk), 1)
        ok = kpos <= tpos
        if windowed:
            ok = ok & (kpos > tpos - WINDOW)
        else:
            blk_of_key = j * (tk // SLC_BLOCK) + lax.broadcasted_iota(
                jnp.int32, (T_BLOCKS, tk), 1) // SLC_BLOCK
            expand = (lax.broadcasted_iota(jnp.int32, (T_BLOCKS, tk), 0) == blk_of_key)
            selm = jnp.dot(sel_ref[0, 0], expand.astype(jnp.bfloat16),
                           preferred_element_type=jnp.float32)
            ok = ok & (selm > 0.5)
        for r in range(NSA_HEADS_PER_GROUP):
            q_r = q_ref[0, :, r * D:(r + 1) * D]
            s = lax.dot_general(q_r, kblk, (((1,), (1,)), ((), ())),
                                preferred_element_type=jnp.float32)
            s = jnp.where(ok, s, NEG_INF)
            m_prev = m_s[r]
            m_new = jnp.maximum(m_prev, jnp.max(s, axis=-1, keepdims=True))
            alpha = jnp.exp(m_prev - m_new)
            p = jnp.exp(s - m_new)
            l_s[r] = alpha * l_s[r] + jnp.sum(p, axis=-1, keepdims=True)
            acc_s[r] = alpha * acc_s[r] + jnp.dot(p.astype(jnp.bfloat16), vblk,
                                                  preferred_element_type=jnp.float32)
            m_s[r] = m_new
        return carry

    n_tiles = jnp.minimum(i, WINDOW // tk) + 1 if windowed else i + 1
    lax.fori_loop(0, n_tiles, body, 0)
    for r in range(NSA_HEADS_PER_GROUP):
        o_ref[0, :, r * D:(r + 1) * D] = acc_s[r] / l_s[r]


def masked_attention(q, k, v, sel):
    B, T, G, R, D = q.shape
    tq = ATTN_TILE
    windowed = sel is None
    qs = (q * D ** -0.5).astype(jnp.bfloat16).reshape(B, T, G * R * D)
    kb = k.astype(jnp.bfloat16).reshape(B, T, G * D)
    vb = v.astype(jnp.bfloat16).reshape(B, T, G * D)
    in_specs = [pl.BlockSpec((1, tq, R * D), lambda b, g, i: (b, i, g)),
                pl.BlockSpec((1, T, D), lambda b, g, i: (b, 0, g)),
                pl.BlockSpec((1, T, D), lambda b, g, i: (b, 0, g))]
    args = [qs, kb, vb]
    if not windowed:
        in_specs.append(pl.BlockSpec((1, 1, tq, T // SLC_BLOCK), lambda b, g, i: (b, g, i, 0)))
        args.append(sel)
    o = pl.pallas_call(
        functools.partial(_attn_kernel, windowed=windowed),
        grid=(B, G, T // tq),
        in_specs=in_specs,
        out_specs=pl.BlockSpec((1, tq, R * D), lambda b, g, i: (b, i, g)),
        out_shape=jax.ShapeDtypeStruct((B, T, G * R * D), jnp.float32),
        scratch_shapes=[pltpu.VMEM((R, tq, 1), jnp.float32),
                        pltpu.VMEM((R, tq, 1), jnp.float32),
                        pltpu.VMEM((R, tq, D), jnp.float32)],
        compiler_params=pltpu.CompilerParams(
            dimension_semantics=("parallel", "parallel", "arbitrary"), vmem_limit_bytes=VMEM_LIMIT),
        name="window_attention" if windowed else "selected_attention",
    )(*args)
    return o.reshape(B, T, G, R, D)


def nsa_mixer(x, w_q, w_out, k_cmp, v_cmp, k_slc, v_slc, k_win, v_win):
    B, T, _ = x.shape
    G, R, D = NSA_KV_GROUPS, NSA_HEADS_PER_GROUP, NSA_HEAD_DIM
    q = dense(x, w_q[:, :NSA_Q_WIDTH]).reshape(B, T, NSA_HEADS, D)
    gates = jax.nn.sigmoid(dense(x, w_q[:, NSA_Q_WIDTH:])).reshape(B, T, 3, G, R, 1)
    q_rot = rope(q, jnp.arange(T)).reshape(B, T, G, R, D)
    o_cmp, p_slc = nsa_compressed(q.reshape(B, T, G, R, D), k_cmp, v_cmp)
    o_slc = nsa_selected(q_rot, k_slc, v_slc, p_slc)
    o_win = nsa_window(q_rot, k_win, v_win)
    o = gates[:, :, 0] * o_cmp + gates[:, :, 1] * o_slc + gates[:, :, 2] * o_win
    return dense(o.reshape(B, T, NSA_Q_WIDTH).astype(x.dtype), w_out).astype(x.dtype)


def kernel(x, a_w_in, a_conv_w, a_a_log, a_dt_bias, a_norm_w, a_w_out, kv_w, cmp_pe, cmp_w1, cmp_w2,
           b_w_q, b_w_out, router_w, router_bias, moe_w_gate, moe_w_up, moe_w_down, ln_g, ln_b):
    shared_kv = None
    for layer in range(DEPTH):
        if layer < N_A_LAYERS:
            mix = gdn_mixer(x, a_w_in[layer], a_conv_w[layer], a_a_log[layer], a_dt_bias[layer],
                            a_norm_w[layer], a_w_out[layer])
        else:
            if shared_kv is None:
                shared_kv = nsa_shared_kv(x, kv_w, cmp_pe, cmp_w1, cmp_w2)
            j = layer - N_A_LAYERS
            mix = nsa_mixer(x, b_w_q[j], b_w_out[j], *shared_kv)
        x = layer_norm(DEEPNORM_ALPHA * x + mix, ln_g[layer, 0], ln_b[layer, 0]).astype(x.dtype)
        ffn = moe_ffn(x, router_w, router_bias, moe_w_gate[layer], moe_w_up[layer], moe_w_down[layer])
        x = layer_norm(DEEPNORM_ALPHA * x + ffn, ln_g[layer, 1], ln_b[layer, 1]).astype(x.dtype)
    return x
```

```python
import functools
import math

import jax
import jax.numpy as jnp
import numpy as np
from jax import lax
from jax.experimental import pallas as pl
from jax.experimental.pallas import tpu as pltpu

D_MODEL = 2048
BATCH = 2
SEQ = 4096
DEPTH = 2
N_A_LAYERS = DEPTH // 2

GDN_HEAD_DIM = 128
GDN_QK_HEADS = D_MODEL // GDN_HEAD_DIM
GDN_V_HEADS = 2 * GDN_QK_HEADS
GDN_QK_WIDTH = GDN_QK_HEADS * GDN_HEAD_DIM
GDN_V_WIDTH = GDN_V_HEADS * GDN_HEAD_DIM
GDN_CONV_CH = 2 * GDN_QK_WIDTH + GDN_V_WIDTH
GDN_CHUNK = 64
GDN_CONV = 4
GDN_TILE = 256

NSA_HEAD_DIM = 128
NSA_HEADS = D_MODEL // NSA_HEAD_DIM
NSA_KV_GROUPS = 4
NSA_HEADS_PER_GROUP = NSA_HEADS // NSA_KV_GROUPS
NSA_Q_WIDTH = NSA_HEADS * NSA_HEAD_DIM
CMP_BLOCK = 32
CMP_STRIDE = 16
SLC_BLOCK = 64
SLC_TOPK = 16
SLC_LOCAL = 2
WINDOW = 512
WIN_Q_BLOCK = 128
SLC_Q_BLOCK = 64
ROPE_THETA = 10000.0

N_EXPERTS = 32
N_GROUPS = 8
EXPERTS_PER_GROUP = N_EXPERTS // N_GROUPS
TOP_K = 2
D_EXPERT = D_MODEL // 4

DEEPNORM_ALPHA = (2 * DEPTH) ** 0.25
LN_EPS = 1e-5
RMS_EPS = 1e-6
NEG_INF = -1e30
FORCE_SCORE = 1e6

LANE = 128
VMEM_LIMIT = 48 * 1024 * 1024
MOE_VMEM_LIMIT = 56 * 1024 * 1024
MOE_TILE = 256
ATTN_TILE = 256
T_BLOCKS = SEQ // SLC_BLOCK


def _mm_kernel(a_ref, b_ref, o_ref, b16_s):
    @pl.when(pl.program_id(1) == 0)
    def _():
        b16_s[...] = b_ref[...].astype(jnp.bfloat16)

    o_ref[...] = jnp.dot(a_ref[...], b16_s[...], preferred_element_type=jnp.float32)


def pmatmul(a, b, col_start=0, n_cols=None, tm=512):
    m, k = a.shape
    n_cols = b.shape[1] - col_start if n_cols is None else n_cols
    if n_cols % LANE or col_start % LANE:
        b = jnp.pad(b[:, col_start:col_start + n_cols], ((0, 0), (0, -n_cols % LANE)))
        return pmatmul(a, b)[:, :n_cols]
    tn = math.gcd(n_cols, 1024 if k <= 2048 else 512)
    assert col_start % tn == 0 and m % tm == 0
    off = col_start // tn
    return pl.pallas_call(
        _mm_kernel,
        grid=(n_cols // tn, m // tm),
        in_specs=[pl.BlockSpec((tm, k), lambda j, i: (i, 0)),
                  pl.BlockSpec((k, tn), lambda j, i: (0, j + off))],
        out_specs=pl.BlockSpec((tm, tn), lambda j, i: (i, j)),
        out_shape=jax.ShapeDtypeStruct((m, n_cols), jnp.float32),
        scratch_shapes=[pltpu.VMEM((k, tn), jnp.bfloat16)],
        compiler_params=pltpu.CompilerParams(
            dimension_semantics=("parallel", "arbitrary"), vmem_limit_bytes=VMEM_LIMIT),
        name="dense_matmul",
    )(a.astype(jnp.bfloat16), b.astype(jnp.float32))


def dense(x, w, col_start=0, n_cols=None):
    lead = x.shape[:-1]
    out = pmatmul(x.reshape(-1, x.shape[-1]), w, col_start, n_cols)
    return out.reshape(lead + (out.shape[-1],))


def _moe_kernel(tile_expert_ref, n_tiles_ref, live_ref, tok_ref, tok_next_ref, dst_ref, x_hbm, gate_ref,
                wg_ref, wu_ref, wd_ref, out_hbm, xbuf, ybuf, wg16, wu16, wd16, sem_in, sem_out):
    i = pl.program_id(0)
    n_tiles = n_tiles_ref[0]
    slot = i % 2

    def start_gather(idx_ref, s):
        def body(r, c):
            pltpu.make_async_copy(x_hbm.at[pl.ds(idx_ref[0, 0, r], 1)], xbuf.at[s, pl.ds(r, 1)],
                                  sem_in.at[s]).start()
            return c
        lax.fori_loop(0, MOE_TILE, body, 0, unroll=8)

    def wait_gather(s):
        pltpu.make_async_copy(x_hbm.at[pl.ds(0, MOE_TILE)], xbuf.at[s], sem_in.at[s]).wait()

    def row_out(s, r, dst_row):
        return pltpu.make_async_copy(ybuf.at[s, pl.ds(r, 1)], out_hbm.at[pl.ds(dst_row, 1)], sem_out.at[s])

    def start_scatter(s):
        def body(r, c):
            row_out(s, r, dst_ref[0, 0, r]).start()
            return c
        lax.fori_loop(0, live_ref[i], body, 0)

    def wait_scatter(s, tile):
        def body(r, c):
            row_out(s, r, 0).wait()
            return c
        lax.fori_loop(0, live_ref[tile], body, 0)

    @pl.when(i < n_tiles)
    def _():
        @pl.when(i == 0)
        def _():
            start_gather(tok_ref, 0)

        @pl.when(i + 1 < n_tiles)
        def _():
            start_gather(tok_next_ref, 1 - slot)

        @pl.when(jnp.logical_or(i == 0, tile_expert_ref[i] != tile_expert_ref[jnp.maximum(i - 1, 0)]))
        def _():
            wg16[...] = wg_ref[0].astype(jnp.bfloat16)
            wu16[...] = wu_ref[0].astype(jnp.bfloat16)
            wd16[...] = wd_ref[0].astype(jnp.bfloat16)

        wait_gather(slot)
        x = xbuf[slot].astype(jnp.bfloat16)
        g = jnp.dot(x, wg16[...], preferred_element_type=jnp.float32)
        u = jnp.dot(x, wu16[...], preferred_element_type=jnp.float32)
        h = (g * jax.nn.sigmoid(g)) * u * gate_ref[...]
        ybuf[slot] = jnp.dot(h.astype(jnp.bfloat16), wd16[...], preferred_element_type=jnp.float32)

        @pl.when(i >= 1)
        def _():
            wait_scatter(1 - slot, i - 1)

        start_scatter(slot)

        @pl.when(i == n_tiles - 1)
        def _():
            wait_scatter(slot, i)


def moe_ffn(x, router_w, router_bias, w_gate, w_up, w_down):
    B, T, D = x.shape
    n_tok = B * T
    h = x.reshape(n_tok, D)
    aff = jax.nn.sigmoid(jnp.dot(h, router_w).astype(jnp.float32))
    biased = (aff + router_bias.astype(jnp.float32)).reshape(-1, N_GROUPS, EXPERTS_PER_GROUP)
    group_score = lax.top_k(biased, TOP_K)[0].sum(axis=-1)
    best_group = jnp.argmax(group_score, axis=-1)
    in_group = jnp.arange(N_GROUPS)[None, :] == best_group[:, None]
    cand = jnp.where(in_group[:, :, None], biased, NEG_INF).reshape(-1, N_EXPERTS)
    _, top_idx = lax.top_k(cand, TOP_K)
    top_aff = jnp.take_along_axis(aff, top_idx, axis=-1)
    top_w = top_aff / jnp.sum(top_aff, axis=-1, keepdims=True)

    n_asg = n_tok * TOP_K
    max_tiles = n_asg // MOE_TILE + N_EXPERTS
    n_rows = max_tiles * MOE_TILE
    e_flat = top_idx.reshape(-1).astype(jnp.int32)
    order = jnp.argsort(e_flat, stable=True).astype(jnp.int32)
    e_sorted = e_flat[order]
    counts = jnp.zeros((N_EXPERTS,), jnp.int32).at[e_flat].add(1)
    tiles_per = (counts + MOE_TILE - 1) // MOE_TILE
    tile_end = jnp.cumsum(tiles_per)
    row_start = (tile_end - tiles_per) * MOE_TILE
    first = jnp.cumsum(counts) - counts
    row_of_sorted = row_start[e_sorted] + (jnp.arange(n_asg, dtype=jnp.int32) - first[e_sorted])
    asg_of_row = jnp.full((n_rows,), -1, jnp.int32).at[row_of_sorted].set(order)
    token_of_row = jnp.maximum(asg_of_row, 0) // TOP_K
    gate_of_row = jnp.where(asg_of_row >= 0, top_w.reshape(-1)[jnp.maximum(asg_of_row, 0)], 0.0)
    dst_of_row = jnp.maximum(asg_of_row, 0)
    live_rows = jnp.sum((asg_of_row >= 0).reshape(max_tiles, MOE_TILE), axis=1).astype(jnp.int32)
    n_tiles = tile_end[-1:].astype(jnp.int32)
    tile_expert = jnp.minimum(
        jnp.searchsorted(tile_end, jnp.arange(max_tiles, dtype=jnp.int32), side="right"),
        N_EXPERTS - 1).astype(jnp.int32)
    tile_expert = jnp.where(jnp.arange(max_tiles) < n_tiles[0], tile_expert,
                            tile_expert[jnp.maximum(n_tiles[0] - 1, 0)])

    wspec_in = pl.BlockSpec((1, D, D_EXPERT), lambda i, te, nt, lv: (te[i], 0, 0))
    idx_spec = pl.BlockSpec((1, 1, MOE_TILE), lambda i, te, nt, lv: (i, 0, 0), memory_space=pltpu.SMEM)
    next_spec = pl.BlockSpec((1, 1, MOE_TILE), lambda i, te, nt, lv: (jnp.minimum(i + 1, max_tiles - 1), 0, 0),
                             memory_space=pltpu.SMEM)
    tok3 = token_of_row.reshape(max_tiles, 1, MOE_TILE)
    y = pl.pallas_call(
        _moe_kernel,
        grid_spec=pltpu.PrefetchScalarGridSpec(
            num_scalar_prefetch=3,
            grid=(max_tiles,),
            in_specs=[idx_spec, next_spec, idx_spec,
                      pl.BlockSpec(memory_space=pl.ANY),
                      pl.BlockSpec((MOE_TILE, 1), lambda i, te, nt, lv: (i, 0)),
                      wspec_in, wspec_in,
                      pl.BlockSpec((1, D_EXPERT, D), lambda i, te, nt, lv: (te[i], 0, 0))],
            out_specs=pl.BlockSpec(memory_space=pl.ANY),
            scratch_shapes=[pltpu.VMEM((2, MOE_TILE, D), jnp.float32),
                            pltpu.VMEM((2, MOE_TILE, D), jnp.float32),
                            pltpu.VMEM((D, D_EXPERT), jnp.bfloat16),
                            pltpu.VMEM((D, D_EXPERT), jnp.bfloat16),
                            pltpu.VMEM((D_EXPERT, D), jnp.bfloat16),
                            pltpu.SemaphoreType.DMA((2,)), pltpu.SemaphoreType.DMA((2,))],
        ),
        out_shape=jax.ShapeDtypeStruct((n_asg, D), jnp.float32),
        compiler_params=pltpu.CompilerParams(
            dimension_semantics=("arbitrary",), vmem_limit_bytes=MOE_VMEM_LIMIT),
        name="routed_moe",
    )(tile_expert, n_tiles, live_rows, tok3, tok3, dst_of_row.reshape(max_tiles, 1, MOE_TILE), h,
      gate_of_row[:, None], w_gate, w_up, w_down)
    return y.reshape(B, T, TOP_K, D).sum(axis=2).astype(x.dtype)


def layer_norm(x, g, b):
    xf = x.astype(jnp.float32)
    mu = jnp.mean(xf, axis=-1, keepdims=True)
    var = jnp.mean(jnp.square(xf - mu), axis=-1, keepdims=True)
    return (xf - mu) * lax.rsqrt(var + LN_EPS) * g.astype(jnp.float32) + b.astype(jnp.float32)


def l2_normalize(a):
    return a * lax.rsqrt(jnp.sum(a * a, axis=-1, keepdims=True) + RMS_EPS)


def rope(x, pos):
    half = x.shape[-1] // 2
    inv_freq = ROPE_THETA ** (-jnp.arange(half, dtype=jnp.float32) / half)
    ang = pos.astype(jnp.float32)[:, None] * inv_freq[None, :]
    cos = jnp.cos(ang)[None, :, None, :]
    sin = jnp.sin(ang)[None, :, None, :]
    xf = x.astype(jnp.float32)
    x1, x2 = xf[..., :half], xf[..., half:]
    return jnp.concatenate([x1 * cos - x2 * sin, x2 * cos + x1 * sin], axis=-1).astype(x.dtype)


def causal_depthwise_conv(x, w):
    k_width, ch = w.shape
    return lax.conv_general_dilated(
        x, w[:, None, :].astype(x.dtype), window_strides=(1,), padding=[(k_width - 1, 0)],
        dimension_numbers=('NWC', 'WIO', 'NWC'), feature_group_count=ch)


def _softplus(x):
    return jnp.maximum(x, 0.0) + jnp.log1p(jnp.exp(-jnp.abs(x)))


def _bdot(a, b):
    return jnp.dot(a.astype(jnp.bfloat16), b.astype(jnp.bfloat16), preferred_element_type=jnp.float32)


def _bdot_nt(a, b):
    return lax.dot_general(a.astype(jnp.bfloat16), b.astype(jnp.bfloat16), (((1,), (1,)), ((), ())),
                           preferred_element_type=jnp.float32)


def _bdot_tn(a, b):
    return lax.dot_general(a.astype(jnp.bfloat16), b.astype(jnp.bfloat16), (((0,), (0,)), ((), ())),
                           preferred_element_type=jnp.float32)


def _gdn_prep_kernel(x_ref, halo_ref, w_ref, o_ref):
    i = pl.program_id(1)
    j = pl.program_id(2)
    tt = x_ref.shape[1]
    x = x_ref[0]
    halo = jnp.where(i == 0, 0.0, halo_ref[0])
    xx = jnp.concatenate([halo, x], axis=0)
    w = w_ref[...]
    y = w[3:4] * x
    for tap in range(GDN_CONV - 1):
        lo = 8 - (GDN_CONV - 1) + tap
        y = y + w[tap:tap + 1] * xx[lo:lo + tt]
    y = y * jax.nn.sigmoid(y)
    q_blocks = GDN_QK_WIDTH // x_ref.shape[2]

    @pl.when(j >= 2 * q_blocks)
    def _():
        o_ref[0] = y

    @pl.when(j < 2 * q_blocks)
    def _():
        scale = jnp.where(j < q_blocks, GDN_HEAD_DIM ** -0.5, 1.0)
        for h in range(x_ref.shape[2] // GDN_HEAD_DIM):
            yh = y[:, h * GDN_HEAD_DIM:(h + 1) * GDN_HEAD_DIM]
            inv = lax.rsqrt(jnp.sum(yh * yh, axis=-1, keepdims=True) + RMS_EPS)
            o_ref[0, :, h * GDN_HEAD_DIM:(h + 1) * GDN_HEAD_DIM] = yh * inv * scale


def gdn_prep(qkv, conv_w):
    B, T, CH = qkv.shape
    tt, tc = GDN_TILE, 512
    return pl.pallas_call(
        _gdn_prep_kernel,
        grid=(B, T // tt, CH // tc),
        in_specs=[pl.BlockSpec((1, tt, tc), lambda b, i, j: (b, i, j)),
                  pl.BlockSpec((1, 8, tc), lambda b, i, j: (b, jnp.maximum(i * (tt // 8) - 1, 0), j)),
                  pl.BlockSpec((GDN_CONV, tc), lambda b, i, j: (0, j))],
        out_specs=pl.BlockSpec((1, tt, tc), lambda b, i, j: (b, i, j)),
        out_shape=jax.ShapeDtypeStruct((B, T, CH), jnp.float32),
        compiler_params=pltpu.CompilerParams(
            dimension_semantics=("parallel", "parallel", "parallel"), vmem_limit_bytes=VMEM_LIMIT),
        name="gdn_conv_silu_norm",
    )(qkv, qkv, conv_w)


def _gdn_kernel(q_ref, k_ref, v_ref, z_ref, acol_ref, arow_ref, bcol_ref, alog_ref, dt_ref, nw_ref,
                o_ref, state_s):
    C, Dh = GDN_CHUNK, GDN_HEAD_DIM

    @pl.when(pl.program_id(2) == 0)
    def _():
        state_s[...] = jnp.zeros(state_s.shape, jnp.float32)

    row = lax.broadcasted_iota(jnp.int32, (C, C), 0)
    col = lax.broadcasted_iota(jnp.int32, (C, C), 1)
    causal = row >= col
    strict = row > col
    nw = nw_ref[...]
    probs = []
    for n in range(GDN_TILE // C):
        sl = slice(n * C, (n + 1) * C)
        q = q_ref[0, sl, :]
        k = k_ref[0, sl, :]
        kk = _bdot_nt(k, k)
        qk = _bdot_nt(q, k)
        for hh in range(2):
            neg_a = -jnp.exp(alog_ref[0, :, hh:hh + 1])
            dt = dt_ref[0, :, hh:hh + 1]
            g_col = neg_a * _softplus(acol_ref[0, 0, sl, hh:hh + 1] + dt)
            g_row = neg_a * _softplus(arow_ref[0, 0, hh:hh + 1, sl] + dt)
            beta = jax.nn.sigmoid(bcol_ref[0, 0, sl, hh:hh + 1])
            gc_col = jnp.sum(jnp.where(causal, g_row, 0.0), axis=1, keepdims=True)
            gc_row = jnp.sum(jnp.where(row <= col, g_col, 0.0), axis=0, keepdims=True)
            g_last = jnp.sum(g_row, axis=1, keepdims=True)
            decay = jnp.where(causal, jnp.exp(jnp.where(causal, gc_col - gc_row, 0.0)), 0.0)
            m = jnp.where(strict, beta * kk * decay, 0.0)
            e_col = jnp.exp(gc_col)
            v = v_ref[0, sl, hh * Dh:(hh + 1) * Dh]
            probs.append(dict(
                sl=sl, hh=hh, x=-m, p=m, a=qk * decay,
                rhs=jnp.concatenate([beta * v, beta * e_col * k], axis=1),
                qe=q * e_col, ke=k * jnp.exp(g_last - gc_col), s_decay=jnp.exp(g_last)))
    for _ in range(int(math.log2(C)) - 1):
        for pr in probs:
            pr["p"] = _bdot(pr["p"], pr["p"])
        for pr in probs:
            pr["x"] = pr["x"] + pr["p"] + _bdot(pr["x"], pr["p"])
    for pr in probs:
        pr["sol"] = pr["rhs"] + _bdot(pr["x"], pr["rhs"])
    for pr in probs:
        sl, hh = pr["sl"], pr["hh"]
        u, w = pr["sol"][:, :Dh], pr["sol"][:, Dh:]
        state = state_s[hh]
        ws = _bdot(jnp.concatenate([w, pr["qe"]], axis=0), state)
        v_new = u - ws[:C]
        o = ws[C:] + _bdot(pr["a"], v_new)
        state_s[hh] = state * pr["s_decay"] + _bdot_tn(pr["ke"], v_new)
        z = z_ref[0, sl, hh * Dh:(hh + 1) * Dh]
        o = o * lax.rsqrt(jnp.mean(o * o, axis=-1, keepdims=True) + RMS_EPS) * nw
        o_ref[0, sl, hh * Dh:(hh + 1) * Dh] = (o * (z * jax.nn.sigmoid(z))).astype(o_ref.dtype)


def gdn_core(qkv, z, a_raw, b_raw, a_log, dt_bias, norm_w):
    B, T, _ = qkv.shape
    Hk, Dh, tt = GDN_QK_HEADS, GDN_HEAD_DIM, GDN_TILE
    a4 = a_raw.reshape(B, T, Hk, 2)
    a_col = a4.transpose(0, 2, 1, 3)
    a_row = a4.transpose(0, 2, 3, 1)
    b_col = b_raw.reshape(B, T, Hk, 2).transpose(0, 2, 1, 3)
    col_spec = pl.BlockSpec((1, 1, tt, 2), lambda b, h, c: (b, h, c, 0))
    head_spec = pl.BlockSpec((1, 1, 2), lambda b, h, c: (h, 0, 0))
    return pl.pallas_call(
        _gdn_kernel,
        grid=(B, Hk, T // tt),
        in_specs=[pl.BlockSpec((1, tt, Dh), lambda b, h, c: (b, c, h)),
                  pl.BlockSpec((1, tt, Dh), lambda b, h, c: (b, c, Hk + h)),
                  pl.BlockSpec((1, tt, 2 * Dh), lambda b, h, c: (b, c, Hk + h)),
                  pl.BlockSpec((1, tt, 2 * Dh), lambda b, h, c: (b, c, h)),
                  col_spec,
                  pl.BlockSpec((1, 1, 2, tt), lambda b, h, c: (b, h, 0, c)),
                  col_spec, head_spec, head_spec,
                  pl.BlockSpec((1, Dh), lambda b, h, c: (0, 0))],
        out_specs=pl.BlockSpec((1, tt, 2 * Dh), lambda b, h, c: (b, c, h)),
        out_shape=jax.ShapeDtypeStruct((B, T, GDN_V_WIDTH), jnp.bfloat16),
        scratch_shapes=[pltpu.VMEM((2, Dh, Dh), jnp.float32)],
        compiler_params=pltpu.CompilerParams(
            dimension_semantics=("parallel", "parallel", "arbitrary"), vmem_limit_bytes=VMEM_LIMIT),
        name="gated_delta_rule",
    )(qkv, qkv, qkv, z, a_col, a_row, b_col,
      a_log.astype(jnp.float32).reshape(Hk, 1, 2), dt_bias.astype(jnp.float32).reshape(Hk, 1, 2),
      norm_w.astype(jnp.float32).reshape(1, Dh))


def gdn_mixer(x, w_in, conv_w, a_log, dt_bias, norm_w, w_out):
    B, T, _ = x.shape
    Hk, Hv, Dh = GDN_QK_HEADS, GDN_V_HEADS, GDN_HEAD_DIM
    qkv = dense(x, w_in, 0, GDN_CONV_CH)
    z = dense(x, w_in, GDN_CONV_CH, GDN_V_WIDTH)
    a_raw, b_raw = jnp.split(dense(x, w_in, GDN_CONV_CH + GDN_V_WIDTH), [Hv], axis=-1)
    o = gdn_core(gdn_prep(qkv, conv_w), z, a_raw, b_raw, a_log, dt_bias, norm_w)
    return dense(o, w_out).astype(x.dtype)


def compress_blocks(a, pe, w1, w2):
    B, T, G, D = a.shape
    n_cmp = (T - CMP_BLOCK) // CMP_STRIDE + 1
    idx = np.arange(n_cmp)[:, None] * CMP_STRIDE + np.arange(CMP_BLOCK)[None, :]
    blocks = a[:, idx] + pe[None, None, :, None, :]
    flat = jnp.moveaxis(blocks, 3, 2).reshape(B, n_cmp, G, CMP_BLOCK * D)
    return jax.nn.silu(flat @ w1) @ w2


def nsa_shared_kv(h, kv_w, cmp_pe, cmp_w1, cmp_w2):
    B, T, _ = h.shape
    kv = dense(h, kv_w).reshape(B, T, 6, NSA_KV_GROUPS, NSA_HEAD_DIM)
    pos = jnp.arange(T)
    k_cmp = compress_blocks(kv[:, :, 0], cmp_pe[0], cmp_w1[0], cmp_w2[0])
    v_cmp = compress_blocks(kv[:, :, 1], cmp_pe[1], cmp_w1[1], cmp_w2[1])
    k_slc = rope(kv[:, :, 2], pos)
    v_slc = kv[:, :, 3]
    k_win = rope(kv[:, :, 4], pos)
    v_win = kv[:, :, 5]
    return (k_cmp, v_cmp, k_slc, v_slc, k_win, v_win)


def compression_overlap(n_cmp, n_slc):
    c0 = np.arange(n_cmp) * CMP_STRIDE
    s0 = np.arange(n_slc) * SLC_BLOCK
    ov = (np.minimum(c0[:, None] + CMP_BLOCK, s0[None, :] + SLC_BLOCK)
          - np.maximum(c0[:, None], s0[None, :]))
    return jnp.asarray(np.clip(ov, 0, None) / CMP_BLOCK, dtype=jnp.float32)


def nsa_compressed(q, k_cmp, v_cmp):
    B, T, G, R, D = q.shape
    n_cmp = k_cmp.shape[1]
    s = jnp.einsum('btgrd,bcgd->bgrtc', q, k_cmp).astype(jnp.float32) * D ** -0.5
    block_end = jnp.arange(n_cmp) * CMP_STRIDE + CMP_BLOCK - 1
    visible = block_end[None, :] <= jnp.arange(T)[:, None]
    p = jax.nn.softmax(jnp.where(visible, s, NEG_INF), axis=-1) * visible
    o = jnp.einsum('bgrtc,bcgd->btgrd', p, v_cmp)
    p_slc = jnp.einsum('bgtc,cs->bgts', p.sum(axis=2), compression_overlap(n_cmp, T // SLC_BLOCK))
    return o, p_slc


def nsa_selected(q, k, v, p_slc):
    B, T, G, R, D = q.shape
    L = SLC_BLOCK
    n_slc = T // L
    n_sel = min(SLC_TOPK, n_slc)
    cur = (jnp.arange(T) // L)[:, None]
    blk = jnp.arange(n_slc)[None, :]
    causal_blk = blk <= cur
    forced = (blk == 0) | (causal_blk & (blk > cur - SLC_LOCAL))
    score = jnp.where(causal_blk, jnp.where(forced, FORCE_SCORE, p_slc), -1.0)
    top_score, top_idx = lax.top_k(score, n_sel)
    top_valid = top_score >= 0.0
    picked = jax.nn.one_hot(top_idx, n_slc, dtype=jnp.float32) * top_valid[..., None]
    sel = (picked.sum(axis=-2) > 0).astype(jnp.bfloat16)
    return masked_attention(q, k, v, sel)


def nsa_window(q, k, v):
    return masked_attention(q, k, v, None)


def _attn_kernel(*refs, windowed):
    if windowed:
        q_ref, k_ref, v_ref, o_ref, m_s, l_s, acc_s = refs
    else:
        q_ref, k_ref, v_ref, sel_ref, o_ref, m_s, l_s, acc_s = refs
    i = pl.program_id(2)
    tq, tk, D = ATTN_TILE, ATTN_TILE, NSA_HEAD_DIM
    m_s[...] = jnp.full(m_s.shape, NEG_INF, jnp.float32)
    l_s[...] = jnp.zeros(l_s.shape, jnp.float32)
    acc_s[...] = jnp.zeros(acc_s.shape, jnp.float32)
    tpos = i * tq + lax.broadcasted_iota(jnp.int32, (tk, tq), 1)

    def body(jj, carry):
        j = i - jj
        start = pl.multiple_of(j * tk, tk)
        kblk = k_ref[0, pl.ds(start, tk), :]
        vblk = v_ref[0, pl.ds(start, tk), :]
        kpos = j * tk + lax.broadcasted_iota(jnp.int32, (tk, tq), 0)
        ok = kpos <= tpos
        if windowed:
            ok = ok & (kpos > tpos - WINDOW)
        else:
            blk_of_key = j * (tk // SLC_BLOCK) + lax.broadcasted_iota(
                jnp.int32, (tk, T_BLOCKS), 0) // SLC_BLOCK
            expand = (lax.broadcasted_iota(jnp.int32, (tk, T_BLOCKS), 1) == blk_of_key)
            selm = jnp.dot(expand.astype(jnp.bfloat16), sel_ref[0, 0],
                           preferred_element_type=jnp.float32)
            ok = ok & (selm > 0.5)
        def scores(r):
            return lax.dot_general(kblk, q_ref[0, :, r * D:(r + 1) * D], (((1,), (1,)), ((), ())),
                                   preferred_element_type=jnp.float32)

        s_next = scores(0)
        for r in range(NSA_HEADS_PER_GROUP):
            s = jnp.where(ok, s_next, NEG_INF)
            if r + 1 < NSA_HEADS_PER_GROUP:
                s_next = scores(r + 1)
            m_prev = m_s[r]
            m_new = jnp.maximum(m_prev, jnp.max(s, axis=0, keepdims=True))
            alpha = jnp.exp2(m_prev - m_new)
            p = jnp.exp2(s - m_new)
            l_s[r] = alpha * l_s[r] + jnp.sum(p, axis=0, keepdims=True)
            acc_s[r] = alpha * acc_s[r] + lax.dot_general(
                vblk, p.astype(jnp.bfloat16), (((0,), (0,)), ((), ())),
                preferred_element_type=jnp.float32)
            m_s[r] = m_new
        return carry

    n_tiles = jnp.minimum(i, WINDOW // tk) + 1 if windowed else i + 1
    lax.fori_loop(0, n_tiles, body, 0)
    for r in range(NSA_HEADS_PER_GROUP):
        o_ref[0, :, r * D:(r + 1) * D] = (acc_s[r] / l_s[r]).T


def masked_attention(q, k, v, sel):
    B, T, G, R, D = q.shape
    tq = ATTN_TILE
    windowed = sel is None
    qs = (q * (D ** -0.5 * math.log2(math.e))).astype(jnp.bfloat16).reshape(B, T, G * R * D)
    kb = k.astype(jnp.bfloat16).reshape(B, T, G * D)
    vb = v.astype(jnp.bfloat16).reshape(B, T, G * D)
    in_specs = [pl.BlockSpec((1, tq, R * D), lambda b, g, i: (b, i, g)),
                pl.BlockSpec((1, T, D), lambda b, g, i: (b, 0, g)),
                pl.BlockSpec((1, T, D), lambda b, g, i: (b, 0, g))]
    args = [qs, kb, vb]
    if not windowed:
        in_specs.append(pl.BlockSpec((1, 1, T // SLC_BLOCK, tq), lambda b, g, i: (b, g, 0, i)))
        args.append(jnp.swapaxes(sel, 2, 3))
    o = pl.pallas_call(
        functools.partial(_attn_kernel, windowed=windowed),
        grid=(B, G, T // tq),
        in_specs=in_specs,
        out_specs=pl.BlockSpec((1, tq, R * D), lambda b, g, i: (b, i, g)),
        out_shape=jax.ShapeDtypeStruct((B, T, G * R * D), jnp.float32),
        scratch_shapes=[pltpu.VMEM((R, 1, tq), jnp.float32),
                        pltpu.VMEM((R, 1, tq), jnp.float32),
                        pltpu.VMEM((R, D, tq), jnp.float32)],
        compiler_params=pltpu.CompilerParams(
            dimension_semantics=("parallel", "parallel", "arbitrary"), vmem_limit_bytes=VMEM_LIMIT),
        name="window_attention" if windowed else "selected_attention",
    )(*args)
    return o.reshape(B, T, G, R, D)


def nsa_mixer(x, w_q, w_out, k_cmp, v_cmp, k_slc, v_slc, k_win, v_win):
    B, T, _ = x.shape
    G, R, D = NSA_KV_GROUPS, NSA_HEADS_PER_GROUP, NSA_HEAD_DIM
    q = dense(x, w_q, 0, NSA_Q_WIDTH).reshape(B, T, NSA_HEADS, D)
    gates = jax.nn.sigmoid(dense(x, w_q, NSA_Q_WIDTH)).reshape(B, T, 3, G, R, 1)
    q_rot = rope(q, jnp.arange(T)).reshape(B, T, G, R, D)
    o_cmp, p_slc = nsa_compressed(q.reshape(B, T, G, R, D), k_cmp, v_cmp)
    o_slc = nsa_selected(q_rot, k_slc, v_slc, p_slc)
    o_win = nsa_window(q_rot, k_win, v_win)
    o = gates[:, :, 0] * o_cmp + gates[:, :, 1] * o_slc + gates[:, :, 2] * o_win
    return dense(o.reshape(B, T, NSA_Q_WIDTH).astype(x.dtype), w_out).astype(x.dtype)


def kernel(x, a_w_in, a_conv_w, a_a_log, a_dt_bias, a_norm_w, a_w_out, kv_w, cmp_pe, cmp_w1, cmp_w2,
           b_w_q, b_w_out, router_w, router_bias, moe_w_gate, moe_w_up, moe_w_down, ln_g, ln_b):
    shared_kv = None
    for layer in range(DEPTH):
        if layer < N_A_LAYERS:
            mix = gdn_mixer(x, a_w_in[layer], a_conv_w[layer], a_a_log[layer], a_dt_bias[layer],
                            a_norm_w[layer], a_w_out[layer])
        else:
            if shared_kv is None:
                shared_kv = nsa_shared_kv(x, kv_w, cmp_pe, cmp_w1, cmp_w2)
            j = layer - N_A_LAYERS
            mix = nsa_mixer(x, b_w_q[j], b_w_out[j], *shared_kv)
        x = layer_norm(DEEPNORM_ALPHA * x + mix, ln_g[layer, 0], ln_b[layer, 0]).astype(x.dtype)
        ffn = moe_ffn(x, router_w, router_bias, moe_w_gate[layer], moe_w_up[layer], moe_w_down[layer])
        x = layer_norm(DEEPNORM_ALPHA * x + ffn, ln_g[layer, 1], ln_b[layer, 1]).astype(x.dtype)
    return x
```

```python
import functools
import math

import jax
import jax.numpy as jnp
import numpy as np
from jax import lax
from jax.experimental import pallas as pl
from jax.experimental.pallas import tpu as pltpu

D_MODEL = 2048
BATCH = 2
SEQ = 4096
DEPTH = 2
N_A_LAYERS = DEPTH // 2

GDN_HEAD_DIM = 128
GDN_QK_HEADS = D_MODEL // GDN_HEAD_DIM
GDN_V_HEADS = 2 * GDN_QK_HEADS
GDN_QK_WIDTH = GDN_QK_HEADS * GDN_HEAD_DIM
GDN_V_WIDTH = GDN_V_HEADS * GDN_HEAD_DIM
GDN_CONV_CH = 2 * GDN_QK_WIDTH + GDN_V_WIDTH
GDN_CHUNK = 64
GDN_CONV = 4
GDN_TILE = 256

NSA_HEAD_DIM = 128
NSA_HEADS = D_MODEL // NSA_HEAD_DIM
NSA_KV_GROUPS = 4
NSA_HEADS_PER_GROUP = NSA_HEADS // NSA_KV_GROUPS
NSA_Q_WIDTH = NSA_HEADS * NSA_HEAD_DIM
CMP_BLOCK = 32
CMP_STRIDE = 16
CMP_HIDDEN = 512
SLC_BLOCK = 64
SLC_TOPK = 16
SLC_LOCAL = 2
WINDOW = 512
WIN_Q_BLOCK = 128
SLC_Q_BLOCK = 64
ROPE_THETA = 10000.0

N_EXPERTS = 32
N_GROUPS = 8
EXPERTS_PER_GROUP = N_EXPERTS // N_GROUPS
TOP_K = 2
D_EXPERT = D_MODEL // 4

DEEPNORM_ALPHA = (2 * DEPTH) ** 0.25
LN_EPS = 1e-5
RMS_EPS = 1e-6
NEG_INF = -1e30
FORCE_SCORE = 1e6

LANE = 128
VMEM_LIMIT = 48 * 1024 * 1024
MOE_VMEM_LIMIT = 56 * 1024 * 1024
MOE_TILE = 256
ATTN_TILE = 256
T_BLOCKS = SEQ // SLC_BLOCK


def _mm_kernel(a_ref, b_ref, o_ref, b16_s):
    @pl.when(pl.program_id(1) == 0)
    def _():
        b16_s[...] = b_ref[...].astype(jnp.bfloat16)

    o_ref[...] = jnp.dot(a_ref[...], b16_s[...], preferred_element_type=jnp.float32)


def pmatmul(a, b, col_start=0, n_cols=None, row_block=0, tm=512):
    m, k = a.shape
    n_cols = b.shape[1] - col_start if n_cols is None else n_cols
    if n_cols % LANE or col_start % LANE:
        b = jnp.pad(b[:, col_start:col_start + n_cols], ((0, 0), (0, -n_cols % LANE)))
        return pmatmul(a, b)[:, :n_cols]
    tn = math.gcd(n_cols, 1024 if k <= 2048 else 512)
    assert col_start % tn == 0 and m % tm == 0 and b.shape[0] % k == 0
    off = col_start // tn
    return pl.pallas_call(
        _mm_kernel,
        grid=(n_cols // tn, m // tm),
        in_specs=[pl.BlockSpec((tm, k), lambda j, i: (i, 0)),
                  pl.BlockSpec((k, tn), lambda j, i: (row_block, j + off))],
        out_specs=pl.BlockSpec((tm, tn), lambda j, i: (i, j)),
        out_shape=jax.ShapeDtypeStruct((m, n_cols), jnp.float32),
        scratch_shapes=[pltpu.VMEM((k, tn), jnp.bfloat16)],
        compiler_params=pltpu.CompilerParams(
            dimension_semantics=("parallel", "arbitrary"), vmem_limit_bytes=VMEM_LIMIT),
        name="dense_matmul",
    )(a.astype(jnp.bfloat16), b.astype(jnp.float32))


def dense(x, w, col_start=0, n_cols=None):
    lead = x.shape[:-1]
    out = pmatmul(x.reshape(-1, x.shape[-1]), w, col_start, n_cols)
    return out.reshape(lead + (out.shape[-1],))


def _moe_kernel(tile_expert_ref, n_tiles_ref, live_ref, tok_ref, tok_next_ref, dst_ref, x_hbm, gate_ref,
                wg_ref, wu_ref, wd_ref, out_hbm, xbuf, ybuf, wg16, wu16, wd16, sem_in, sem_out):
    i = pl.program_id(0)
    n_tiles = n_tiles_ref[0]
    slot = i % 2

    def start_gather(idx_ref, s):
        def body(r, c):
            pltpu.make_async_copy(x_hbm.at[pl.ds(idx_ref[0, 0, r], 1)], xbuf.at[s, pl.ds(r, 1)],
                                  sem_in.at[s]).start()
            return c
        lax.fori_loop(0, MOE_TILE, body, 0, unroll=8)

    def wait_gather(s):
        pltpu.make_async_copy(x_hbm.at[pl.ds(0, MOE_TILE)], xbuf.at[s], sem_in.at[s]).wait()

    def row_out(s, r, dst_row):
        return pltpu.make_async_copy(ybuf.at[s, pl.ds(r, 1)], out_hbm.at[pl.ds(dst_row, 1)], sem_out.at[s])

    def start_scatter(s):
        def body(r, c):
            row_out(s, r, dst_ref[0, 0, r]).start()
            return c
        lax.fori_loop(0, live_ref[i], body, 0)

    def wait_scatter(s, tile):
        def body(r, c):
            row_out(s, r, 0).wait()
            return c
        lax.fori_loop(0, live_ref[tile], body, 0)

    @pl.when(i < n_tiles)
    def _():
        @pl.when(i == 0)
        def _():
            start_gather(tok_ref, 0)

        @pl.when(i + 1 < n_tiles)
        def _():
            start_gather(tok_next_ref, 1 - slot)

        @pl.when(jnp.logical_or(i == 0, tile_expert_ref[i] != tile_expert_ref[jnp.maximum(i - 1, 0)]))
        def _():
            wg16[...] = wg_ref[0].astype(jnp.bfloat16)
            wu16[...] = wu_ref[0].astype(jnp.bfloat16)
            wd16[...] = wd_ref[0].astype(jnp.bfloat16)

        wait_gather(slot)
        x = xbuf[slot].astype(jnp.bfloat16)
        g = jnp.dot(x, wg16[...], preferred_element_type=jnp.float32)
        u = jnp.dot(x, wu16[...], preferred_element_type=jnp.float32)
        h = (g * jax.nn.sigmoid(g)) * u * gate_ref[...]
        ybuf[slot] = jnp.dot(h.astype(jnp.bfloat16), wd16[...], preferred_element_type=jnp.float32)

        @pl.when(i >= 1)
        def _():
            wait_scatter(1 - slot, i - 1)

        start_scatter(slot)

        @pl.when(i == n_tiles - 1)
        def _():
            wait_scatter(slot, i)


def moe_ffn(h, router_logits, router_bias, w_gate, w_up, w_down):
    n_tok, D = h.shape
    aff = jax.nn.sigmoid(router_logits.astype(jnp.float32))
    biased = (aff + router_bias.astype(jnp.float32)).reshape(-1, N_GROUPS, EXPERTS_PER_GROUP)
    group_score = lax.top_k(biased, TOP_K)[0].sum(axis=-1)
    best_group = jnp.argmax(group_score, axis=-1)
    in_group = jnp.arange(N_GROUPS)[None, :] == best_group[:, None]
    cand = jnp.where(in_group[:, :, None], biased, NEG_INF).reshape(-1, N_EXPERTS)
    _, top_idx = lax.top_k(cand, TOP_K)
    top_aff = jnp.take_along_axis(aff, top_idx, axis=-1)
    top_w = top_aff / jnp.sum(top_aff, axis=-1, keepdims=True)

    n_asg = n_tok * TOP_K
    max_tiles = n_asg // MOE_TILE + N_EXPERTS
    n_rows = max_tiles * MOE_TILE
    e_flat = top_idx.reshape(-1).astype(jnp.int32)
    order = jnp.argsort(e_flat, stable=True).astype(jnp.int32)
    e_sorted = e_flat[order]
    counts = jnp.zeros((N_EXPERTS,), jnp.int32).at[e_flat].add(1)
    tiles_per = (counts + MOE_TILE - 1) // MOE_TILE
    tile_end = jnp.cumsum(tiles_per)
    row_start = (tile_end - tiles_per) * MOE_TILE
    first = jnp.cumsum(counts) - counts
    row_of_sorted = row_start[e_sorted] + (jnp.arange(n_asg, dtype=jnp.int32) - first[e_sorted])
    asg_of_row = jnp.full((n_rows,), -1, jnp.int32).at[row_of_sorted].set(order)
    token_of_row = jnp.maximum(asg_of_row, 0) // TOP_K
    gate_of_row = jnp.where(asg_of_row >= 0, top_w.reshape(-1)[jnp.maximum(asg_of_row, 0)], 0.0)
    dst_of_row = jnp.maximum(asg_of_row, 0)
    live_rows = jnp.sum((asg_of_row >= 0).reshape(max_tiles, MOE_TILE), axis=1).astype(jnp.int32)
    n_tiles = tile_end[-1:].astype(jnp.int32)
    tile_expert = jnp.minimum(
        jnp.searchsorted(tile_end, jnp.arange(max_tiles, dtype=jnp.int32), side="right"),
        N_EXPERTS - 1).astype(jnp.int32)
    tile_expert = jnp.where(jnp.arange(max_tiles) < n_tiles[0], tile_expert,
                            tile_expert[jnp.maximum(n_tiles[0] - 1, 0)])

    wspec_in = pl.BlockSpec((1, D, D_EXPERT), lambda i, te, nt, lv: (te[i], 0, 0))
    idx_spec = pl.BlockSpec((1, 1, MOE_TILE), lambda i, te, nt, lv: (i, 0, 0), memory_space=pltpu.SMEM)
    next_spec = pl.BlockSpec((1, 1, MOE_TILE), lambda i, te, nt, lv: (jnp.minimum(i + 1, max_tiles - 1), 0, 0),
                             memory_space=pltpu.SMEM)
    tok3 = token_of_row.reshape(max_tiles, 1, MOE_TILE)
    y = pl.pallas_call(
        _moe_kernel,
        grid_spec=pltpu.PrefetchScalarGridSpec(
            num_scalar_prefetch=3,
            grid=(max_tiles,),
            in_specs=[idx_spec, next_spec, idx_spec,
                      pl.BlockSpec(memory_space=pl.ANY),
                      pl.BlockSpec((MOE_TILE, 1), lambda i, te, nt, lv: (i, 0)),
                      wspec_in, wspec_in,
                      pl.BlockSpec((1, D_EXPERT, D), lambda i, te, nt, lv: (te[i], 0, 0))],
            out_specs=pl.BlockSpec(memory_space=pl.ANY),
            scratch_shapes=[pltpu.VMEM((2, MOE_TILE, D), jnp.float32),
                            pltpu.VMEM((2, MOE_TILE, D), jnp.float32),
                            pltpu.VMEM((D, D_EXPERT), jnp.bfloat16),
                            pltpu.VMEM((D, D_EXPERT), jnp.bfloat16),
                            pltpu.VMEM((D_EXPERT, D), jnp.bfloat16),
                            pltpu.SemaphoreType.DMA((2,)), pltpu.SemaphoreType.DMA((2,))],
        ),
        out_shape=jax.ShapeDtypeStruct((n_asg, D), jnp.float32),
        compiler_params=pltpu.CompilerParams(
            dimension_semantics=("arbitrary",), vmem_limit_bytes=MOE_VMEM_LIMIT),
        name="routed_moe",
    )(tile_expert, n_tiles, live_rows, tok3, tok3, dst_of_row.reshape(max_tiles, 1, MOE_TILE), h,
      gate_of_row[:, None], w_gate, w_up, w_down)
    return y.reshape(n_tok, TOP_K * D)


def _residual_ln_kernel(x_ref, mix_ref, g_ref, b_ref, *rest, n_mix, with_router):
    if with_router:
        rw_ref, o_ref, o16_ref, logit_ref = rest
    else:
        o_ref, o16_ref = rest
    d = x_ref.shape[1]
    h = DEEPNORM_ALPHA * x_ref[...]
    for s in range(n_mix):
        h = h + mix_ref[:, s * d:(s + 1) * d]
    mu = jnp.mean(h, axis=-1, keepdims=True)
    var = jnp.mean(jnp.square(h - mu), axis=-1, keepdims=True)
    y = (h - mu) * lax.rsqrt(var + LN_EPS) * g_ref[...] + b_ref[...]
    o_ref[...] = y
    y16 = y.astype(jnp.bfloat16)
    o16_ref[...] = y16
    if with_router:
        logit_ref[...] = jnp.dot(y16, rw_ref[...].astype(jnp.bfloat16), preferred_element_type=jnp.float32)


def residual_layer_norm(x, mix, g, b, router_w=None):
    n, d = x.shape
    n_mix = mix.shape[1] // d
    tm = 256
    row = lambda i: (i, 0)
    fixed = lambda i: (0, 0)
    in_specs = [pl.BlockSpec((tm, d), row), pl.BlockSpec((tm, n_mix * d), row),
                pl.BlockSpec((1, d), fixed), pl.BlockSpec((1, d), fixed)]
    args = [x, mix, g.astype(jnp.float32).reshape(1, d), b.astype(jnp.float32).reshape(1, d)]
    out_specs = [pl.BlockSpec((tm, d), row), pl.BlockSpec((tm, d), row)]
    out_shape = [jax.ShapeDtypeStruct((n, d), jnp.float32), jax.ShapeDtypeStruct((n, d), jnp.bfloat16)]
    if router_w is not None:
        e_pad = -router_w.shape[1] % LANE
        rw = jnp.pad(router_w.astype(jnp.float32), ((0, 0), (0, e_pad)))
        in_specs.append(pl.BlockSpec(rw.shape, fixed))
        args.append(rw)
        out_specs.append(pl.BlockSpec((tm, rw.shape[1]), row))
        out_shape.append(jax.ShapeDtypeStruct((n, rw.shape[1]), jnp.float32))
    outs = pl.pallas_call(
        functools.partial(_residual_ln_kernel, n_mix=n_mix, with_router=router_w is not None),
        grid=(n // tm,),
        in_specs=in_specs, out_specs=out_specs, out_shape=out_shape,
        compiler_params=pltpu.CompilerParams(
            dimension_semantics=("parallel",), vmem_limit_bytes=VMEM_LIMIT),
        name="residual_layer_norm",
    )(*args)
    if router_w is not None:
        return outs[0], outs[1], outs[2][:, :router_w.shape[1]]
    return outs[0], outs[1]


def _softplus(x):
    return jnp.maximum(x, 0.0) + jnp.log1p(jnp.exp(-jnp.abs(x)))


def _bdot(a, b):
    return jnp.dot(a.astype(jnp.bfloat16), b.astype(jnp.bfloat16), preferred_element_type=jnp.float32)


def _bdot_nt(a, b):
    return lax.dot_general(a.astype(jnp.bfloat16), b.astype(jnp.bfloat16), (((1,), (1,)), ((), ())),
                           preferred_element_type=jnp.float32)


def _bdot_tn(a, b):
    return lax.dot_general(a.astype(jnp.bfloat16), b.astype(jnp.bfloat16), (((0,), (0,)), ((), ())),
                           preferred_element_type=jnp.float32)


def _gdn_prep_kernel(x_ref, halo_ref, w_ref, o_ref):
    i = pl.program_id(1)
    j = pl.program_id(2)
    tt = x_ref.shape[1]
    x = x_ref[0]
    halo = jnp.where(i == 0, 0.0, halo_ref[0])
    xx = jnp.concatenate([halo, x], axis=0)
    w = w_ref[...]
    y = w[3:4] * x
    for tap in range(GDN_CONV - 1):
        lo = 8 - (GDN_CONV - 1) + tap
        y = y + w[tap:tap + 1] * xx[lo:lo + tt]
    y = y * jax.nn.sigmoid(y)
    q_blocks = GDN_QK_WIDTH // x_ref.shape[2]

    @pl.when(j >= 2 * q_blocks)
    def _():
        o_ref[0] = y

    @pl.when(j < 2 * q_blocks)
    def _():
        scale = jnp.where(j < q_blocks, GDN_HEAD_DIM ** -0.5, 1.0)
        for h in range(x_ref.shape[2] // GDN_HEAD_DIM):
            yh = y[:, h * GDN_HEAD_DIM:(h + 1) * GDN_HEAD_DIM]
            inv = lax.rsqrt(jnp.sum(yh * yh, axis=-1, keepdims=True) + RMS_EPS)
            o_ref[0, :, h * GDN_HEAD_DIM:(h + 1) * GDN_HEAD_DIM] = yh * inv * scale


def gdn_prep(qkv, conv_w):
    B, T, CH = qkv.shape
    tt, tc = GDN_TILE, 512
    return pl.pallas_call(
        _gdn_prep_kernel,
        grid=(B, T // tt, CH // tc),
        in_specs=[pl.BlockSpec((1, tt, tc), lambda b, i, j: (b, i, j)),
                  pl.BlockSpec((1, 8, tc), lambda b, i, j: (b, jnp.maximum(i * (tt // 8) - 1, 0), j)),
                  pl.BlockSpec((GDN_CONV, tc), lambda b, i, j: (0, j))],
        out_specs=pl.BlockSpec((1, tt, tc), lambda b, i, j: (b, i, j)),
        out_shape=jax.ShapeDtypeStruct((B, T, CH), jnp.float32),
        compiler_params=pltpu.CompilerParams(
            dimension_semantics=("parallel", "parallel", "parallel"), vmem_limit_bytes=VMEM_LIMIT),
        name="gdn_conv_silu_norm",
    )(qkv, qkv, conv_w)


def _gdn_kernel(q_ref, k_ref, v_ref, z_ref, acol_ref, arow_ref, bcol_ref, alog_ref, dt_ref, nw_ref,
                o_ref, state_s):
    C, Dh = GDN_CHUNK, GDN_HEAD_DIM

    @pl.when(pl.program_id(2) == 0)
    def _():
        state_s[...] = jnp.zeros(state_s.shape, jnp.float32)

    row = lax.broadcasted_iota(jnp.int32, (C, C), 0)
    col = lax.broadcasted_iota(jnp.int32, (C, C), 1)
    causal = row >= col
    strict = row > col
    nw = nw_ref[...]
    probs = []
    for n in range(GDN_TILE // C):
        sl = slice(n * C, (n + 1) * C)
        q = q_ref[0, sl, :]
        k = k_ref[0, sl, :]
        kk = _bdot_nt(k, k)
        qk = _bdot_nt(q, k)
        for hh in range(2):
            neg_a = -jnp.exp(alog_ref[0, :, hh:hh + 1])
            dt = dt_ref[0, :, hh:hh + 1]
            g_col = neg_a * _softplus(acol_ref[0, 0, sl, hh:hh + 1] + dt)
            g_row = neg_a * _softplus(arow_ref[0, 0, hh:hh + 1, sl] + dt)
            beta = jax.nn.sigmoid(bcol_ref[0, 0, sl, hh:hh + 1])
            gc_col = jnp.sum(jnp.where(causal, g_row, 0.0), axis=1, keepdims=True)
            gc_row = jnp.sum(jnp.where(row <= col, g_col, 0.0), axis=0, keepdims=True)
            g_last = jnp.sum(g_row, axis=1, keepdims=True)
            decay = jnp.where(causal, jnp.exp(jnp.where(causal, gc_col - gc_row, 0.0)), 0.0)
            m = jnp.where(strict, beta * kk * decay, 0.0)
            e_col = jnp.exp(gc_col)
            v = v_ref[0, sl, hh * Dh:(hh + 1) * Dh]
            probs.append(dict(
                sl=sl, hh=hh, x=-m, p=m, a=qk * decay,
                rhs=jnp.concatenate([beta * v, beta * e_col * k], axis=1),
                qe=q * e_col, ke=k * jnp.exp(g_last - gc_col), s_decay=jnp.exp(g_last)))
    for _ in range(int(math.log2(C)) - 1):
        for pr in probs:
            pr["p"] = _bdot(pr["p"], pr["p"])
        for pr in probs:
            pr["x"] = pr["x"] + pr["p"] + _bdot(pr["x"], pr["p"])
    for pr in probs:
        pr["sol"] = pr["rhs"] + _bdot(pr["x"], pr["rhs"])
    for pr in probs:
        sl, hh = pr["sl"], pr["hh"]
        u, w = pr["sol"][:, :Dh], pr["sol"][:, Dh:]
        state = state_s[hh]
        ws = _bdot(jnp.concatenate([w, pr["qe"]], axis=0), state)
        v_new = u - ws[:C]
        o = ws[C:] + _bdot(pr["a"], v_new)
        state_s[hh] = state * pr["s_decay"] + _bdot_tn(pr["ke"], v_new)
        z = z_ref[0, sl, hh * Dh:(hh + 1) * Dh]
        o = o * lax.rsqrt(jnp.mean(o * o, axis=-1, keepdims=True) + RMS_EPS) * nw
        o_ref[0, sl, hh * Dh:(hh + 1) * Dh] = (o * (z * jax.nn.sigmoid(z))).astype(o_ref.dtype)


def gdn_core(qkv, z, a_raw, b_raw, a_log, dt_bias, norm_w):
    B, T, _ = qkv.shape
    Hk, Dh, tt = GDN_QK_HEADS, GDN_HEAD_DIM, GDN_TILE
    a4 = a_raw.reshape(B, T, Hk, 2)
    a_col = a4.transpose(0, 2, 1, 3)
    a_row = a4.transpose(0, 2, 3, 1)
    b_col = b_raw.reshape(B, T, Hk, 2).transpose(0, 2, 1, 3)
    col_spec = pl.BlockSpec((1, 1, tt, 2), lambda b, h, c: (b, h, c, 0))
    head_spec = pl.BlockSpec((1, 1, 2), lambda b, h, c: (h, 0, 0))
    return pl.pallas_call(
        _gdn_kernel,
        grid=(B, Hk, T // tt),
        in_specs=[pl.BlockSpec((1, tt, Dh), lambda b, h, c: (b, c, h)),
                  pl.BlockSpec((1, tt, Dh), lambda b, h, c: (b, c, Hk + h)),
                  pl.BlockSpec((1, tt, 2 * Dh), lambda b, h, c: (b, c, Hk + h)),
                  pl.BlockSpec((1, tt, 2 * Dh), lambda b, h, c: (b, c, h)),
                  col_spec,
                  pl.BlockSpec((1, 1, 2, tt), lambda b, h, c: (b, h, 0, c)),
                  col_spec, head_spec, head_spec,
                  pl.BlockSpec((1, Dh), lambda b, h, c: (0, 0))],
        out_specs=pl.BlockSpec((1, tt, 2 * Dh), lambda b, h, c: (b, c, h)),
        out_shape=jax.ShapeDtypeStruct((B, T, GDN_V_WIDTH), jnp.bfloat16),
        scratch_shapes=[pltpu.VMEM((2, Dh, Dh), jnp.float32)],
        compiler_params=pltpu.CompilerParams(
            dimension_semantics=("parallel", "parallel", "arbitrary"), vmem_limit_bytes=VMEM_LIMIT),
        name="gated_delta_rule",
    )(qkv, qkv, qkv, z, a_col, a_row, b_col,
      a_log.astype(jnp.float32).reshape(Hk, 1, 2), dt_bias.astype(jnp.float32).reshape(Hk, 1, 2),
      norm_w.astype(jnp.float32).reshape(1, Dh))


def gdn_mixer(x, w_in, conv_w, a_log, dt_bias, norm_w, w_out):
    B, T, _ = x.shape
    Hk, Hv, Dh = GDN_QK_HEADS, GDN_V_HEADS, GDN_HEAD_DIM
    qkv = dense(x, w_in, 0, GDN_CONV_CH)
    z = dense(x, w_in, GDN_CONV_CH, GDN_V_WIDTH)
    a_raw, b_raw = jnp.split(dense(x, w_in, GDN_CONV_CH + GDN_V_WIDTH), [Hv], axis=-1)
    o = gdn_core(gdn_prep(qkv, conv_w), z, a_raw, b_raw, a_log, dt_bias, norm_w)
    return dense(o, w_out)


def _rope_cast_kernel(x_ref, cos_ref, sin_ref, o_ref, *, rotate, scale):
    D = NSA_HEAD_DIM
    for h in range(x_ref.shape[2] // D):
        x = x_ref[0, :, h * D:(h + 1) * D]
        if rotate:
            x = x * cos_ref[...] + pltpu.roll(x, D // 2, axis=1) * sin_ref[...]
        if scale != 1.0:
            x = x * scale
        o_ref[0, :, h * D:(h + 1) * D] = x.astype(o_ref.dtype)


def rope_cast(x, col_start, n_cols, rotate, scale=1.0):
    B, T, _ = x.shape
    tt, tc, D = 512, 512, NSA_HEAD_DIM
    half = D // 2
    inv_freq = ROPE_THETA ** (-jnp.arange(half, dtype=jnp.float32) / half)
    ang = jnp.arange(T, dtype=jnp.float32)[:, None] * inv_freq[None, :]
    cos2 = jnp.concatenate([jnp.cos(ang), jnp.cos(ang)], axis=1)
    sin2 = jnp.concatenate([-jnp.sin(ang), jnp.sin(ang)], axis=1)
    off = col_start // tc
    return pl.pallas_call(
        functools.partial(_rope_cast_kernel, rotate=rotate, scale=scale),
        grid=(B, T // tt, n_cols // tc),
        in_specs=[pl.BlockSpec((1, tt, tc), lambda b, i, j: (b, i, off + j)),
                  pl.BlockSpec((tt, D), lambda b, i, j: (i, 0)),
                  pl.BlockSpec((tt, D), lambda b, i, j: (i, 0))],
        out_specs=pl.BlockSpec((1, tt, tc), lambda b, i, j: (b, i, j)),
        out_shape=jax.ShapeDtypeStruct((B, T, n_cols), jnp.bfloat16),
        compiler_params=pltpu.CompilerParams(
            dimension_semantics=("parallel", "parallel", "parallel"), vmem_limit_bytes=VMEM_LIMIT),
        name="rope_cast",
    )(x, cos2, sin2)


def compress_blocks(a, pe, w1, w2):
    B, T, _ = a.shape
    G, D, S = NSA_KV_GROUPS, NSA_HEAD_DIM, CMP_STRIDE
    n_half = T // S
    n_cmp = (T - CMP_BLOCK) // S + 1
    halves = a.reshape(B, n_half, S, G, D).transpose(0, 3, 1, 2, 4).reshape(B * G * n_half, S * D)
    top = halves + pe[:S].reshape(1, S * D)
    bot = halves + pe[S:].reshape(1, S * D)
    p_top = pmatmul(top, w1, row_block=0).reshape(B * G, n_half, CMP_HIDDEN)
    p_bot = pmatmul(bot, w1, row_block=1).reshape(B * G, n_half, CMP_HIDDEN)
    hid = jax.nn.silu(p_top[:, :n_cmp] + p_bot[:, 1:n_cmp + 1])
    hid = jnp.pad(hid, ((0, 0), (0, n_half - n_cmp), (0, 0))).reshape(B * G * n_half, CMP_HIDDEN)
    out = pmatmul(hid, w2).reshape(B, G, n_half, D)
    live = (jnp.arange(n_half) < n_cmp)[None, None, :, None]
    return jnp.where(live, out, 0.0).astype(jnp.bfloat16)


def nsa_shared_kv(h, kv_w, cmp_pe, cmp_w1, cmp_w2):
    W = NSA_KV_GROUPS * NSA_HEAD_DIM
    kv = dense(h, kv_w)
    k_cmp = compress_blocks(kv[..., :W], cmp_pe[0], cmp_w1[0], cmp_w2[0])
    v_cmp = compress_blocks(kv[..., W:2 * W], cmp_pe[1], cmp_w1[1], cmp_w2[1])
    k_slc = rope_cast(kv, 2 * W, W, True)
    v_slc = rope_cast(kv, 3 * W, W, False)
    k_win = rope_cast(kv, 4 * W, W, True)
    v_win = rope_cast(kv, 5 * W, W, False)
    return (k_cmp, v_cmp, k_slc, v_slc, k_win, v_win)


def _cmp_select_kernel(q_ref, kc_ref, vc_ref, ov_ref, o_ref, sel_ref):
    i = pl.program_id(2)
    tq, D, n_c, n_s = ATTN_TILE, NSA_HEAD_DIM, kc_ref.shape[2], T_BLOCKS
    tpos = i * tq + lax.broadcasted_iota(jnp.int32, (n_c, tq), 1)
    block_end = lax.broadcasted_iota(jnp.int32, (n_c, tq), 0) * CMP_STRIDE + (CMP_BLOCK - 1)
    visible = block_end <= tpos
    kc = kc_ref[0, 0]
    vc = vc_ref[0, 0]
    p_sum = jnp.zeros((n_c, tq), jnp.float32)
    for r in range(NSA_HEADS_PER_GROUP):
        s = lax.dot_general(kc, q_ref[0, :, r * D:(r + 1) * D], (((1,), (1,)), ((), ())),
                            preferred_element_type=jnp.float32)
        s = jnp.where(visible, s, NEG_INF)
        e = jnp.exp2(s - jnp.max(s, axis=0, keepdims=True))
        p = jnp.where(visible, e / jnp.sum(e, axis=0, keepdims=True), 0.0)
        o_t = lax.dot_general(vc, p.astype(jnp.bfloat16), (((0,), (0,)), ((), ())),
                              preferred_element_type=jnp.float32)
        o_ref[0, :, r * D:(r + 1) * D] = o_t.T
        p_sum = p_sum + p
    p_slc = jnp.dot(ov_ref[...], p_sum.astype(jnp.bfloat16), preferred_element_type=jnp.float32)
    blk = lax.broadcasted_iota(jnp.int32, (n_s, tq), 0)
    cur = (i * tq + lax.broadcasted_iota(jnp.int32, (n_s, tq), 1)) // SLC_BLOCK
    causal_blk = blk <= cur
    forced = (blk == 0) | (causal_blk & (blk > cur - SLC_LOCAL))
    score = jnp.where(causal_blk, jnp.where(forced, FORCE_SCORE, p_slc), -1.0)
    rank = jnp.zeros((n_s, tq), jnp.float32)
    for other in range(n_s):
        row = score[other:other + 1, :]
        ahead = (row > score) | ((row == score) & (blk > other))
        rank = rank + jnp.where(ahead, 1.0, 0.0)
    picked = (rank < float(min(SLC_TOPK, n_s))) & (score >= 0.0)
    sel_ref[0, 0] = jnp.where(picked, 1.0, 0.0).astype(sel_ref.dtype)


def nsa_compressed_select(q_cmp, k_cmp, v_cmp):
    B, T, _ = q_cmp.shape
    G, R, D, tq = NSA_KV_GROUPS, NSA_HEADS_PER_GROUP, NSA_HEAD_DIM, ATTN_TILE
    n_c = k_cmp.shape[2]
    n_s = T // SLC_BLOCK
    c0 = np.arange(n_c) * CMP_STRIDE
    s0 = np.arange(n_s) * SLC_BLOCK
    ov = np.clip(np.minimum(c0[None, :] + CMP_BLOCK, s0[:, None] + SLC_BLOCK)
                 - np.maximum(c0[None, :], s0[:, None]), 0, None) / CMP_BLOCK
    cmp_spec = pl.BlockSpec((1, 1, n_c, D), lambda b, g, i: (b, g, 0, 0))
    return pl.pallas_call(
        _cmp_select_kernel,
        grid=(B, G, T // tq),
        in_specs=[pl.BlockSpec((1, tq, R * D), lambda b, g, i: (b, i, g)), cmp_spec, cmp_spec,
                  pl.BlockSpec((n_s, n_c), lambda b, g, i: (0, 0))],
        out_specs=[pl.BlockSpec((1, tq, R * D), lambda b, g, i: (b, i, g)),
                   pl.BlockSpec((1, 1, n_s, tq), lambda b, g, i: (b, g, 0, i))],
        out_shape=[jax.ShapeDtypeStruct((B, T, G * R * D), jnp.float32),
                   jax.ShapeDtypeStruct((B, G, n_s, T), jnp.bfloat16)],
        compiler_params=pltpu.CompilerParams(
            dimension_semantics=("parallel", "parallel", "parallel"), vmem_limit_bytes=VMEM_LIMIT),
        name="compressed_attention_select",
    )(q_cmp, k_cmp, v_cmp, jnp.asarray(ov, jnp.bfloat16))


def _attn_kernel(*refs, windowed):
    if windowed:
        q_ref, k_ref, v_ref, o_ref, m_s, l_s, acc_s = refs
    else:
        q_ref, k_ref, v_ref, sel_ref, o_ref, m_s, l_s, acc_s = refs
    i = pl.program_id(2)
    tq, tk, D = ATTN_TILE, ATTN_TILE, NSA_HEAD_DIM
    m_s[...] = jnp.full(m_s.shape, NEG_INF, jnp.float32)
    l_s[...] = jnp.zeros(l_s.shape, jnp.float32)
    acc_s[...] = jnp.zeros(acc_s.shape, jnp.float32)
    tpos = i * tq + lax.broadcasted_iota(jnp.int32, (tk, tq), 1)

    def body(jj, carry):
        j = i - jj
        start = pl.multiple_of(j * tk, tk)
        kblk = k_ref[0, pl.ds(start, tk), :]
        vblk = v_ref[0, pl.ds(start, tk), :]
        kpos = j * tk + lax.broadcasted_iota(jnp.int32, (tk, tq), 0)
        ok = kpos <= tpos
        if windowed:
            ok = ok & (kpos > tpos - WINDOW)
        else:
            blk_of_key = j * (tk // SLC_BLOCK) + lax.broadcasted_iota(
                jnp.int32, (tk, T_BLOCKS), 0) // SLC_BLOCK
            expand = (lax.broadcasted_iota(jnp.int32, (tk, T_BLOCKS), 1) == blk_of_key)
            selm = jnp.dot(expand.astype(jnp.bfloat16), sel_ref[0, 0],
                           preferred_element_type=jnp.float32)
            ok = ok & (selm > 0.5)
        def scores(r):
            return lax.dot_general(kblk, q_ref[0, :, r * D:(r + 1) * D], (((1,), (1,)), ((), ())),
                                   preferred_element_type=jnp.float32)

        s_next = scores(0)
        for r in range(NSA_HEADS_PER_GROUP):
            s = jnp.where(ok, s_next, NEG_INF)
            if r + 1 < NSA_HEADS_PER_GROUP:
                s_next = scores(r + 1)
            m_prev = m_s[r]
            m_new = jnp.maximum(m_prev, jnp.max(s, axis=0, keepdims=True))
            alpha = jnp.exp2(m_prev - m_new)
            p = jnp.exp2(s - m_new)
            l_s[r] = alpha * l_s[r] + jnp.sum(p, axis=0, keepdims=True)
            acc_s[r] = alpha * acc_s[r] + lax.dot_general(
                vblk, p.astype(jnp.bfloat16), (((0,), (0,)), ((), ())),
                preferred_element_type=jnp.float32)
            m_s[r] = m_new
        return carry

    n_tiles = jnp.minimum(i, WINDOW // tk) + 1 if windowed else i + 1
    lax.fori_loop(0, n_tiles, body, 0)
    for r in range(NSA_HEADS_PER_GROUP):
        o_ref[0, :, r * D:(r + 1) * D] = (acc_s[r] / l_s[r]).T


def masked_attention(qs, kb, vb, sel):
    B, T, _ = qs.shape
    G, R, D = NSA_KV_GROUPS, NSA_HEADS_PER_GROUP, NSA_HEAD_DIM
    tq = ATTN_TILE
    windowed = sel is None
    in_specs = [pl.BlockSpec((1, tq, R * D), lambda b, g, i: (b, i, g)),
                pl.BlockSpec((1, T, D), lambda b, g, i: (b, 0, g)),
                pl.BlockSpec((1, T, D), lambda b, g, i: (b, 0, g))]
    args = [qs, kb, vb]
    if not windowed:
        in_specs.append(pl.BlockSpec((1, 1, T // SLC_BLOCK, tq), lambda b, g, i: (b, g, 0, i)))
        args.append(sel)
    return pl.pallas_call(
        functools.partial(_attn_kernel, windowed=windowed),
        grid=(B, G, T // tq),
        in_specs=in_specs,
        out_specs=pl.BlockSpec((1, tq, R * D), lambda b, g, i: (b, i, g)),
        out_shape=jax.ShapeDtypeStruct((B, T, G * R * D), jnp.float32),
        scratch_shapes=[pltpu.VMEM((R, 1, tq), jnp.float32),
                        pltpu.VMEM((R, 1, tq), jnp.float32),
                        pltpu.VMEM((R, D, tq), jnp.float32)],
        compiler_params=pltpu.CompilerParams(
            dimension_semantics=("parallel", "parallel", "arbitrary"), vmem_limit_bytes=VMEM_LIMIT),
        name="window_attention" if windowed else "selected_attention",
    )(*args)


def nsa_mixer(x, w_q, w_out, k_cmp, v_cmp, k_slc, v_slc, k_win, v_win):
    B, T, _ = x.shape
    H, D = NSA_HEADS, NSA_HEAD_DIM
    q = dense(x, w_q, 0, NSA_Q_WIDTH)
    gates = jax.nn.sigmoid(dense(x, w_q, NSA_Q_WIDTH)).reshape(B, T, 3, H, 1)
    q_scale = D ** -0.5 * math.log2(math.e)
    q_cmp = rope_cast(q, 0, NSA_Q_WIDTH, False, q_scale)
    q_rot = rope_cast(q, 0, NSA_Q_WIDTH, True, q_scale)
    o_cmp, sel = nsa_compressed_select(q_cmp, k_cmp, v_cmp)
    o_slc = masked_attention(q_rot, k_slc, v_slc, sel)
    o_win = masked_attention(q_rot, k_win, v_win, None)
    o = (gates[:, :, 0] * o_cmp.reshape(B, T, H, D) + gates[:, :, 1] * o_slc.reshape(B, T, H, D)
         + gates[:, :, 2] * o_win.reshape(B, T, H, D))
    return dense(o.reshape(B, T, NSA_Q_WIDTH), w_out)


def kernel(x, a_w_in, a_conv_w, a_a_log, a_dt_bias, a_norm_w, a_w_out, kv_w, cmp_pe, cmp_w1, cmp_w2,
           b_w_q, b_w_out, router_w, router_bias, moe_w_gate, moe_w_up, moe_w_down, ln_g, ln_b):
    B, T, D = x.shape
    xf = x.astype(jnp.float32).reshape(B * T, D)
    x16 = xf.astype(jnp.bfloat16)
    shared_kv = None
    for layer in range(DEPTH):
        xin = x16.reshape(B, T, D)
        if layer < N_A_LAYERS:
            mix = gdn_mixer(xin, a_w_in[layer], a_conv_w[layer], a_a_log[layer], a_dt_bias[layer],
                            a_norm_w[layer], a_w_out[layer])
        else:
            if shared_kv is None:
                shared_kv = nsa_shared_kv(xin, kv_w, cmp_pe, cmp_w1, cmp_w2)
            j = layer - N_A_LAYERS
            mix = nsa_mixer(xin, b_w_q[j], b_w_out[j], *shared_kv)
        xf, x16, logits = residual_layer_norm(xf, mix.reshape(B * T, D), ln_g[layer, 0], ln_b[layer, 0],
                                              router_w)
        ffn = moe_ffn(xf, logits, router_bias, moe_w_gate[layer], moe_w_up[layer], moe_w_down[layer])
        xf, x16 = residual_layer_norm(xf, ffn, ln_g[layer, 1], ln_b[layer, 1])
    return xf.reshape(B, T, D).astype(x.dtype)
```

```python
import functools
import math

import jax
import jax.numpy as jnp
import numpy as np
from jax import lax
from jax.experimental import pallas as pl
from jax.experimental.pallas import tpu as pltpu

D_MODEL = 2048
BATCH = 2
SEQ = 4096
DEPTH = 2
N_A_LAYERS = DEPTH // 2

GDN_HEAD_DIM = 128
GDN_QK_HEADS = D_MODEL // GDN_HEAD_DIM
GDN_V_HEADS = 2 * GDN_QK_HEADS
GDN_QK_WIDTH = GDN_QK_HEADS * GDN_HEAD_DIM
GDN_V_WIDTH = GDN_V_HEADS * GDN_HEAD_DIM
GDN_CONV_CH = 2 * GDN_QK_WIDTH + GDN_V_WIDTH
GDN_CHUNK = 64
GDN_CONV = 4
GDN_TILE = 256
GDN_HK_STEP = 4

NSA_HEAD_DIM = 128
NSA_HEADS = D_MODEL // NSA_HEAD_DIM
NSA_KV_GROUPS = 4
NSA_HEADS_PER_GROUP = NSA_HEADS // NSA_KV_GROUPS
NSA_Q_WIDTH = NSA_HEADS * NSA_HEAD_DIM
CMP_BLOCK = 32
CMP_STRIDE = 16
CMP_HIDDEN = 512
SLC_BLOCK = 64
SLC_TOPK = 16
SLC_LOCAL = 2
WINDOW = 512
WIN_Q_BLOCK = 128
SLC_Q_BLOCK = 64
ROPE_THETA = 10000.0

N_EXPERTS = 32
N_GROUPS = 8
EXPERTS_PER_GROUP = N_EXPERTS // N_GROUPS
TOP_K = 2
D_EXPERT = D_MODEL // 4

DEEPNORM_ALPHA = (2 * DEPTH) ** 0.25
LN_EPS = 1e-5
RMS_EPS = 1e-6
NEG_INF = -1e30
FORCE_SCORE = 1e6

LANE = 128
VMEM_LIMIT = 48 * 1024 * 1024
MOE_VMEM_LIMIT = 56 * 1024 * 1024
MOE_TILE = 256
ATTN_TILE = 256
T_BLOCKS = SEQ // SLC_BLOCK


def _mm_kernel(a_ref, b_ref, o_ref, b16_s):
    @pl.when(pl.program_id(1) == 0)
    def _():
        b16_s[...] = b_ref[...].astype(jnp.bfloat16)

    o_ref[...] = jnp.dot(a_ref[...], b16_s[...], preferred_element_type=jnp.float32)


def pmatmul(a, b, col_start=0, n_cols=None, row_block=0, tm=512):
    m, k = a.shape
    n_cols = b.shape[1] - col_start if n_cols is None else n_cols
    if n_cols % LANE or col_start % LANE:
        b = jnp.pad(b[:, col_start:col_start + n_cols], ((0, 0), (0, -n_cols % LANE)))
        return pmatmul(a, b)[:, :n_cols]
    tn = math.gcd(n_cols, 1024 if k <= 2048 else 512)
    assert col_start % tn == 0 and m % tm == 0 and b.shape[0] % k == 0
    off = col_start // tn
    return pl.pallas_call(
        _mm_kernel,
        grid=(n_cols // tn, m // tm),
        in_specs=[pl.BlockSpec((tm, k), lambda j, i: (i, 0)),
                  pl.BlockSpec((k, tn), lambda j, i: (row_block, j + off))],
        out_specs=pl.BlockSpec((tm, tn), lambda j, i: (i, j)),
        out_shape=jax.ShapeDtypeStruct((m, n_cols), jnp.float32),
        scratch_shapes=[pltpu.VMEM((k, tn), jnp.bfloat16)],
        compiler_params=pltpu.CompilerParams(
            dimension_semantics=("parallel", "arbitrary"), vmem_limit_bytes=VMEM_LIMIT),
        name="dense_matmul",
    )(a.astype(jnp.bfloat16), b.astype(jnp.float32))


def dense(x, w, col_start=0, n_cols=None):
    lead = x.shape[:-1]
    out = pmatmul(x.reshape(-1, x.shape[-1]), w, col_start, n_cols)
    return out.reshape(lead + (out.shape[-1],))


def _moe_kernel(tile_expert_ref, n_tiles_ref, live_ref, tok_ref, tok_next_ref, dst_ref, x_hbm, gate_ref,
                wg_ref, wu_ref, wd_ref, out_hbm, xbuf, ybuf, wg16, wu16, wd16, sem_in, sem_out):
    i = pl.program_id(0)
    n_tiles = n_tiles_ref[0]
    slot = i % 2

    def start_gather(idx_ref, s):
        def body(r, c):
            pltpu.make_async_copy(x_hbm.at[pl.ds(idx_ref[0, 0, r], 1)], xbuf.at[s, pl.ds(r, 1)],
                                  sem_in.at[s]).start()
            return c
        lax.fori_loop(0, MOE_TILE, body, 0, unroll=8)

    def wait_gather(s):
        pltpu.make_async_copy(x_hbm.at[pl.ds(0, MOE_TILE)], xbuf.at[s], sem_in.at[s]).wait()

    def row_out(s, r, dst_row):
        return pltpu.make_async_copy(ybuf.at[s, pl.ds(r, 1)], out_hbm.at[pl.ds(dst_row, 1)], sem_out.at[s])

    def start_scatter(s):
        def body(r, c):
            row_out(s, r, dst_ref[0, 0, r]).start()
            return c
        lax.fori_loop(0, live_ref[i], body, 0)

    def wait_scatter(s, tile):
        def body(r, c):
            row_out(s, r, 0).wait()
            return c
        lax.fori_loop(0, live_ref[tile], body, 0)

    @pl.when(i < n_tiles)
    def _():
        @pl.when(i == 0)
        def _():
            start_gather(tok_ref, 0)

        @pl.when(i + 1 < n_tiles)
        def _():
            start_gather(tok_next_ref, 1 - slot)

        @pl.when(jnp.logical_or(i == 0, tile_expert_ref[i] != tile_expert_ref[jnp.maximum(i - 1, 0)]))
        def _():
            wg16[...] = wg_ref[0, 0].astype(jnp.bfloat16)
            wu16[...] = wu_ref[0, 0].astype(jnp.bfloat16)
            wd16[...] = wd_ref[0, 0].astype(jnp.bfloat16)

        wait_gather(slot)
        x = xbuf[slot].astype(jnp.bfloat16)
        g = jnp.dot(x, wg16[...], preferred_element_type=jnp.float32)
        u = jnp.dot(x, wu16[...], preferred_element_type=jnp.float32)
        h = (g * jax.nn.sigmoid(g)) * u * gate_ref[...]
        ybuf[slot] = jnp.dot(h.astype(jnp.bfloat16), wd16[...], preferred_element_type=jnp.float32)

        @pl.when(i >= 1)
        def _():
            wait_scatter(1 - slot, i - 1)

        start_scatter(slot)

        @pl.when(i == n_tiles - 1)
        def _():
            wait_scatter(slot, i)


def moe_ffn(h, router_logits, router_bias, w_gate, w_up, w_down, layer):
    n_tok, D = h.shape
    aff = jax.nn.sigmoid(router_logits.astype(jnp.float32))
    biased = (aff + router_bias.astype(jnp.float32)).reshape(-1, N_GROUPS, EXPERTS_PER_GROUP)

    def top2(v):
        i1 = jnp.argmax(v, axis=-1)
        rest = jnp.where(jnp.arange(v.shape[-1]) == i1[..., None], -jnp.inf, v)
        i2 = jnp.argmax(rest, axis=-1)
        return jnp.max(v, axis=-1), jnp.max(rest, axis=-1), i1, i2

    g1, g2, _, _ = top2(biased)
    best_group = jnp.argmax(g1 + g2, axis=-1)
    cand = jnp.take_along_axis(biased, best_group[:, None, None], axis=1)[:, 0]
    _, _, i1, i2 = top2(cand)
    top_idx = best_group[:, None] * EXPERTS_PER_GROUP + jnp.stack([i1, i2], axis=-1)
    top_aff = jnp.take_along_axis(aff, top_idx, axis=-1)
    top_w = top_aff / jnp.sum(top_aff, axis=-1, keepdims=True)

    n_asg = n_tok * TOP_K
    max_tiles = n_asg // MOE_TILE + N_EXPERTS
    n_rows = max_tiles * MOE_TILE
    e_flat = top_idx.reshape(-1).astype(jnp.int32)
    hot = (e_flat[:, None] == jnp.arange(N_EXPERTS, dtype=jnp.int32)[None, :]).astype(jnp.int32)
    running = jnp.cumsum(hot, axis=0)
    rank = jnp.sum(hot * (running - 1), axis=1)
    counts = running[-1]
    tiles_per = (counts + MOE_TILE - 1) // MOE_TILE
    tile_end = jnp.cumsum(tiles_per)
    row_start = (tile_end - tiles_per) * MOE_TILE
    row_of_asg = jnp.sum(hot * row_start[None, :], axis=1) + rank
    asg_of_row = jnp.full((n_rows,), -1, jnp.int32).at[row_of_asg].set(jnp.arange(n_asg, dtype=jnp.int32))
    token_of_row = jnp.maximum(asg_of_row, 0) // TOP_K
    gate_of_row = jnp.where(asg_of_row >= 0, top_w.reshape(-1)[jnp.maximum(asg_of_row, 0)], 0.0)
    dst_of_row = jnp.maximum(asg_of_row, 0)
    live_rows = jnp.sum((asg_of_row >= 0).reshape(max_tiles, MOE_TILE), axis=1).astype(jnp.int32)
    n_tiles = tile_end[-1:].astype(jnp.int32)
    tile_expert = jnp.minimum(
        jnp.searchsorted(tile_end, jnp.arange(max_tiles, dtype=jnp.int32), side="right"),
        N_EXPERTS - 1).astype(jnp.int32)
    tile_expert = jnp.where(jnp.arange(max_tiles) < n_tiles[0], tile_expert,
                            tile_expert[jnp.maximum(n_tiles[0] - 1, 0)])

    wspec_in = pl.BlockSpec((1, 1, D, D_EXPERT), lambda i, te, nt, lv: (layer, te[i], 0, 0))
    idx_spec = pl.BlockSpec((1, 1, MOE_TILE), lambda i, te, nt, lv: (i, 0, 0), memory_space=pltpu.SMEM)
    next_spec = pl.BlockSpec((1, 1, MOE_TILE), lambda i, te, nt, lv: (jnp.minimum(i + 1, max_tiles - 1), 0, 0),
                             memory_space=pltpu.SMEM)
    tok3 = token_of_row.reshape(max_tiles, 1, MOE_TILE)
    y = pl.pallas_call(
        _moe_kernel,
        grid_spec=pltpu.PrefetchScalarGridSpec(
            num_scalar_prefetch=3,
            grid=(max_tiles,),
            in_specs=[idx_spec, next_spec, idx_spec,
                      pl.BlockSpec(memory_space=pl.ANY),
                      pl.BlockSpec((MOE_TILE, 1), lambda i, te, nt, lv: (i, 0)),
                      wspec_in, wspec_in,
                      pl.BlockSpec((1, 1, D_EXPERT, D), lambda i, te, nt, lv: (layer, te[i], 0, 0))],
            out_specs=pl.BlockSpec(memory_space=pl.ANY),
            scratch_shapes=[pltpu.VMEM((2, MOE_TILE, D), jnp.float32),
                            pltpu.VMEM((2, MOE_TILE, D), jnp.float32),
                            pltpu.VMEM((D, D_EXPERT), jnp.bfloat16),
                            pltpu.VMEM((D, D_EXPERT), jnp.bfloat16),
                            pltpu.VMEM((D_EXPERT, D), jnp.bfloat16),
                            pltpu.SemaphoreType.DMA((2,)), pltpu.SemaphoreType.DMA((2,))],
        ),
        out_shape=jax.ShapeDtypeStruct((n_asg, D), jnp.float32),
        compiler_params=pltpu.CompilerParams(
            dimension_semantics=("arbitrary",), vmem_limit_bytes=MOE_VMEM_LIMIT),
        name="routed_moe",
    )(tile_expert, n_tiles, live_rows, tok3, tok3, dst_of_row.reshape(max_tiles, 1, MOE_TILE), h,
      gate_of_row[:, None], w_gate, w_up, w_down)
    return y.reshape(n_tok, TOP_K * D)


def _residual_ln_kernel(x_ref, mix_ref, g_ref, b_ref, *rest, n_mix, with_router):
    if with_router:
        rw_ref, o_ref, o16_ref, logit_ref = rest
    else:
        o_ref, o16_ref = rest
    d = x_ref.shape[1]
    h = DEEPNORM_ALPHA * x_ref[...]
    for s in range(n_mix):
        h = h + mix_ref[:, s * d:(s + 1) * d]
    mu = jnp.mean(h, axis=-1, keepdims=True)
    var = jnp.mean(jnp.square(h - mu), axis=-1, keepdims=True)
    y = (h - mu) * lax.rsqrt(var + LN_EPS) * g_ref[...] + b_ref[...]
    o_ref[...] = y
    y16 = y.astype(jnp.bfloat16)
    o16_ref[...] = y16
    if with_router:
        logit_ref[...] = jnp.dot(y16, rw_ref[...].astype(jnp.bfloat16), preferred_element_type=jnp.float32)


def residual_layer_norm(x, mix, g, b, router_w=None):
    n, d = x.shape
    n_mix = mix.shape[1] // d
    tm = 256
    row = lambda i: (i, 0)
    fixed = lambda i: (0, 0)
    in_specs = [pl.BlockSpec((tm, d), row), pl.BlockSpec((tm, n_mix * d), row),
                pl.BlockSpec((1, d), fixed), pl.BlockSpec((1, d), fixed)]
    args = [x, mix, g.astype(jnp.float32).reshape(1, d), b.astype(jnp.float32).reshape(1, d)]
    out_specs = [pl.BlockSpec((tm, d), row), pl.BlockSpec((tm, d), row)]
    out_shape = [jax.ShapeDtypeStruct((n, d), jnp.float32), jax.ShapeDtypeStruct((n, d), jnp.bfloat16)]
    if router_w is not None:
        e_pad = -router_w.shape[1] % LANE
        rw = jnp.pad(router_w.astype(jnp.float32), ((0, 0), (0, e_pad)))
        in_specs.append(pl.BlockSpec(rw.shape, fixed))
        args.append(rw)
        out_specs.append(pl.BlockSpec((tm, rw.shape[1]), row))
        out_shape.append(jax.ShapeDtypeStruct((n, rw.shape[1]), jnp.float32))
    outs = pl.pallas_call(
        functools.partial(_residual_ln_kernel, n_mix=n_mix, with_router=router_w is not None),
        grid=(n // tm,),
        in_specs=in_specs, out_specs=out_specs, out_shape=out_shape,
        compiler_params=pltpu.CompilerParams(
            dimension_semantics=("parallel",), vmem_limit_bytes=VMEM_LIMIT),
        name="residual_layer_norm",
    )(*args)
    if router_w is not None:
        return outs[0], outs[1], outs[2][:, :router_w.shape[1]]
    return outs[0], outs[1]


def _softplus(x):
    return jnp.maximum(x, 0.0) + jnp.log1p(jnp.exp(-jnp.abs(x)))


def _bdot(a, b):
    return jnp.dot(a.astype(jnp.bfloat16), b.astype(jnp.bfloat16), preferred_element_type=jnp.float32)


def _bdot_nt(a, b):
    return lax.dot_general(a.astype(jnp.bfloat16), b.astype(jnp.bfloat16), (((1,), (1,)), ((), ())),
                           preferred_element_type=jnp.float32)


def _bdot_tn(a, b):
    return lax.dot_general(a.astype(jnp.bfloat16), b.astype(jnp.bfloat16), (((0,), (0,)), ((), ())),
                           preferred_element_type=jnp.float32)


def _gdn_prep_kernel(x_ref, halo_ref, w_ref, o_ref):
    i = pl.program_id(1)
    j = pl.program_id(2)
    tt = x_ref.shape[1]
    x = x_ref[0]
    halo = jnp.where(i == 0, 0.0, halo_ref[0])
    xx = jnp.concatenate([halo, x], axis=0)
    w = w_ref[...]
    y = w[3:4] * x
    for tap in range(GDN_CONV - 1):
        lo = 8 - (GDN_CONV - 1) + tap
        y = y + w[tap:tap + 1] * xx[lo:lo + tt]
    y = y * jax.nn.sigmoid(y)
    q_blocks = GDN_QK_WIDTH // x_ref.shape[2]

    @pl.when(j >= 2 * q_blocks)
    def _():
        o_ref[0] = y

    @pl.when(j < 2 * q_blocks)
    def _():
        scale = jnp.where(j < q_blocks, GDN_HEAD_DIM ** -0.5, 1.0)
        for h in range(x_ref.shape[2] // GDN_HEAD_DIM):
            yh = y[:, h * GDN_HEAD_DIM:(h + 1) * GDN_HEAD_DIM]
            inv = lax.rsqrt(jnp.sum(yh * yh, axis=-1, keepdims=True) + RMS_EPS)
            o_ref[0, :, h * GDN_HEAD_DIM:(h + 1) * GDN_HEAD_DIM] = yh * inv * scale


def gdn_prep(qkv, conv_w):
    B, T, CH = qkv.shape
    tt, tc = GDN_TILE, 512
    return pl.pallas_call(
        _gdn_prep_kernel,
        grid=(B, T // tt, CH // tc),
        in_specs=[pl.BlockSpec((1, tt, tc), lambda b, i, j: (b, i, j)),
                  pl.BlockSpec((1, 8, tc), lambda b, i, j: (b, jnp.maximum(i * (tt // 8) - 1, 0), j)),
                  pl.BlockSpec((GDN_CONV, tc), lambda b, i, j: (0, j))],
        out_specs=pl.BlockSpec((1, tt, tc), lambda b, i, j: (b, i, j)),
        out_shape=jax.ShapeDtypeStruct((B, T, CH), jnp.float32),
        compiler_params=pltpu.CompilerParams(
            dimension_semantics=("parallel", "parallel", "parallel"), vmem_limit_bytes=VMEM_LIMIT),
        name="gdn_conv_silu_norm",
    )(qkv, qkv, conv_w)


def _gdn_kernel(q_ref, k_ref, v_ref, z_ref, acol_ref, arow_ref, bcol_ref, alog_ref, dt_ref, nw_ref,
                o_ref, state_s):
    C, Dh = GDN_CHUNK, GDN_HEAD_DIM

    @pl.when(pl.program_id(2) == 0)
    def _():
        state_s[...] = jnp.zeros(state_s.shape, jnp.float32)

    row = lax.broadcasted_iota(jnp.int32, (C, C), 0)
    col = lax.broadcasted_iota(jnp.int32, (C, C), 1)
    causal = row >= col
    strict = row > col
    nw = nw_ref[...]
    probs = []
    for n, kh in [(n, kh) for n in range(GDN_TILE // C) for kh in range(GDN_HK_STEP)]:
        sl = slice(n * C, (n + 1) * C)
        q = q_ref[0, sl, kh * Dh:(kh + 1) * Dh]
        k = k_ref[0, sl, kh * Dh:(kh + 1) * Dh]
        kk = _bdot_nt(k, k)
        qk = _bdot_nt(q, k)
        for hh in range(2):
            vh = 2 * kh + hh
            neg_a = -jnp.exp(alog_ref[kh, :, hh:hh + 1])
            dt = dt_ref[kh, :, hh:hh + 1]
            g_col = neg_a * _softplus(acol_ref[0, kh, sl, hh:hh + 1] + dt)
            g_row = neg_a * _softplus(arow_ref[0, kh, hh:hh + 1, sl] + dt)
            beta = jax.nn.sigmoid(bcol_ref[0, kh, sl, hh:hh + 1])
            gc_col = jnp.sum(jnp.where(causal, g_row, 0.0), axis=1, keepdims=True)
            gc_row = jnp.sum(jnp.where(row <= col, g_col, 0.0), axis=0, keepdims=True)
            g_last = jnp.sum(g_row, axis=1, keepdims=True)
            decay = jnp.where(causal, jnp.exp(jnp.where(causal, gc_col - gc_row, 0.0)), 0.0)
            m = jnp.where(strict, beta * kk * decay, 0.0)
            e_col = jnp.exp(gc_col)
            v = v_ref[0, sl, vh * Dh:(vh + 1) * Dh]
            probs.append(dict(
                sl=sl, hh=vh, x=-m, p=m, a=qk * decay,
                rhs=jnp.concatenate([beta * v, beta * e_col * k], axis=1),
                qe=q * e_col, ke=k * jnp.exp(g_last - gc_col), s_decay=jnp.exp(g_last)))
    for _ in range(int(math.log2(C)) - 1):
        for pr in probs:
            pr["p"] = _bdot(pr["p"], pr["p"])
        for pr in probs:
            pr["x"] = pr["x"] + pr["p"] + _bdot(pr["x"], pr["p"])
    for pr in probs:
        pr["sol"] = pr["rhs"] + _bdot(pr["x"], pr["rhs"])
    for pr in probs:
        sl, hh = pr["sl"], pr["hh"]
        u, w = pr["sol"][:, :Dh], pr["sol"][:, Dh:]
        state = state_s[hh]
        ws = _bdot(jnp.concatenate([w, pr["qe"]], axis=0), state)
        v_new = u - ws[:C]
        o = ws[C:] + _bdot(pr["a"], v_new)
        state_s[hh] = state * pr["s_decay"] + _bdot_tn(pr["ke"], v_new)
        z = z_ref[0, sl, hh * Dh:(hh + 1) * Dh]
        o = o * lax.rsqrt(jnp.mean(o * o, axis=-1, keepdims=True) + RMS_EPS) * nw
        o_ref[0, sl, hh * Dh:(hh + 1) * Dh] = (o * (z * jax.nn.sigmoid(z))).astype(o_ref.dtype)


def gdn_core(qkv, z, a_raw, b_raw, a_log, dt_bias, norm_w):
    B, T, _ = qkv.shape
    Hk, Dh, tt = GDN_QK_HEADS, GDN_HEAD_DIM, GDN_TILE
    a4 = a_raw.reshape(B, T, Hk, 2)
    a_col = a4.transpose(0, 2, 1, 3)
    a_row = a4.transpose(0, 2, 3, 1)
    b_col = b_raw.reshape(B, T, Hk, 2).transpose(0, 2, 1, 3)
    hs = GDN_HK_STEP
    qw, vw = hs * Dh, 2 * hs * Dh
    k_off, v_off = GDN_QK_WIDTH // qw, 2 * GDN_QK_WIDTH // vw
    col_spec = pl.BlockSpec((1, hs, tt, 2), lambda b, h, c: (b, h, c, 0))
    head_spec = pl.BlockSpec((hs, 1, 2), lambda b, h, c: (h, 0, 0))
    return pl.pallas_call(
        _gdn_kernel,
        grid=(B, Hk // hs, T // tt),
        in_specs=[pl.BlockSpec((1, tt, qw), lambda b, h, c: (b, c, h)),
                  pl.BlockSpec((1, tt, qw), lambda b, h, c: (b, c, k_off + h)),
                  pl.BlockSpec((1, tt, vw), lambda b, h, c: (b, c, v_off + h)),
                  pl.BlockSpec((1, tt, vw), lambda b, h, c: (b, c, h)),
                  col_spec,
                  pl.BlockSpec((1, hs, 2, tt), lambda b, h, c: (b, h, 0, c)),
                  col_spec, head_spec, head_spec,
                  pl.BlockSpec((1, Dh), lambda b, h, c: (0, 0))],
        out_specs=pl.BlockSpec((1, tt, vw), lambda b, h, c: (b, c, h)),
        out_shape=jax.ShapeDtypeStruct((B, T, GDN_V_WIDTH), jnp.bfloat16),
        scratch_shapes=[pltpu.VMEM((2 * hs, Dh, Dh), jnp.float32)],
        compiler_params=pltpu.CompilerParams(
            dimension_semantics=("parallel", "parallel", "arbitrary"), vmem_limit_bytes=VMEM_LIMIT),
        name="gated_delta_rule",
    )(qkv, qkv, qkv, z, a_col, a_row, b_col,
      a_log.astype(jnp.float32).reshape(Hk, 1, 2), dt_bias.astype(jnp.float32).reshape(Hk, 1, 2),
      norm_w.astype(jnp.float32).reshape(1, Dh))


def gdn_mixer(x, w_in, conv_w, a_log, dt_bias, norm_w, w_out):
    B, T, _ = x.shape
    Hk, Hv, Dh = GDN_QK_HEADS, GDN_V_HEADS, GDN_HEAD_DIM
    qkv = dense(x, w_in, 0, GDN_CONV_CH)
    z = dense(x, w_in, GDN_CONV_CH, GDN_V_WIDTH)
    a_raw, b_raw = jnp.split(dense(x, w_in, GDN_CONV_CH + GDN_V_WIDTH), [Hv], axis=-1)
    o = gdn_core(gdn_prep(qkv, conv_w), z, a_raw, b_raw, a_log, dt_bias, norm_w)
    return dense(o, w_out)


def _rope_cast_kernel(x_ref, cos_ref, sin_ref, o_ref, *, rotate, scale):
    D = NSA_HEAD_DIM
    for h in range(x_ref.shape[2] // D):
        x = x_ref[0, :, h * D:(h + 1) * D]
        if rotate:
            x = x * cos_ref[...] + pltpu.roll(x, D // 2, axis=1) * sin_ref[...]
        if scale != 1.0:
            x = x * scale
        o_ref[0, :, h * D:(h + 1) * D] = x.astype(o_ref.dtype)


def rope_cast(x, col_start, n_cols, rotate, scale=1.0):
    B, T, _ = x.shape
    tt, tc, D = 512, 512, NSA_HEAD_DIM
    half = D // 2
    inv_freq = ROPE_THETA ** (-jnp.arange(half, dtype=jnp.float32) / half)
    ang = jnp.arange(T, dtype=jnp.float32)[:, None] * inv_freq[None, :]
    cos2 = jnp.concatenate([jnp.cos(ang), jnp.cos(ang)], axis=1)
    sin2 = jnp.concatenate([-jnp.sin(ang), jnp.sin(ang)], axis=1)
    off = col_start // tc
    return pl.pallas_call(
        functools.partial(_rope_cast_kernel, rotate=rotate, scale=scale),
        grid=(B, T // tt, n_cols // tc),
        in_specs=[pl.BlockSpec((1, tt, tc), lambda b, i, j: (b, i, off + j)),
                  pl.BlockSpec((tt, D), lambda b, i, j: (i, 0)),
                  pl.BlockSpec((tt, D), lambda b, i, j: (i, 0))],
        out_specs=pl.BlockSpec((1, tt, tc), lambda b, i, j: (b, i, j)),
        out_shape=jax.ShapeDtypeStruct((B, T, n_cols), jnp.bfloat16),
        compiler_params=pltpu.CompilerParams(
            dimension_semantics=("parallel", "parallel", "parallel"), vmem_limit_bytes=VMEM_LIMIT),
        name="rope_cast",
    )(x, cos2, sin2)


def compress_blocks(a, pe, w1, w2):
    B, T, _ = a.shape
    G, D, S = NSA_KV_GROUPS, NSA_HEAD_DIM, CMP_STRIDE
    n_half = T // S
    n_cmp = (T - CMP_BLOCK) // S + 1
    halves = a.reshape(B, n_half, S, G, D).transpose(0, 3, 1, 2, 4).reshape(B * G * n_half, S * D)
    top = halves + pe[:S].reshape(1, S * D)
    bot = halves + pe[S:].reshape(1, S * D)
    p_top = pmatmul(top, w1, row_block=0).reshape(B * G, n_half, CMP_HIDDEN)
    p_bot = pmatmul(bot, w1, row_block=1).reshape(B * G, n_half, CMP_HIDDEN)
    hid = jax.nn.silu(p_top[:, :n_cmp] + p_bot[:, 1:n_cmp + 1])
    hid = jnp.pad(hid, ((0, 0), (0, n_half - n_cmp), (0, 0))).reshape(B * G * n_half, CMP_HIDDEN)
    out = pmatmul(hid, w2).reshape(B, G, n_half, D)
    live = (jnp.arange(n_half) < n_cmp)[None, None, :, None]
    return jnp.where(live, out, 0.0).astype(jnp.bfloat16)


def nsa_shared_kv(h, kv_w, cmp_pe, cmp_w1, cmp_w2):
    W = NSA_KV_GROUPS * NSA_HEAD_DIM
    kv = dense(h, kv_w)
    k_cmp = compress_blocks(kv[..., :W], cmp_pe[0], cmp_w1[0], cmp_w2[0])
    v_cmp = compress_blocks(kv[..., W:2 * W], cmp_pe[1], cmp_w1[1], cmp_w2[1])
    k_slc = rope_cast(kv, 2 * W, W, True)
    v_slc = rope_cast(kv, 3 * W, W, False)
    k_win = rope_cast(kv, 4 * W, W, True)
    v_win = rope_cast(kv, 5 * W, W, False)
    return (k_cmp, v_cmp, k_slc, v_slc, k_win, v_win)


def _cmp_select_kernel(q_ref, kc_ref, vc_ref, ov_ref, o_ref, sel_ref):
    i = pl.program_id(2)
    tq, D, n_c, n_s = ATTN_TILE, NSA_HEAD_DIM, kc_ref.shape[2], T_BLOCKS
    tpos = i * tq + lax.broadcasted_iota(jnp.int32, (n_c, tq), 1)
    block_end = lax.broadcasted_iota(jnp.int32, (n_c, tq), 0) * CMP_STRIDE + (CMP_BLOCK - 1)
    visible = block_end <= tpos
    kc = kc_ref[0, 0]
    vc = vc_ref[0, 0]
    p_sum = jnp.zeros((n_c, tq), jnp.float32)
    for r in range(NSA_HEADS_PER_GROUP):
        s = lax.dot_general(kc, q_ref[0, :, r * D:(r + 1) * D], (((1,), (1,)), ((), ())),
                            preferred_element_type=jnp.float32)
        s = jnp.where(visible, s, NEG_INF)
        e = jnp.exp2(s - jnp.max(s, axis=0, keepdims=True))
        p = jnp.where(visible, e / jnp.sum(e, axis=0, keepdims=True), 0.0)
        o_t = lax.dot_general(vc, p.astype(jnp.bfloat16), (((0,), (0,)), ((), ())),
                              preferred_element_type=jnp.float32)
        o_ref[0, :, r * D:(r + 1) * D] = o_t.T
        p_sum = p_sum + p
    p_slc = jnp.dot(ov_ref[...], p_sum.astype(jnp.bfloat16), preferred_element_type=jnp.float32)
    blk = lax.broadcasted_iota(jnp.int32, (n_s, tq), 0)
    cur = (i * tq + lax.broadcasted_iota(jnp.int32, (n_s, tq), 1)) // SLC_BLOCK
    causal_blk = blk <= cur
    forced = (blk == 0) | (causal_blk & (blk > cur - SLC_LOCAL))
    score = jnp.where(causal_blk, jnp.where(forced, FORCE_SCORE, p_slc), -1.0)
    rank = jnp.zeros((n_s, tq), jnp.float32)
    for other in range(n_s):
        row = score[other:other + 1, :]
        ahead = (row > score) | ((row == score) & (blk > other))
        rank = rank + jnp.where(ahead, 1.0, 0.0)
    picked = (rank < float(min(SLC_TOPK, n_s))) & (score >= 0.0)
    sel_ref[0, 0] = jnp.where(picked, 1.0, 0.0).astype(sel_ref.dtype)


def nsa_compressed_select(q_cmp, k_cmp, v_cmp):
    B, T, _ = q_cmp.shape
    G, R, D, tq = NSA_KV_GROUPS, NSA_HEADS_PER_GROUP, NSA_HEAD_DIM, ATTN_TILE
    n_c = k_cmp.shape[2]
    n_s = T // SLC_BLOCK
    c0 = np.arange(n_c) * CMP_STRIDE
    s0 = np.arange(n_s) * SLC_BLOCK
    ov = np.clip(np.minimum(c0[None, :] + CMP_BLOCK, s0[:, None] + SLC_BLOCK)
                 - np.maximum(c0[None, :], s0[:, None]), 0, None) / CMP_BLOCK
    cmp_spec = pl.BlockSpec((1, 1, n_c, D), lambda b, g, i: (b, g, 0, 0))
    return pl.pallas_call(
        _cmp_select_kernel,
        grid=(B, G, T // tq),
        in_specs=[pl.BlockSpec((1, tq, R * D), lambda b, g, i: (b, i, g)), cmp_spec, cmp_spec,
                  pl.BlockSpec((n_s, n_c), lambda b, g, i: (0, 0))],
        out_specs=[pl.BlockSpec((1, tq, R * D), lambda b, g, i: (b, i, g)),
                   pl.BlockSpec((1, 1, n_s, tq), lambda b, g, i: (b, g, 0, i))],
        out_shape=[jax.ShapeDtypeStruct((B, T, G * R * D), jnp.float32),
                   jax.ShapeDtypeStruct((B, G, n_s, T), jnp.bfloat16)],
        compiler_params=pltpu.CompilerParams(
            dimension_semantics=("parallel", "parallel", "parallel"), vmem_limit_bytes=VMEM_LIMIT),
        name="compressed_attention_select",
    )(q_cmp, k_cmp, v_cmp, jnp.asarray(ov, jnp.bfloat16))


def _attn_kernel(*refs, windowed):
    if windowed:
        q_ref, k_ref, v_ref, o_ref, m_s, l_s, acc_s = refs
    else:
        q_ref, k_ref, v_ref, sel_ref, o_ref, m_s, l_s, acc_s = refs
    i = pl.program_id(2)
    tq, tk, D = ATTN_TILE, ATTN_TILE, NSA_HEAD_DIM
    m_s[...] = jnp.full(m_s.shape, NEG_INF, jnp.float32)
    l_s[...] = jnp.zeros(l_s.shape, jnp.float32)
    acc_s[...] = jnp.zeros(acc_s.shape, jnp.float32)
    tpos = i * tq + lax.broadcasted_iota(jnp.int32, (tk, tq), 1)

    def body(jj, carry):
        j = i - jj
        start = pl.multiple_of(j * tk, tk)
        kblk = k_ref[0, pl.ds(start, tk), :]
        vblk = v_ref[0, pl.ds(start, tk), :]
        kpos = j * tk + lax.broadcasted_iota(jnp.int32, (tk, tq), 0)
        ok = kpos <= tpos
        if windowed:
            ok = ok & (kpos > tpos - WINDOW)
        else:
            blk_of_key = j * (tk // SLC_BLOCK) + lax.broadcasted_iota(
                jnp.int32, (tk, T_BLOCKS), 0) // SLC_BLOCK
            expand = (lax.broadcasted_iota(jnp.int32, (tk, T_BLOCKS), 1) == blk_of_key)
            selm = jnp.dot(expand.astype(jnp.bfloat16), sel_ref[0, 0],
                           preferred_element_type=jnp.float32)
            ok = ok & (selm > 0.5)
        def scores(r):
            return lax.dot_general(kblk, q_ref[0, :, r * D:(r + 1) * D], (((1,), (1,)), ((), ())),
                                   preferred_element_type=jnp.float32)

        s_next = scores(0)
        for r in range(NSA_HEADS_PER_GROUP):
            s = jnp.where(ok, s_next, NEG_INF)
            if r + 1 < NSA_HEADS_PER_GROUP:
                s_next = scores(r + 1)
            m_prev = m_s[r]
            m_new = jnp.maximum(m_prev, jnp.max(s, axis=0, keepdims=True))
            alpha = jnp.exp2(m_prev - m_new)
            p = jnp.exp2(s - m_new)
            l_s[r] = alpha * l_s[r] + jnp.sum(p, axis=0, keepdims=True)
            acc_s[r] = alpha * acc_s[r] + lax.dot_general(
                vblk, p.astype(jnp.bfloat16), (((0,), (0,)), ((), ())),
                preferred_element_type=jnp.float32)
            m_s[r] = m_new
        return carry

    n_tiles = jnp.minimum(i, WINDOW // tk) + 1 if windowed else i + 1
    lax.fori_loop(0, n_tiles, body, 0)
    for r in range(NSA_HEADS_PER_GROUP):
        o_ref[0, :, r * D:(r + 1) * D] = (acc_s[r] / l_s[r]).T


def masked_attention(qs, kb, vb, sel):
    B, T, _ = qs.shape
    G, R, D = NSA_KV_GROUPS, NSA_HEADS_PER_GROUP, NSA_HEAD_DIM
    tq = ATTN_TILE
    windowed = sel is None
    in_specs = [pl.BlockSpec((1, tq, R * D), lambda b, g, i: (b, i, g)),
                pl.BlockSpec((1, T, D), lambda b, g, i: (b, 0, g)),
                pl.BlockSpec((1, T, D), lambda b, g, i: (b, 0, g))]
    args = [qs, kb, vb]
    if not windowed:
        in_specs.append(pl.BlockSpec((1, 1, T // SLC_BLOCK, tq), lambda b, g, i: (b, g, 0, i)))
        args.append(sel)
    return pl.pallas_call(
        functools.partial(_attn_kernel, windowed=windowed),
        grid=(B, G, T // tq),
        in_specs=in_specs,
        out_specs=pl.BlockSpec((1, tq, R * D), lambda b, g, i: (b, i, g)),
        out_shape=jax.ShapeDtypeStruct((B, T, G * R * D), jnp.float32),
        scratch_shapes=[pltpu.VMEM((R, 1, tq), jnp.float32),
                        pltpu.VMEM((R, 1, tq), jnp.float32),
                        pltpu.VMEM((R, D, tq), jnp.float32)],
        compiler_params=pltpu.CompilerParams(
            dimension_semantics=("parallel", "parallel", "arbitrary"), vmem_limit_bytes=VMEM_LIMIT),
        name="window_attention" if windowed else "selected_attention",
    )(*args)


def nsa_mixer(x, w_q, w_out, k_cmp, v_cmp, k_slc, v_slc, k_win, v_win):
    B, T, _ = x.shape
    H, D = NSA_HEADS, NSA_HEAD_DIM
    q = dense(x, w_q, 0, NSA_Q_WIDTH)
    gates = jax.nn.sigmoid(dense(x, w_q, NSA_Q_WIDTH)).reshape(B, T, 3, H, 1)
    q_scale = D ** -0.5 * math.log2(math.e)
    q_cmp = rope_cast(q, 0, NSA_Q_WIDTH, False, q_scale)
    q_rot = rope_cast(q, 0, NSA_Q_WIDTH, True, q_scale)
    o_cmp, sel = nsa_compressed_select(q_cmp, k_cmp, v_cmp)
    o_slc = masked_attention(q_rot, k_slc, v_slc, sel)
    o_win = masked_attention(q_rot, k_win, v_win, None)
    o = (gates[:, :, 0] * o_cmp.reshape(B, T, H, D) + gates[:, :, 1] * o_slc.reshape(B, T, H, D)
         + gates[:, :, 2] * o_win.reshape(B, T, H, D))
    return dense(o.reshape(B, T, NSA_Q_WIDTH), w_out)


def kernel(x, a_w_in, a_conv_w, a_a_log, a_dt_bias, a_norm_w, a_w_out, kv_w, cmp_pe, cmp_w1, cmp_w2,
           b_w_q, b_w_out, router_w, router_bias, moe_w_gate, moe_w_up, moe_w_down, ln_g, ln_b):
    B, T, D = x.shape
    xf = x.astype(jnp.float32).reshape(B * T, D)
    x16 = xf.astype(jnp.bfloat16)
    shared_kv = None
    for layer in range(DEPTH):
        xin = x16.reshape(B, T, D)
        if layer < N_A_LAYERS:
            mix = gdn_mixer(xin, a_w_in[layer], a_conv_w[layer], a_a_log[layer], a_dt_bias[layer],
                            a_norm_w[layer], a_w_out[layer])
        else:
            if shared_kv is None:
                shared_kv = nsa_shared_kv(xin, kv_w, cmp_pe, cmp_w1, cmp_w2)
            j = layer - N_A_LAYERS
            mix = nsa_mixer(xin, b_w_q[j], b_w_out[j], *shared_kv)
        xf, x16, logits = residual_layer_norm(xf, mix.reshape(B * T, D), ln_g[layer, 0], ln_b[layer, 0],
                                              router_w)
        ffn = moe_ffn(xf, logits, router_bias, moe_w_gate, moe_w_up, moe_w_down, layer)
        xf, x16 = residual_layer_norm(xf, ffn, ln_g[layer, 1], ln_b[layer, 1])
    return xf.reshape(B, T, D).astype(x.dtype)
```

```python
import functools
import math

import jax
import jax.numpy as jnp
import numpy as np
from jax import lax
from jax.experimental import pallas as pl
from jax.experimental.pallas import tpu as pltpu

D_MODEL = 2048
BATCH = 2
SEQ = 4096
DEPTH = 2
N_A_LAYERS = DEPTH // 2

GDN_HEAD_DIM = 128
GDN_QK_HEADS = D_MODEL // GDN_HEAD_DIM
GDN_V_HEADS = 2 * GDN_QK_HEADS
GDN_QK_WIDTH = GDN_QK_HEADS * GDN_HEAD_DIM
GDN_V_WIDTH = GDN_V_HEADS * GDN_HEAD_DIM
GDN_CONV_CH = 2 * GDN_QK_WIDTH + GDN_V_WIDTH
GDN_CHUNK = 64
GDN_CONV = 4
GDN_TILE = 256
GDN_HK_STEP = 4

NSA_HEAD_DIM = 128
NSA_HEADS = D_MODEL // NSA_HEAD_DIM
NSA_KV_GROUPS = 4
NSA_HEADS_PER_GROUP = NSA_HEADS // NSA_KV_GROUPS
NSA_Q_WIDTH = NSA_HEADS * NSA_HEAD_DIM
CMP_BLOCK = 32
CMP_STRIDE = 16
CMP_HIDDEN = 512
SLC_BLOCK = 64
SLC_TOPK = 16
SLC_LOCAL = 2
WINDOW = 512
WIN_Q_BLOCK = 128
SLC_Q_BLOCK = 64
ROPE_THETA = 10000.0

N_EXPERTS = 32
N_GROUPS = 8
EXPERTS_PER_GROUP = N_EXPERTS // N_GROUPS
TOP_K = 2
D_EXPERT = D_MODEL // 4

DEEPNORM_ALPHA = (2 * DEPTH) ** 0.25
LN_EPS = 1e-5
RMS_EPS = 1e-6
NEG_INF = -1e30
FORCE_SCORE = 1e6

LANE = 128
VMEM_LIMIT = 48 * 1024 * 1024
MOE_VMEM_LIMIT = 56 * 1024 * 1024
MOE_TILE = 256
ATTN_TILE = 256
ATTN_TK = 256
T_BLOCKS = SEQ // SLC_BLOCK


def _mm_kernel(a_ref, b_ref, o_ref, b16_s):
    @pl.when(pl.program_id(1) == 0)
    def _():
        b16_s[...] = b_ref[...].astype(jnp.bfloat16)

    o_ref[...] = jnp.dot(a_ref[...], b16_s[...], preferred_element_type=jnp.float32)


def pmatmul(a, b, col_start=0, n_cols=None, row_block=0, tm=512):
    m, k = a.shape
    n_cols = b.shape[1] - col_start if n_cols is None else n_cols
    if n_cols % LANE or col_start % LANE:
        b = jnp.pad(b[:, col_start:col_start + n_cols], ((0, 0), (0, -n_cols % LANE)))
        return pmatmul(a, b)[:, :n_cols]
    tn = math.gcd(n_cols, 1024 if k <= 2048 else 512)
    assert col_start % tn == 0 and m % tm == 0 and b.shape[0] % k == 0
    off = col_start // tn
    return pl.pallas_call(
        _mm_kernel,
        grid=(n_cols // tn, m // tm),
        in_specs=[pl.BlockSpec((tm, k), lambda j, i: (i, 0)),
                  pl.BlockSpec((k, tn), lambda j, i: (row_block, j + off))],
        out_specs=pl.BlockSpec((tm, tn), lambda j, i: (i, j)),
        out_shape=jax.ShapeDtypeStruct((m, n_cols), jnp.float32),
        scratch_shapes=[pltpu.VMEM((k, tn), jnp.bfloat16)],
        compiler_params=pltpu.CompilerParams(
            dimension_semantics=("parallel", "arbitrary"), vmem_limit_bytes=VMEM_LIMIT),
        name="dense_matmul",
    )(a.astype(jnp.bfloat16), b.astype(jnp.float32))


def dense(x, w, col_start=0, n_cols=None):
    lead = x.shape[:-1]
    out = pmatmul(x.reshape(-1, x.shape[-1]), w, col_start, n_cols)
    return out.reshape(lead + (out.shape[-1],))


def _moe_kernel(tile_expert_ref, n_tiles_ref, tok_ref, tok_next_ref, dst_prev_ref, x_hbm, gate_ref,
                wg_ref, wu_ref, wd_ref, out_hbm, xbuf, ybuf, wg16, wu16, wd16, sem_in, sem_out):
    i = pl.program_id(0)
    n_tiles = n_tiles_ref[0]
    slot = i % 2
    spare = out_hbm.shape[0] - MOE_TILE

    def gather_row(idx_ref, s, r):
        return pltpu.make_async_copy(x_hbm.at[pl.ds(idx_ref[0, 0, r], 1)], xbuf.at[s, pl.ds(r, 1)],
                                     sem_in.at[s])

    def scatter_row(s, r, dst_row):
        return pltpu.make_async_copy(ybuf.at[s, pl.ds(r, 1)], out_hbm.at[pl.ds(dst_row, 1)], sem_out)

    def wait_gather(s):
        pltpu.make_async_copy(x_hbm.at[pl.ds(0, MOE_TILE)], xbuf.at[s], sem_in.at[s]).wait()

    def wait_scatter():
        pltpu.make_async_copy(ybuf.at[0], out_hbm.at[pl.ds(0, MOE_TILE)], sem_out).wait()

    @pl.when(i == 0)
    def _():
        ybuf[...] = jnp.zeros(ybuf.shape, jnp.float32)

        def first_rows(r, c):
            gather_row(tok_ref, 0, r).start()
            scatter_row(0, r, spare + r).start()
            return c
        lax.fori_loop(0, MOE_TILE, first_rows, 0, unroll=8)

    @pl.when(i < n_tiles)
    def _():
        @pl.when(jnp.logical_or(i == 0, tile_expert_ref[i] != tile_expert_ref[jnp.maximum(i - 1, 0)]))
        def _():
            wg16[...] = wg_ref[0, 0].astype(jnp.bfloat16)
            wu16[...] = wu_ref[0, 0].astype(jnp.bfloat16)
            wd16[...] = wd_ref[0, 0].astype(jnp.bfloat16)

        wait_gather(slot)
        wait_scatter()
        x = xbuf[slot].astype(jnp.bfloat16)
        for r in range(MOE_TILE):
            gather_row(tok_next_ref, 1 - slot, r).start()
            scatter_row(1 - slot, r, dst_prev_ref[0, 0, r]).start()
        g = jnp.dot(x, wg16[...], preferred_element_type=jnp.float32)
        u = jnp.dot(x, wu16[...], preferred_element_type=jnp.float32)
        h = (g * jax.nn.sigmoid(g)) * u * gate_ref[...]
        ybuf[slot] = jnp.dot(h.astype(jnp.bfloat16), wd16[...], preferred_element_type=jnp.float32)

    @pl.when(i == n_tiles)
    def _():
        wait_scatter()

        def last_rows(r, c):
            scatter_row(1 - slot, r, dst_prev_ref[0, 0, r]).start()
            return c
        lax.fori_loop(0, MOE_TILE, last_rows, 0, unroll=8)
        wait_scatter()
        wait_gather(slot)


def moe_ffn(h, router_logits, router_bias, w_gate, w_up, w_down, layer):
    n_tok, D = h.shape
    aff = jax.nn.sigmoid(router_logits.astype(jnp.float32))
    biased = (aff + router_bias.astype(jnp.float32)).reshape(-1, N_GROUPS, EXPERTS_PER_GROUP)

    def top2(v):
        i1 = jnp.argmax(v, axis=-1)
        rest = jnp.where(jnp.arange(v.shape[-1]) == i1[..., None], -jnp.inf, v)
        i2 = jnp.argmax(rest, axis=-1)
        return jnp.max(v, axis=-1), jnp.max(rest, axis=-1), i1, i2

    g1, g2, _, _ = top2(biased)
    best_group = jnp.argmax(g1 + g2, axis=-1)
    cand = jnp.take_along_axis(biased, best_group[:, None, None], axis=1)[:, 0]
    _, _, i1, i2 = top2(cand)
    top_idx = best_group[:, None] * EXPERTS_PER_GROUP + jnp.stack([i1, i2], axis=-1)
    top_aff = jnp.take_along_axis(aff, top_idx, axis=-1)
    top_w = top_aff / jnp.sum(top_aff, axis=-1, keepdims=True)

    n_asg = n_tok * TOP_K
    max_tiles = n_asg // MOE_TILE + N_EXPERTS + 1
    n_rows = max_tiles * MOE_TILE
    e_flat = top_idx.reshape(-1).astype(jnp.int32)
    hot = (e_flat[:, None] == jnp.arange(N_EXPERTS, dtype=jnp.int32)[None, :]).astype(jnp.int32)
    running = jnp.cumsum(hot, axis=0)
    rank = jnp.sum(hot * (running - 1), axis=1)
    counts = running[-1]
    tiles_per = (counts + MOE_TILE - 1) // MOE_TILE
    tile_end = jnp.cumsum(tiles_per)
    row_start = (tile_end - tiles_per) * MOE_TILE
    row_of_asg = jnp.sum(hot * row_start[None, :], axis=1) + rank
    asg_of_row = jnp.full((n_rows,), -1, jnp.int32).at[row_of_asg].set(jnp.arange(n_asg, dtype=jnp.int32))
    token_of_row = jnp.maximum(asg_of_row, 0) // TOP_K
    gate_of_row = jnp.where(asg_of_row >= 0, top_w.reshape(-1)[jnp.maximum(asg_of_row, 0)], 0.0)
    spare_rows = n_asg + jnp.arange(MOE_TILE, dtype=jnp.int32)
    dst3 = jnp.where(asg_of_row >= 0, asg_of_row, jnp.tile(spare_rows, max_tiles)).reshape(max_tiles, 1, MOE_TILE)
    dst_prev3 = jnp.concatenate([spare_rows.reshape(1, 1, MOE_TILE), dst3[:-1]], axis=0)
    n_tiles = tile_end[-1:].astype(jnp.int32)
    tile_expert = jnp.minimum(
        jnp.searchsorted(tile_end, jnp.arange(max_tiles, dtype=jnp.int32), side="right"),
        N_EXPERTS - 1).astype(jnp.int32)
    tile_expert = jnp.where(jnp.arange(max_tiles) < n_tiles[0], tile_expert,
                            tile_expert[jnp.maximum(n_tiles[0] - 1, 0)])

    wspec_in = pl.BlockSpec((1, 1, D, D_EXPERT), lambda i, te, nt: (layer, te[i], 0, 0))
    idx_spec = pl.BlockSpec((1, 1, MOE_TILE), lambda i, te, nt: (i, 0, 0), memory_space=pltpu.SMEM)
    next_spec = pl.BlockSpec((1, 1, MOE_TILE), lambda i, te, nt: (jnp.minimum(i + 1, max_tiles - 1), 0, 0),
                             memory_space=pltpu.SMEM)
    tok3 = token_of_row.reshape(max_tiles, 1, MOE_TILE)
    y = pl.pallas_call(
        _moe_kernel,
        grid_spec=pltpu.PrefetchScalarGridSpec(
            num_scalar_prefetch=2,
            grid=(max_tiles,),
            in_specs=[idx_spec, next_spec, idx_spec,
                      pl.BlockSpec(memory_space=pl.ANY),
                      pl.BlockSpec((MOE_TILE, 1), lambda i, te, nt: (i, 0)),
                      wspec_in, wspec_in,
                      pl.BlockSpec((1, 1, D_EXPERT, D), lambda i, te, nt: (layer, te[i], 0, 0))],
            out_specs=pl.BlockSpec(memory_space=pl.ANY),
            scratch_shapes=[pltpu.VMEM((2, MOE_TILE, D), jnp.float32),
                            pltpu.VMEM((2, MOE_TILE, D), jnp.float32),
                            pltpu.VMEM((D, D_EXPERT), jnp.bfloat16),
                            pltpu.VMEM((D, D_EXPERT), jnp.bfloat16),
                            pltpu.VMEM((D_EXPERT, D), jnp.bfloat16),
                            pltpu.SemaphoreType.DMA((2,)), pltpu.SemaphoreType.DMA(())],
        ),
        out_shape=jax.ShapeDtypeStruct((n_asg + MOE_TILE, D), jnp.float32),
        compiler_params=pltpu.CompilerParams(
            dimension_semantics=("arbitrary",), vmem_limit_bytes=MOE_VMEM_LIMIT),
        name="routed_moe",
    )(tile_expert, n_tiles, tok3, tok3, dst_prev3, h, gate_of_row[:, None], w_gate, w_up, w_down)
    return y.reshape(-1, TOP_K * D)


def _residual_ln_kernel(x_ref, mix_ref, g_ref, b_ref, *rest, n_mix, with_router):
    if with_router:
        rw_ref, o_ref, o16_ref, logit_ref = rest
    else:
        o_ref, o16_ref = rest
    d = x_ref.shape[1]
    h = DEEPNORM_ALPHA * x_ref[...]
    for s in range(n_mix):
        h = h + mix_ref[:, s * d:(s + 1) * d]
    mu = jnp.mean(h, axis=-1, keepdims=True)
    var = jnp.mean(jnp.square(h - mu), axis=-1, keepdims=True)
    y = (h - mu) * lax.rsqrt(var + LN_EPS) * g_ref[...] + b_ref[...]
    o_ref[...] = y
    y16 = y.astype(jnp.bfloat16)
    o16_ref[...] = y16
    if with_router:
        logit_ref[...] = jnp.dot(y16, rw_ref[...].astype(jnp.bfloat16), preferred_element_type=jnp.float32)


def residual_layer_norm(x, mix, g, b, router_w=None):
    n, d = x.shape
    n_mix = mix.shape[1] // d
    tm = 256
    row = lambda i: (i, 0)
    fixed = lambda i: (0, 0)
    in_specs = [pl.BlockSpec((tm, d), row), pl.BlockSpec((tm, n_mix * d), row),
                pl.BlockSpec((1, d), fixed), pl.BlockSpec((1, d), fixed)]
    args = [x, mix, g.astype(jnp.float32).reshape(1, d), b.astype(jnp.float32).reshape(1, d)]
    out_specs = [pl.BlockSpec((tm, d), row), pl.BlockSpec((tm, d), row)]
    out_shape = [jax.ShapeDtypeStruct((n, d), jnp.float32), jax.ShapeDtypeStruct((n, d), jnp.bfloat16)]
    if router_w is not None:
        e_pad = -router_w.shape[1] % LANE
        rw = jnp.pad(router_w.astype(jnp.float32), ((0, 0), (0, e_pad)))
        in_specs.append(pl.BlockSpec(rw.shape, fixed))
        args.append(rw)
        out_specs.append(pl.BlockSpec((tm, rw.shape[1]), row))
        out_shape.append(jax.ShapeDtypeStruct((n, rw.shape[1]), jnp.float32))
    outs = pl.pallas_call(
        functools.partial(_residual_ln_kernel, n_mix=n_mix, with_router=router_w is not None),
        grid=(n // tm,),
        in_specs=in_specs, out_specs=out_specs, out_shape=out_shape,
        compiler_params=pltpu.CompilerParams(
            dimension_semantics=("parallel",), vmem_limit_bytes=VMEM_LIMIT),
        name="residual_layer_norm",
    )(*args)
    if router_w is not None:
        return outs[0], outs[1], outs[2][:, :router_w.shape[1]]
    return outs[0], outs[1]


def _softplus(x):
    return jnp.maximum(x, 0.0) + jnp.log1p(jnp.exp(-jnp.abs(x)))


def _bdot(a, b):
    return jnp.dot(a.astype(jnp.bfloat16), b.astype(jnp.bfloat16), preferred_element_type=jnp.float32)


def _bdot_nt(a, b):
    return lax.dot_general(a.astype(jnp.bfloat16), b.astype(jnp.bfloat16), (((1,), (1,)), ((), ())),
                           preferred_element_type=jnp.float32)


def _bdot_tn(a, b):
    return lax.dot_general(a.astype(jnp.bfloat16), b.astype(jnp.bfloat16), (((0,), (0,)), ((), ())),
                           preferred_element_type=jnp.float32)


def _gdn_prep_kernel(x_ref, halo_ref, w_ref, o_ref):
    i = pl.program_id(1)
    j = pl.program_id(2)
    tt = x_ref.shape[1]
    x = x_ref[0]
    halo = jnp.where(i == 0, 0.0, halo_ref[0])
    xx = jnp.concatenate([halo, x], axis=0)
    w = w_ref[...]
    y = w[3:4] * x
    for tap in range(GDN_CONV - 1):
        lo = 8 - (GDN_CONV - 1) + tap
        y = y + w[tap:tap + 1] * xx[lo:lo + tt]
    y = y * jax.nn.sigmoid(y)
    q_blocks = GDN_QK_WIDTH // x_ref.shape[2]

    @pl.when(j >= 2 * q_blocks)
    def _():
        o_ref[0] = y

    @pl.when(j < 2 * q_blocks)
    def _():
        scale = jnp.where(j < q_blocks, GDN_HEAD_DIM ** -0.5, 1.0)
        for h in range(x_ref.shape[2] // GDN_HEAD_DIM):
            yh = y[:, h * GDN_HEAD_DIM:(h + 1) * GDN_HEAD_DIM]
            inv = lax.rsqrt(jnp.sum(yh * yh, axis=-1, keepdims=True) + RMS_EPS)
            o_ref[0, :, h * GDN_HEAD_DIM:(h + 1) * GDN_HEAD_DIM] = yh * inv * scale


def gdn_prep(qkv, conv_w):
    B, T, CH = qkv.shape
    tt, tc = GDN_TILE, 512
    return pl.pallas_call(
        _gdn_prep_kernel,
        grid=(B, T // tt, CH // tc),
        in_specs=[pl.BlockSpec((1, tt, tc), lambda b, i, j: (b, i, j)),
                  pl.BlockSpec((1, 8, tc), lambda b, i, j: (b, jnp.maximum(i * (tt // 8) - 1, 0), j)),
                  pl.BlockSpec((GDN_CONV, tc), lambda b, i, j: (0, j))],
        out_specs=pl.BlockSpec((1, tt, tc), lambda b, i, j: (b, i, j)),
        out_shape=jax.ShapeDtypeStruct((B, T, CH), jnp.float32),
        compiler_params=pltpu.CompilerParams(
            dimension_semantics=("parallel", "parallel", "parallel"), vmem_limit_bytes=VMEM_LIMIT),
        name="gdn_conv_silu_norm",
    )(qkv, qkv, conv_w)


def _gdn_kernel(q_ref, k_ref, v_ref, z_ref, acol_ref, arow_ref, bcol_ref, alog_ref, dt_ref, nw_ref,
                o_ref, state_s):
    C, Dh = GDN_CHUNK, GDN_HEAD_DIM

    @pl.when(pl.program_id(2) == 0)
    def _():
        state_s[...] = jnp.zeros(state_s.shape, jnp.float32)

    row = lax.broadcasted_iota(jnp.int32, (C, C), 0)
    col = lax.broadcasted_iota(jnp.int32, (C, C), 1)
    causal = row >= col
    strict = row > col
    nw = nw_ref[...]
    probs = []
    for n, kh in [(n, kh) for n in range(GDN_TILE // C) for kh in range(GDN_HK_STEP)]:
        sl = slice(n * C, (n + 1) * C)
        q = q_ref[0, sl, kh * Dh:(kh + 1) * Dh]
        k = k_ref[0, sl, kh * Dh:(kh + 1) * Dh]
        kk = _bdot_nt(k, k)
        qk = _bdot_nt(q, k)
        for hh in range(2):
            vh = 2 * kh + hh
            neg_a = -jnp.exp(alog_ref[kh, :, hh:hh + 1])
            dt = dt_ref[kh, :, hh:hh + 1]
            g_col = neg_a * _softplus(acol_ref[0, kh, sl, hh:hh + 1] + dt)
            g_row = neg_a * _softplus(arow_ref[0, kh, hh:hh + 1, sl] + dt)
            beta = jax.nn.sigmoid(bcol_ref[0, kh, sl, hh:hh + 1])
            gc_col = jnp.sum(jnp.where(causal, g_row, 0.0), axis=1, keepdims=True)
            gc_row = jnp.sum(jnp.where(row <= col, g_col, 0.0), axis=0, keepdims=True)
            g_last = jnp.sum(g_row, axis=1, keepdims=True)
            decay = jnp.where(causal, jnp.exp(jnp.where(causal, gc_col - gc_row, 0.0)), 0.0)
            m = jnp.where(strict, beta * kk * decay, 0.0)
            e_col = jnp.exp(gc_col)
            v = v_ref[0, sl, vh * Dh:(vh + 1) * Dh]
            probs.append(dict(
                sl=sl, hh=vh, x=-m, p=m, a=qk * decay,
                rhs=jnp.concatenate([beta * v, beta * e_col * k], axis=1),
                qe=q * e_col, ke=k * jnp.exp(g_last - gc_col), s_decay=jnp.exp(g_last)))
    for _ in range(int(math.log2(C)) - 1):
        for pr in probs:
            pr["p"] = _bdot(pr["p"], pr["p"])
        for pr in probs:
            pr["x"] = pr["x"] + pr["p"] + _bdot(pr["x"], pr["p"])
    for pr in probs:
        pr["sol"] = pr["rhs"] + _bdot(pr["x"], pr["rhs"])
    for pr in probs:
        sl, hh = pr["sl"], pr["hh"]
        u, w = pr["sol"][:, :Dh], pr["sol"][:, Dh:]
        state = state_s[hh]
        ws = _bdot(jnp.concatenate([w, pr["qe"]], axis=0), state)
        v_new = u - ws[:C]
        o = ws[C:] + _bdot(pr["a"], v_new)
        state_s[hh] = state * pr["s_decay"] + _bdot_tn(pr["ke"], v_new)
        z = z_ref[0, sl, hh * Dh:(hh + 1) * Dh]
        o = o * lax.rsqrt(jnp.mean(o * o, axis=-1, keepdims=True) + RMS_EPS) * nw
        o_ref[0, sl, hh * Dh:(hh + 1) * Dh] = (o * (z * jax.nn.sigmoid(z))).astype(o_ref.dtype)


def gdn_core(qkv, z, a_raw, b_raw, a_log, dt_bias, norm_w):
    B, T, _ = qkv.shape
    Hk, Dh, tt = GDN_QK_HEADS, GDN_HEAD_DIM, GDN_TILE
    a4 = a_raw.reshape(B, T, Hk, 2)
    a_col = a4.transpose(0, 2, 1, 3)
    a_row = a4.transpose(0, 2, 3, 1)
    b_col = b_raw.reshape(B, T, Hk, 2).transpose(0, 2, 1, 3)
    hs = GDN_HK_STEP
    qw, vw = hs * Dh, 2 * hs * Dh
    k_off, v_off = GDN_QK_WIDTH // qw, 2 * GDN_QK_WIDTH // vw
    col_spec = pl.BlockSpec((1, hs, tt, 2), lambda b, h, c: (b, h, c, 0))
    head_spec = pl.BlockSpec((hs, 1, 2), lambda b, h, c: (h, 0, 0))
    return pl.pallas_call(
        _gdn_kernel,
        grid=(B, Hk // hs, T // tt),
        in_specs=[pl.BlockSpec((1, tt, qw), lambda b, h, c: (b, c, h)),
                  pl.BlockSpec((1, tt, qw), lambda b, h, c: (b, c, k_off + h)),
                  pl.BlockSpec((1, tt, vw), lambda b, h, c: (b, c, v_off + h)),
                  pl.BlockSpec((1, tt, vw), lambda b, h, c: (b, c, h)),
                  col_spec,
                  pl.BlockSpec((1, hs, 2, tt), lambda b, h, c: (b, h, 0, c)),
                  col_spec, head_spec, head_spec,
                  pl.BlockSpec((1, Dh), lambda b, h, c: (0, 0))],
        out_specs=pl.BlockSpec((1, tt, vw), lambda b, h, c: (b, c, h)),
        out_shape=jax.ShapeDtypeStruct((B, T, GDN_V_WIDTH), jnp.bfloat16),
        scratch_shapes=[pltpu.VMEM((2 * hs, Dh, Dh), jnp.float32)],
        compiler_params=pltpu.CompilerParams(
            dimension_semantics=("parallel", "parallel", "arbitrary"), vmem_limit_bytes=VMEM_LIMIT),
        name="gated_delta_rule",
    )(qkv, qkv, qkv, z, a_col, a_row, b_col,
      a_log.astype(jnp.float32).reshape(Hk, 1, 2), dt_bias.astype(jnp.float32).reshape(Hk, 1, 2),
      norm_w.astype(jnp.float32).reshape(1, Dh))


def gdn_mixer(x, w_in, conv_w, a_log, dt_bias, norm_w, w_out):
    B, T, _ = x.shape
    Hk, Hv, Dh = GDN_QK_HEADS, GDN_V_HEADS, GDN_HEAD_DIM
    qkv = dense(x, w_in, 0, GDN_CONV_CH)
    z = dense(x, w_in, GDN_CONV_CH, GDN_V_WIDTH)
    a_raw, b_raw = jnp.split(dense(x, w_in, GDN_CONV_CH + GDN_V_WIDTH), [Hv], axis=-1)
    o = gdn_core(gdn_prep(qkv, conv_w), z, a_raw, b_raw, a_log, dt_bias, norm_w)
    return dense(o, w_out)


def _rope_cast_kernel(x_ref, cos_ref, sin_ref, o_ref, *, rotate, scale):
    D = NSA_HEAD_DIM
    for h in range(x_ref.shape[2] // D):
        x = x_ref[0, :, h * D:(h + 1) * D]
        if rotate:
            x = x * cos_ref[...] + pltpu.roll(x, D // 2, axis=1) * sin_ref[...]
        if scale != 1.0:
            x = x * scale
        o_ref[0, :, h * D:(h + 1) * D] = x.astype(o_ref.dtype)


def rope_cast(x, col_start, n_cols, rotate, scale=1.0):
    B, T, _ = x.shape
    tt, tc, D = 512, 512, NSA_HEAD_DIM
    half = D // 2
    inv_freq = ROPE_THETA ** (-jnp.arange(half, dtype=jnp.float32) / half)
    ang = jnp.arange(T, dtype=jnp.float32)[:, None] * inv_freq[None, :]
    cos2 = jnp.concatenate([jnp.cos(ang), jnp.cos(ang)], axis=1)
    sin2 = jnp.concatenate([-jnp.sin(ang), jnp.sin(ang)], axis=1)
    off = col_start // tc
    return pl.pallas_call(
        functools.partial(_rope_cast_kernel, rotate=rotate, scale=scale),
        grid=(B, T // tt, n_cols // tc),
        in_specs=[pl.BlockSpec((1, tt, tc), lambda b, i, j: (b, i, off + j)),
                  pl.BlockSpec((tt, D), lambda b, i, j: (i, 0)),
                  pl.BlockSpec((tt, D), lambda b, i, j: (i, 0))],
        out_specs=pl.BlockSpec((1, tt, tc), lambda b, i, j: (b, i, j)),
        out_shape=jax.ShapeDtypeStruct((B, T, n_cols), jnp.bfloat16),
        compiler_params=pltpu.CompilerParams(
            dimension_semantics=("parallel", "parallel", "parallel"), vmem_limit_bytes=VMEM_LIMIT),
        name="rope_cast",
    )(x, cos2, sin2)


def compress_blocks(a, pe, w1, w2):
    B, T, _ = a.shape
    G, D, S = NSA_KV_GROUPS, NSA_HEAD_DIM, CMP_STRIDE
    n_half = T // S
    n_cmp = (T - CMP_BLOCK) // S + 1
    halves = a.reshape(B, n_half, S, G, D).transpose(0, 3, 1, 2, 4).reshape(B * G * n_half, S * D)
    top = halves + pe[:S].reshape(1, S * D)
    bot = halves + pe[S:].reshape(1, S * D)
    p_top = pmatmul(top, w1, row_block=0).reshape(B * G, n_half, CMP_HIDDEN)
    p_bot = pmatmul(bot, w1, row_block=1).reshape(B * G, n_half, CMP_HIDDEN)
    hid = jax.nn.silu(p_top[:, :n_cmp] + p_bot[:, 1:n_cmp + 1])
    hid = jnp.pad(hid, ((0, 0), (0, n_half - n_cmp), (0, 0))).reshape(B * G * n_half, CMP_HIDDEN)
    out = pmatmul(hid, w2).reshape(B, G, n_half, D)
    live = (jnp.arange(n_half) < n_cmp)[None, None, :, None]
    return jnp.where(live, out, 0.0).astype(jnp.bfloat16)


def nsa_shared_kv(h, kv_w, cmp_pe, cmp_w1, cmp_w2):
    W = NSA_KV_GROUPS * NSA_HEAD_DIM
    kv = dense(h, kv_w)
    k_cmp = compress_blocks(kv[..., :W], cmp_pe[0], cmp_w1[0], cmp_w2[0])
    v_cmp = compress_blocks(kv[..., W:2 * W], cmp_pe[1], cmp_w1[1], cmp_w2[1])
    k_slc = rope_cast(kv, 2 * W, W, True)
    v_slc = rope_cast(kv, 3 * W, W, False)
    k_win = rope_cast(kv, 4 * W, W, True)
    v_win = rope_cast(kv, 5 * W, W, False)
    return (k_cmp, v_cmp, k_slc, v_slc, k_win, v_win)


def _cmp_select_kernel(q_ref, kc_ref, vc_ref, ov_ref, o_ref, sel_ref):
    i = pl.program_id(2)
    tq, D, n_c, n_s = ATTN_TILE, NSA_HEAD_DIM, kc_ref.shape[2], T_BLOCKS
    tpos = i * tq + lax.broadcasted_iota(jnp.int32, (n_c, tq), 1)
    block_end = lax.broadcasted_iota(jnp.int32, (n_c, tq), 0) * CMP_STRIDE + (CMP_BLOCK - 1)
    visible = block_end <= tpos
    kc = kc_ref[0, 0]
    vc = vc_ref[0, 0]
    p_sum = jnp.zeros((n_c, tq), jnp.float32)
    for r in range(NSA_HEADS_PER_GROUP):
        s = lax.dot_general(kc, q_ref[0, :, r * D:(r + 1) * D], (((1,), (1,)), ((), ())),
                            preferred_element_type=jnp.float32)
        s = jnp.where(visible, s, NEG_INF)
        e = jnp.exp2(s - jnp.max(s, axis=0, keepdims=True))
        p = jnp.where(visible, e / jnp.sum(e, axis=0, keepdims=True), 0.0)
        o_t = lax.dot_general(vc, p.astype(jnp.bfloat16), (((0,), (0,)), ((), ())),
                              preferred_element_type=jnp.float32)
        o_ref[0, :, r * D:(r + 1) * D] = o_t.T
        p_sum = p_sum + p
    p_slc = jnp.dot(ov_ref[...], p_sum.astype(jnp.bfloat16), preferred_element_type=jnp.float32)
    blk = lax.broadcasted_iota(jnp.int32, (n_s, tq), 0)
    cur = (i * tq + lax.broadcasted_iota(jnp.int32, (n_s, tq), 1)) // SLC_BLOCK
    causal_blk = blk <= cur
    forced = (blk == 0) | (causal_blk & (blk > cur - SLC_LOCAL))
    score = jnp.where(causal_blk, jnp.where(forced, FORCE_SCORE, p_slc), -1.0)
    rank = jnp.zeros((n_s, tq), jnp.float32)
    for other in range(n_s):
        row = score[other:other + 1, :]
        ahead = (row > score) | ((row == score) & (blk > other))
        rank = rank + jnp.where(ahead, 1.0, 0.0)
    picked = (rank < float(min(SLC_TOPK, n_s))) & (score >= 0.0)
    sel_ref[0, 0] = jnp.where(picked, 1.0, 0.0).astype(sel_ref.dtype)


def nsa_compressed_select(q_cmp, k_cmp, v_cmp):
    B, T, _ = q_cmp.shape
    G, R, D, tq = NSA_KV_GROUPS, NSA_HEADS_PER_GROUP, NSA_HEAD_DIM, ATTN_TILE
    n_c = k_cmp.shape[2]
    n_s = T // SLC_BLOCK
    c0 = np.arange(n_c) * CMP_STRIDE
    s0 = np.arange(n_s) * SLC_BLOCK
    ov = np.clip(np.minimum(c0[None, :] + CMP_BLOCK, s0[:, None] + SLC_BLOCK)
                 - np.maximum(c0[None, :], s0[:, None]), 0, None) / CMP_BLOCK
    cmp_spec = pl.BlockSpec((1, 1, n_c, D), lambda b, g, i: (b, g, 0, 0))
    return pl.pallas_call(
        _cmp_select_kernel,
        grid=(B, G, T // tq),
        in_specs=[pl.BlockSpec((1, tq, R * D), lambda b, g, i: (b, i, g)), cmp_spec, cmp_spec,
                  pl.BlockSpec((n_s, n_c), lambda b, g, i: (0, 0))],
        out_specs=[pl.BlockSpec((1, tq, R * D), lambda b, g, i: (b, i, g)),
                   pl.BlockSpec((1, 1, n_s, tq), lambda b, g, i: (b, g, 0, i))],
        out_shape=[jax.ShapeDtypeStruct((B, T, G * R * D), jnp.float32),
                   jax.ShapeDtypeStruct((B, G, n_s, T), jnp.bfloat16)],
        compiler_params=pltpu.CompilerParams(
            dimension_semantics=("parallel", "parallel", "parallel"), vmem_limit_bytes=VMEM_LIMIT),
        name="compressed_attention_select",
    )(q_cmp, k_cmp, v_cmp, jnp.asarray(ov, jnp.bfloat16))


def _attn_kernel(*refs, windowed):
    if windowed:
        q_ref, k_ref, v_ref, o_ref, m_s, l_s, acc_s = refs
    else:
        q_ref, k_ref, v_ref, sel_ref, o_ref, m_s, l_s, acc_s = refs
    i = pl.program_id(2)
    tq, tk, D = ATTN_TILE, ATTN_TK, NSA_HEAD_DIM
    j_hi = (i + 1) * (tq // tk) - 1
    m_s[...] = jnp.full(m_s.shape, NEG_INF, jnp.float32)
    l_s[...] = jnp.zeros(l_s.shape, jnp.float32)
    acc_s[...] = jnp.zeros(acc_s.shape, jnp.float32)
    tpos = i * tq + lax.broadcasted_iota(jnp.int32, (tk, tq), 1)

    def body(jj, carry):
        j = j_hi - jj
        start = pl.multiple_of(j * tk, tk)
        kblk = k_ref[0, pl.ds(start, tk), :]
        vblk = v_ref[0, pl.ds(start, tk), :]
        kpos = j * tk + lax.broadcasted_iota(jnp.int32, (tk, tq), 0)
        ok = kpos <= tpos
        if windowed:
            ok = ok & (kpos > tpos - WINDOW)
        else:
            blk_of_key = j * (tk // SLC_BLOCK) + lax.broadcasted_iota(
                jnp.int32, (tk, T_BLOCKS), 0) // SLC_BLOCK
            expand = (lax.broadcasted_iota(jnp.int32, (tk, T_BLOCKS), 1) == blk_of_key)
            selm = jnp.dot(expand.astype(jnp.bfloat16), sel_ref[0, 0],
                           preferred_element_type=jnp.float32)
            ok = ok & (selm > 0.5)
        def scores(r):
            return lax.dot_general(kblk, q_ref[0, :, r * D:(r + 1) * D], (((1,), (1,)), ((), ())),
                                   preferred_element_type=jnp.float32)

        s_next = scores(0)
        for r in range(NSA_HEADS_PER_GROUP):
            s = jnp.where(ok, s_next, NEG_INF)
            if r + 1 < NSA_HEADS_PER_GROUP:
                s_next = scores(r + 1)
            m_prev = m_s[r]
            m_new = jnp.maximum(m_prev, jnp.max(s, axis=0, keepdims=True))
            alpha = jnp.exp2(m_prev - m_new)
            p = jnp.exp2(s - m_new)
            l_s[r] = alpha * l_s[r] + jnp.sum(p, axis=0, keepdims=True)
            acc_s[r] = alpha * acc_s[r] + lax.dot_general(
                vblk, p.astype(jnp.bfloat16), (((0,), (0,)), ((), ())),
                preferred_element_type=jnp.float32)
            m_s[r] = m_new
        return carry

    j_lo = jnp.maximum(i * tq - WINDOW + 1, 0) // tk if windowed else 0
    lax.fori_loop(0, j_hi - j_lo + 1, body, 0)
    for r in range(NSA_HEADS_PER_GROUP):
        o_ref[0, :, r * D:(r + 1) * D] = (acc_s[r] / l_s[r]).T


def masked_attention(qs, kb, vb, sel):
    B, T, _ = qs.shape
    G, R, D = NSA_KV_GROUPS, NSA_HEADS_PER_GROUP, NSA_HEAD_DIM
    tq = ATTN_TILE
    windowed = sel is None
    in_specs = [pl.BlockSpec((1, tq, R * D), lambda b, g, i: (b, i, g)),
                pl.BlockSpec((1, T, D), lambda b, g, i: (b, 0, g)),
                pl.BlockSpec((1, T, D), lambda b, g, i: (b, 0, g))]
    args = [qs, kb, vb]
    if not windowed:
        in_specs.append(pl.BlockSpec((1, 1, T // SLC_BLOCK, tq), lambda b, g, i: (b, g, 0, i)))
        args.append(sel)
    return pl.pallas_call(
        functools.partial(_attn_kernel, windowed=windowed),
        grid=(B, G, T // tq),
        in_specs=in_specs,
        out_specs=pl.BlockSpec((1, tq, R * D), lambda b, g, i: (b, i, g)),
        out_shape=jax.ShapeDtypeStruct((B, T, G * R * D), jnp.float32),
        scratch_shapes=[pltpu.VMEM((R, 1, tq), jnp.float32),
                        pltpu.VMEM((R, 1, tq), jnp.float32),
                        pltpu.VMEM((R, D, tq), jnp.float32)],
        compiler_params=pltpu.CompilerParams(
            dimension_semantics=("parallel", "parallel", "arbitrary"), vmem_limit_bytes=VMEM_LIMIT),
        name="window_attention" if windowed else "selected_attention",
    )(*args)


def nsa_mixer(x, w_q, w_out, k_cmp, v_cmp, k_slc, v_slc, k_win, v_win):
    B, T, _ = x.shape
    H, D = NSA_HEADS, NSA_HEAD_DIM
    q = dense(x, w_q, 0, NSA_Q_WIDTH)
    gates = jax.nn.sigmoid(dense(x, w_q, NSA_Q_WIDTH)).reshape(B, T, 3, H, 1)
    q_scale = D ** -0.5 * math.log2(math.e)
    q_cmp = rope_cast(q, 0, NSA_Q_WIDTH, False, q_scale)
    q_rot = rope_cast(q, 0, NSA_Q_WIDTH, True, q_scale)
    o_cmp, sel = nsa_compressed_select(q_cmp, k_cmp, v_cmp)
    o_slc = masked_attention(q_rot, k_slc, v_slc, sel)
    o_win = masked_attention(q_rot, k_win, v_win, None)
    o = (gates[:, :, 0] * o_cmp.reshape(B, T, H, D) + gates[:, :, 1] * o_slc.reshape(B, T, H, D)
         + gates[:, :, 2] * o_win.reshape(B, T, H, D))
    return dense(o.reshape(B, T, NSA_Q_WIDTH), w_out)


def kernel(x, a_w_in, a_conv_w, a_a_log, a_dt_bias, a_norm_w, a_w_out, kv_w, cmp_pe, cmp_w1, cmp_w2,
           b_w_q, b_w_out, router_w, router_bias, moe_w_gate, moe_w_up, moe_w_down, ln_g, ln_b):
    B, T, D = x.shape
    xf = x.astype(jnp.float32).reshape(B * T, D)
    x16 = xf.astype(jnp.bfloat16)
    shared_kv = None
    for layer in range(DEPTH):
        xin = x16.reshape(B, T, D)
        if layer < N_A_LAYERS:
            mix = gdn_mixer(xin, a_w_in[layer], a_conv_w[layer], a_a_log[layer], a_dt_bias[layer],
                            a_norm_w[layer], a_w_out[layer])
        else:
            if shared_kv is None:
                shared_kv = nsa_shared_kv(xin, kv_w, cmp_pe, cmp_w1, cmp_w2)
            j = layer - N_A_LAYERS
            mix = nsa_mixer(xin, b_w_q[j], b_w_out[j], *shared_kv)
        xf, x16, logits = residual_layer_norm(xf, mix.reshape(B * T, D), ln_g[layer, 0], ln_b[layer, 0],
                                              router_w)
        ffn = moe_ffn(xf, logits, router_bias, moe_w_gate, moe_w_up, moe_w_down, layer)
        xf, x16 = residual_layer_norm(xf, ffn, ln_g[layer, 1], ln_b[layer, 1])
    return xf.reshape(B, T, D).astype(x.dtype)
```

```python
import functools
import math

import jax
import jax.numpy as jnp
import numpy as np
from jax import lax
from jax.experimental import pallas as pl
from jax.experimental.pallas import tpu as pltpu

D_MODEL = 2048
BATCH = 2
SEQ = 4096
DEPTH = 2
N_A_LAYERS = DEPTH // 2

GDN_HEAD_DIM = 128
GDN_QK_HEADS = D_MODEL // GDN_HEAD_DIM
GDN_V_HEADS = 2 * GDN_QK_HEADS
GDN_QK_WIDTH = GDN_QK_HEADS * GDN_HEAD_DIM
GDN_V_WIDTH = GDN_V_HEADS * GDN_HEAD_DIM
GDN_CONV_CH = 2 * GDN_QK_WIDTH + GDN_V_WIDTH
GDN_CHUNK = 64
GDN_CONV = 4
GDN_TILE = 256
GDN_HK_STEP = 4

NSA_HEAD_DIM = 128
NSA_HEADS = D_MODEL // NSA_HEAD_DIM
NSA_KV_GROUPS = 4
NSA_HEADS_PER_GROUP = NSA_HEADS // NSA_KV_GROUPS
NSA_Q_WIDTH = NSA_HEADS * NSA_HEAD_DIM
CMP_BLOCK = 32
CMP_STRIDE = 16
CMP_HIDDEN = 512
SLC_BLOCK = 64
SLC_TOPK = 16
SLC_LOCAL = 2
WINDOW = 512
WIN_Q_BLOCK = 128
SLC_Q_BLOCK = 64
ROPE_THETA = 10000.0

N_EXPERTS = 32
N_GROUPS = 8
EXPERTS_PER_GROUP = N_EXPERTS // N_GROUPS
TOP_K = 2
D_EXPERT = D_MODEL // 4

DEEPNORM_ALPHA = (2 * DEPTH) ** 0.25
LN_EPS = 1e-5
RMS_EPS = 1e-6
NEG_INF = -1e30
FORCE_SCORE = 1e6

LANE = 128
VMEM_LIMIT = 48 * 1024 * 1024
MOE_VMEM_LIMIT = 56 * 1024 * 1024
MOE_TILE = 256
ATTN_TILE = 256
ATTN_TK = 256
T_BLOCKS = SEQ // SLC_BLOCK


def _mm_kernel(a_ref, b_ref, o_ref, b16_s):
    @pl.when(pl.program_id(1) == 0)
    def _():
        b16_s[...] = b_ref[...].astype(jnp.bfloat16)

    o_ref[...] = jnp.dot(a_ref[...], b16_s[...], preferred_element_type=jnp.float32)


def pmatmul(a, b, col_start=0, n_cols=None, row_block=0, tm=512):
    m, k = a.shape
    n_cols = b.shape[1] - col_start if n_cols is None else n_cols
    if n_cols % LANE or col_start % LANE:
        b = jnp.pad(b[:, col_start:col_start + n_cols], ((0, 0), (0, -n_cols % LANE)))
        return pmatmul(a, b)[:, :n_cols]
    tn = math.gcd(n_cols, 1024 if k <= 2048 else 512)
    assert col_start % tn == 0 and m % tm == 0 and b.shape[0] % k == 0
    off = col_start // tn
    return pl.pallas_call(
        _mm_kernel,
        grid=(n_cols // tn, m // tm),
        in_specs=[pl.BlockSpec((tm, k), lambda j, i: (i, 0)),
                  pl.BlockSpec((k, tn), lambda j, i: (row_block, j + off))],
        out_specs=pl.BlockSpec((tm, tn), lambda j, i: (i, j)),
        out_shape=jax.ShapeDtypeStruct((m, n_cols), jnp.float32),
        scratch_shapes=[pltpu.VMEM((k, tn), jnp.bfloat16)],
        compiler_params=pltpu.CompilerParams(
            dimension_semantics=("parallel", "arbitrary"), vmem_limit_bytes=VMEM_LIMIT),
        name="dense_matmul",
    )(a.astype(jnp.bfloat16), b.astype(jnp.float32))


def dense(x, w, col_start=0, n_cols=None):
    lead = x.shape[:-1]
    out = pmatmul(x.reshape(-1, x.shape[-1]), w, col_start, n_cols)
    return out.reshape(lead + (out.shape[-1],))


def _moe_kernel(tile_expert_ref, n_tiles_ref, tok_ref, tok_next_ref, dst_prev_ref, x_hbm, gate_ref,
                wg_ref, wu_ref, wd_ref, out_hbm, xbuf, ybuf, wg16, wu16, wd16, sem_in, sem_out):
    i = pl.program_id(0)
    n_tiles = n_tiles_ref[0]
    slot = i % 2
    spare = out_hbm.shape[0] - MOE_TILE

    def gather_row(idx_ref, s, r):
        return pltpu.make_async_copy(x_hbm.at[pl.ds(idx_ref[0, 0, r], 1)], xbuf.at[s, pl.ds(r, 1)],
                                     sem_in.at[s])

    def scatter_row(s, r, dst_row):
        return pltpu.make_async_copy(ybuf.at[s, pl.ds(r, 1)], out_hbm.at[pl.ds(dst_row, 1)], sem_out)

    def wait_gather(s):
        pltpu.make_async_copy(x_hbm.at[pl.ds(0, MOE_TILE)], xbuf.at[s], sem_in.at[s]).wait()

    def wait_scatter():
        pltpu.make_async_copy(ybuf.at[0], out_hbm.at[pl.ds(0, MOE_TILE)], sem_out).wait()

    @pl.when(i == 0)
    def _():
        ybuf[...] = jnp.zeros(ybuf.shape, jnp.float32)

        def first_rows(r, c):
            gather_row(tok_ref, 0, r).start()
            scatter_row(0, r, spare + r).start()
            return c
        lax.fori_loop(0, MOE_TILE, first_rows, 0, unroll=8)

    @pl.when(i < n_tiles)
    def _():
        @pl.when(jnp.logical_or(i == 0, tile_expert_ref[i] != tile_expert_ref[jnp.maximum(i - 1, 0)]))
        def _():
            wg16[...] = wg_ref[0, 0].astype(jnp.bfloat16)
            wu16[...] = wu_ref[0, 0].astype(jnp.bfloat16)
            wd16[...] = wd_ref[0, 0].astype(jnp.bfloat16)

        wait_gather(slot)
        wait_scatter()
        x = xbuf[slot].astype(jnp.bfloat16)
        for r in range(MOE_TILE):
            gather_row(tok_next_ref, 1 - slot, r).start()
            scatter_row(1 - slot, r, dst_prev_ref[0, 0, r]).start()
        g = jnp.dot(x, wg16[...], preferred_element_type=jnp.float32)
        u = jnp.dot(x, wu16[...], preferred_element_type=jnp.float32)
        h = (g * jax.nn.sigmoid(g)) * u * gate_ref[...]
        ybuf[slot] = jnp.dot(h.astype(jnp.bfloat16), wd16[...], preferred_element_type=jnp.float32)

    @pl.when(i == n_tiles)
    def _():
        wait_scatter()

        def last_rows(r, c):
            scatter_row(1 - slot, r, dst_prev_ref[0, 0, r]).start()
            return c
        lax.fori_loop(0, MOE_TILE, last_rows, 0, unroll=8)
        wait_scatter()
        wait_gather(slot)


def moe_ffn(h, router_logits, router_bias, w_gate, w_up, w_down, layer):
    n_tok, D = h.shape
    aff = jax.nn.sigmoid(router_logits.astype(jnp.float32))
    biased = (aff + router_bias.astype(jnp.float32)).reshape(-1, N_GROUPS, EXPERTS_PER_GROUP)

    def top2(v):
        i1 = jnp.argmax(v, axis=-1)
        rest = jnp.where(jnp.arange(v.shape[-1]) == i1[..., None], -jnp.inf, v)
        i2 = jnp.argmax(rest, axis=-1)
        return jnp.max(v, axis=-1), jnp.max(rest, axis=-1), i1, i2

    g1, g2, _, _ = top2(biased)
    best_group = jnp.argmax(g1 + g2, axis=-1)
    cand = jnp.take_along_axis(biased, best_group[:, None, None], axis=1)[:, 0]
    _, _, i1, i2 = top2(cand)
    top_idx = best_group[:, None] * EXPERTS_PER_GROUP + jnp.stack([i1, i2], axis=-1)
    top_aff = jnp.take_along_axis(aff, top_idx, axis=-1)
    top_w = top_aff / jnp.sum(top_aff, axis=-1, keepdims=True)

    n_asg = n_tok * TOP_K
    max_tiles = n_asg // MOE_TILE + N_EXPERTS + 1
    n_rows = max_tiles * MOE_TILE
    e_flat = top_idx.reshape(-1).astype(jnp.int32)
    hot = (e_flat[:, None] == jnp.arange(N_EXPERTS, dtype=jnp.int32)[None, :]).astype(jnp.int32)
    running = jnp.cumsum(hot, axis=0)
    rank = jnp.sum(hot * (running - 1), axis=1)
    counts = running[-1]
    tiles_per = (counts + MOE_TILE - 1) // MOE_TILE
    tile_end = jnp.cumsum(tiles_per)
    row_start = (tile_end - tiles_per) * MOE_TILE
    row_of_asg = jnp.sum(hot * row_start[None, :], axis=1) + rank
    asg_of_row = jnp.full((n_rows,), -1, jnp.int32).at[row_of_asg].set(jnp.arange(n_asg, dtype=jnp.int32))
    token_of_row = jnp.maximum(asg_of_row, 0) // TOP_K
    gate_of_row = jnp.where(asg_of_row >= 0, top_w.reshape(-1)[jnp.maximum(asg_of_row, 0)], 0.0)
    spare_rows = n_asg + jnp.arange(MOE_TILE, dtype=jnp.int32)
    dst3 = jnp.where(asg_of_row >= 0, asg_of_row, jnp.tile(spare_rows, max_tiles)).reshape(max_tiles, 1, MOE_TILE)
    dst_prev3 = jnp.concatenate([spare_rows.reshape(1, 1, MOE_TILE), dst3[:-1]], axis=0)
    n_tiles = tile_end[-1:].astype(jnp.int32)
    tile_expert = jnp.minimum(
        jnp.searchsorted(tile_end, jnp.arange(max_tiles, dtype=jnp.int32), side="right"),
        N_EXPERTS - 1).astype(jnp.int32)
    tile_expert = jnp.where(jnp.arange(max_tiles) < n_tiles[0], tile_expert,
                            tile_expert[jnp.maximum(n_tiles[0] - 1, 0)])

    wspec_in = pl.BlockSpec((1, 1, D, D_EXPERT), lambda i, te, nt: (layer, te[i], 0, 0))
    idx_spec = pl.BlockSpec((1, 1, MOE_TILE), lambda i, te, nt: (i, 0, 0), memory_space=pltpu.SMEM)
    next_spec = pl.BlockSpec((1, 1, MOE_TILE), lambda i, te, nt: (jnp.minimum(i + 1, max_tiles - 1), 0, 0),
                             memory_space=pltpu.SMEM)
    tok3 = token_of_row.reshape(max_tiles, 1, MOE_TILE)
    y = pl.pallas_call(
        _moe_kernel,
        grid_spec=pltpu.PrefetchScalarGridSpec(
            num_scalar_prefetch=2,
            grid=(max_tiles,),
            in_specs=[idx_spec, next_spec, idx_spec,
                      pl.BlockSpec(memory_space=pl.ANY),
                      pl.BlockSpec((MOE_TILE, 1), lambda i, te, nt: (i, 0)),
                      wspec_in, wspec_in,
                      pl.BlockSpec((1, 1, D_EXPERT, D), lambda i, te, nt: (layer, te[i], 0, 0))],
            out_specs=pl.BlockSpec(memory_space=pl.ANY),
            scratch_shapes=[pltpu.VMEM((2, MOE_TILE, D), jnp.float32),
                            pltpu.VMEM((2, MOE_TILE, D), jnp.float32),
                            pltpu.VMEM((D, D_EXPERT), jnp.bfloat16),
                            pltpu.VMEM((D, D_EXPERT), jnp.bfloat16),
                            pltpu.VMEM((D_EXPERT, D), jnp.bfloat16),
                            pltpu.SemaphoreType.DMA((2,)), pltpu.SemaphoreType.DMA(())],
        ),
        out_shape=jax.ShapeDtypeStruct((n_asg + MOE_TILE, D), jnp.float32),
        compiler_params=pltpu.CompilerParams(
            dimension_semantics=("arbitrary",), vmem_limit_bytes=MOE_VMEM_LIMIT),
        name="routed_moe",
    )(tile_expert, n_tiles, tok3, tok3, dst_prev3, h, gate_of_row[:, None], w_gate, w_up, w_down)
    return y.reshape(-1, TOP_K * D)


def _residual_ln_kernel(x_ref, mix_ref, g_ref, b_ref, *rest, n_mix, with_router):
    if with_router:
        rw_ref, o_ref, o16_ref, logit_ref = rest
    else:
        o_ref, o16_ref = rest
    d = x_ref.shape[1]
    h = DEEPNORM_ALPHA * x_ref[...]
    for s in range(n_mix):
        h = h + mix_ref[:, s * d:(s + 1) * d]
    mu = jnp.mean(h, axis=-1, keepdims=True)
    var = jnp.mean(jnp.square(h - mu), axis=-1, keepdims=True)
    y = (h - mu) * lax.rsqrt(var + LN_EPS) * g_ref[...] + b_ref[...]
    o_ref[...] = y
    y16 = y.astype(jnp.bfloat16)
    o16_ref[...] = y16
    if with_router:
        logit_ref[...] = jnp.dot(y16, rw_ref[...].astype(jnp.bfloat16), preferred_element_type=jnp.float32)


def residual_layer_norm(x, mix, g, b, router_w=None):
    n, d = x.shape
    n_mix = mix.shape[1] // d
    tm = 256
    row = lambda i: (i, 0)
    fixed = lambda i: (0, 0)
    in_specs = [pl.BlockSpec((tm, d), row), pl.BlockSpec((tm, n_mix * d), row),
                pl.BlockSpec((1, d), fixed), pl.BlockSpec((1, d), fixed)]
    args = [x, mix, g.astype(jnp.float32).reshape(1, d), b.astype(jnp.float32).reshape(1, d)]
    out_specs = [pl.BlockSpec((tm, d), row), pl.BlockSpec((tm, d), row)]
    out_shape = [jax.ShapeDtypeStruct((n, d), jnp.float32), jax.ShapeDtypeStruct((n, d), jnp.bfloat16)]
    if router_w is not None:
        e_pad = -router_w.shape[1] % LANE
        rw = jnp.pad(router_w.astype(jnp.float32), ((0, 0), (0, e_pad)))
        in_specs.append(pl.BlockSpec(rw.shape, fixed))
        args.append(rw)
        out_specs.append(pl.BlockSpec((tm, rw.shape[1]), row))
        out_shape.append(jax.ShapeDtypeStruct((n, rw.shape[1]), jnp.float32))
    outs = pl.pallas_call(
        functools.partial(_residual_ln_kernel, n_mix=n_mix, with_router=router_w is not None),
        grid=(n // tm,),
        in_specs=in_specs, out_specs=out_specs, out_shape=out_shape,
        compiler_params=pltpu.CompilerParams(
            dimension_semantics=("parallel",), vmem_limit_bytes=VMEM_LIMIT),
        name="residual_layer_norm",
    )(*args)
    if router_w is not None:
        return outs[0], outs[1], outs[2][:, :router_w.shape[1]]
    return outs[0], outs[1]


def _softplus(x):
    return jnp.maximum(x, 0.0) + jnp.log1p(jnp.exp(-jnp.abs(x)))


def _bdot(a, b):
    return jnp.dot(a.astype(jnp.bfloat16), b.astype(jnp.bfloat16), preferred_element_type=jnp.float32)


def _bdot_nt(a, b):
    return lax.dot_general(a.astype(jnp.bfloat16), b.astype(jnp.bfloat16), (((1,), (1,)), ((), ())),
                           preferred_element_type=jnp.float32)


def _bdot_tn(a, b):
    return lax.dot_general(a.astype(jnp.bfloat16), b.astype(jnp.bfloat16), (((0,), (0,)), ((), ())),
                           preferred_element_type=jnp.float32)


def _gdn_prep_kernel(x_ref, halo_ref, w_ref, o_ref):
    i = pl.program_id(1)
    j = pl.program_id(2)
    tt = x_ref.shape[1]
    x = x_ref[0]
    halo = jnp.where(i == 0, 0.0, halo_ref[0])
    xx = jnp.concatenate([halo, x], axis=0)
    w = w_ref[...]
    y = w[3:4] * x
    for tap in range(GDN_CONV - 1):
        lo = 8 - (GDN_CONV - 1) + tap
        y = y + w[tap:tap + 1] * xx[lo:lo + tt]
    y = y * jax.nn.sigmoid(y)
    q_blocks = GDN_QK_WIDTH // x_ref.shape[2]

    @pl.when(j >= 2 * q_blocks)
    def _():
        o_ref[0] = y

    @pl.when(j < 2 * q_blocks)
    def _():
        scale = jnp.where(j < q_blocks, GDN_HEAD_DIM ** -0.5, 1.0)
        for h in range(x_ref.shape[2] // GDN_HEAD_DIM):
            yh = y[:, h * GDN_HEAD_DIM:(h + 1) * GDN_HEAD_DIM]
            inv = lax.rsqrt(jnp.sum(yh * yh, axis=-1, keepdims=True) + RMS_EPS)
            o_ref[0, :, h * GDN_HEAD_DIM:(h + 1) * GDN_HEAD_DIM] = yh * inv * scale


def gdn_prep(qkv, conv_w):
    B, T, CH = qkv.shape
    tt, tc = GDN_TILE, 512
    return pl.pallas_call(
        _gdn_prep_kernel,
        grid=(B, T // tt, CH // tc),
        in_specs=[pl.BlockSpec((1, tt, tc), lambda b, i, j: (b, i, j)),
                  pl.BlockSpec((1, 8, tc), lambda b, i, j: (b, jnp.maximum(i * (tt // 8) - 1, 0), j)),
                  pl.BlockSpec((GDN_CONV, tc), lambda b, i, j: (0, j))],
        out_specs=pl.BlockSpec((1, tt, tc), lambda b, i, j: (b, i, j)),
        out_shape=jax.ShapeDtypeStruct((B, T, CH), jnp.float32),
        compiler_params=pltpu.CompilerParams(
            dimension_semantics=("parallel", "parallel", "parallel"), vmem_limit_bytes=VMEM_LIMIT),
        name="gdn_conv_silu_norm",
    )(qkv, qkv, conv_w)


def _gdn_kernel(q_ref, k_ref, v_ref, z_ref, acol_ref, arow_ref, bcol_ref, alog_ref, dt_ref, nw_ref,
                o_ref, state_s):
    C, Dh = GDN_CHUNK, GDN_HEAD_DIM

    @pl.when(pl.program_id(2) == 0)
    def _():
        state_s[...] = jnp.zeros(state_s.shape, jnp.float32)

    row = lax.broadcasted_iota(jnp.int32, (C, C), 0)
    col = lax.broadcasted_iota(jnp.int32, (C, C), 1)
    causal = row >= col
    strict = row > col
    nw = nw_ref[...]
    probs = []
    for n, kh in [(n, kh) for n in range(GDN_TILE // C) for kh in range(GDN_HK_STEP)]:
        sl = slice(n * C, (n + 1) * C)
        q = q_ref[0, sl, kh * Dh:(kh + 1) * Dh]
        k = k_ref[0, sl, kh * Dh:(kh + 1) * Dh]
        kk = _bdot_nt(k, k)
        qk = _bdot_nt(q, k)
        for hh in range(2):
            vh = 2 * kh + hh
            neg_a = -jnp.exp(alog_ref[kh, :, hh:hh + 1])
            dt = dt_ref[kh, :, hh:hh + 1]
            g_col = neg_a * _softplus(acol_ref[0, kh, sl, hh:hh + 1] + dt)
            g_row = neg_a * _softplus(arow_ref[0, kh, hh:hh + 1, sl] + dt)
            beta = jax.nn.sigmoid(bcol_ref[0, kh, sl, hh:hh + 1])
            gc_col = jnp.sum(jnp.where(causal, g_row, 0.0), axis=1, keepdims=True)
            gc_row = jnp.sum(jnp.where(row <= col, g_col, 0.0), axis=0, keepdims=True)
            g_last = jnp.sum(g_row, axis=1, keepdims=True)
            decay = jnp.where(causal, jnp.exp(jnp.where(causal, gc_col - gc_row, 0.0)), 0.0)
            m = jnp.where(strict, beta * kk * decay, 0.0)
            e_col = jnp.exp(gc_col)
            v = v_ref[0, sl, vh * Dh:(vh + 1) * Dh]
            probs.append(dict(
                sl=sl, hh=vh, x=-m, p=m, a=qk * decay,
                rhs=jnp.concatenate([beta * v, beta * e_col * k], axis=1),
                qe=q * e_col, ke=k * jnp.exp(g_last - gc_col), s_decay=jnp.exp(g_last)))
    for _ in range(int(math.log2(C)) - 1):
        for pr in probs:
            pr["p"] = _bdot(pr["p"], pr["p"])
        for pr in probs:
            pr["x"] = pr["x"] + pr["p"] + _bdot(pr["x"], pr["p"])
    for pr in probs:
        pr["sol"] = pr["rhs"] + _bdot(pr["x"], pr["rhs"])
    for pr in probs:
        sl, hh = pr["sl"], pr["hh"]
        u, w = pr["sol"][:, :Dh], pr["sol"][:, Dh:]
        state = state_s[hh]
        ws = _bdot(jnp.concatenate([w, pr["qe"]], axis=0), state)
        v_new = u - ws[:C]
        o = ws[C:] + _bdot(pr["a"], v_new)
        state_s[hh] = state * pr["s_decay"] + _bdot_tn(pr["ke"], v_new)
        z = z_ref[0, sl, hh * Dh:(hh + 1) * Dh]
        o = o * lax.rsqrt(jnp.mean(o * o, axis=-1, keepdims=True) + RMS_EPS) * nw
        o_ref[0, sl, hh * Dh:(hh + 1) * Dh] = (o * (z * jax.nn.sigmoid(z))).astype(o_ref.dtype)


def gdn_core(qkv, z, a_raw, b_raw, a_log, dt_bias, norm_w):
    B, T, _ = qkv.shape
    Hk, Dh, tt = GDN_QK_HEADS, GDN_HEAD_DIM, GDN_TILE
    a4 = a_raw.reshape(B, T, Hk, 2)
    a_col = a4.transpose(0, 2, 1, 3)
    a_row = a4.transpose(0, 2, 3, 1)
    b_col = b_raw.reshape(B, T, Hk, 2).transpose(0, 2, 1, 3)
    hs = GDN_HK_STEP
    qw, vw = hs * Dh, 2 * hs * Dh
    k_off, v_off = GDN_QK_WIDTH // qw, 2 * GDN_QK_WIDTH // vw
    col_spec = pl.BlockSpec((1, hs, tt, 2), lambda b, h, c: (b, h, c, 0))
    head_spec = pl.BlockSpec((hs, 1, 2), lambda b, h, c: (h, 0, 0))
    return pl.pallas_call(
        _gdn_kernel,
        grid=(B, Hk // hs, T // tt),
        in_specs=[pl.BlockSpec((1, tt, qw), lambda b, h, c: (b, c, h)),
                  pl.BlockSpec((1, tt, qw), lambda b, h, c: (b, c, k_off + h)),
                  pl.BlockSpec((1, tt, vw), lambda b, h, c: (b, c, v_off + h)),
                  pl.BlockSpec((1, tt, vw), lambda b, h, c: (b, c, h)),
                  col_spec,
                  pl.BlockSpec((1, hs, 2, tt), lambda b, h, c: (b, h, 0, c)),
                  col_spec, head_spec, head_spec,
                  pl.BlockSpec((1, Dh), lambda b, h, c: (0, 0))],
        out_specs=pl.BlockSpec((1, tt, vw), lambda b, h, c: (b, c, h)),
        out_shape=jax.ShapeDtypeStruct((B, T, GDN_V_WIDTH), jnp.bfloat16),
        scratch_shapes=[pltpu.VMEM((2 * hs, Dh, Dh), jnp.float32)],
        compiler_params=pltpu.CompilerParams(
            dimension_semantics=("parallel", "parallel", "arbitrary"), vmem_limit_bytes=VMEM_LIMIT),
        name="gated_delta_rule",
    )(qkv, qkv, qkv, z, a_col, a_row, b_col,
      a_log.astype(jnp.float32).reshape(Hk, 1, 2), dt_bias.astype(jnp.float32).reshape(Hk, 1, 2),
      norm_w.astype(jnp.float32).reshape(1, Dh))


def gdn_mixer(x, w_in, conv_w, a_log, dt_bias, norm_w, w_out):
    B, T, _ = x.shape
    Hk, Hv, Dh = GDN_QK_HEADS, GDN_V_HEADS, GDN_HEAD_DIM
    qkv = dense(x, w_in, 0, GDN_CONV_CH)
    z = dense(x, w_in, GDN_CONV_CH, GDN_V_WIDTH)
    a_raw, b_raw = jnp.split(dense(x, w_in, GDN_CONV_CH + GDN_V_WIDTH), [Hv], axis=-1)
    o = gdn_core(gdn_prep(qkv, conv_w), z, a_raw, b_raw, a_log, dt_bias, norm_w)
    return dense(o, w_out)


def _rope_cast_kernel(x_ref, cos_ref, sin_ref, o_ref, *, rotate, scale):
    D = NSA_HEAD_DIM
    for h in range(x_ref.shape[2] // D):
        x = x_ref[0, :, h * D:(h + 1) * D]
        if rotate:
            x = x * cos_ref[...] + pltpu.roll(x, D // 2, axis=1) * sin_ref[...]
        if scale != 1.0:
            x = x * scale
        o_ref[0, :, h * D:(h + 1) * D] = x.astype(o_ref.dtype)


def rope_cast(x, col_start, n_cols, rotate, scale=1.0):
    B, T, _ = x.shape
    tt, tc, D = 512, 512, NSA_HEAD_DIM
    half = D // 2
    inv_freq = ROPE_THETA ** (-jnp.arange(half, dtype=jnp.float32) / half)
    ang = jnp.arange(T, dtype=jnp.float32)[:, None] * inv_freq[None, :]
    cos2 = jnp.concatenate([jnp.cos(ang), jnp.cos(ang)], axis=1)
    sin2 = jnp.concatenate([-jnp.sin(ang), jnp.sin(ang)], axis=1)
    off = col_start // tc
    return pl.pallas_call(
        functools.partial(_rope_cast_kernel, rotate=rotate, scale=scale),
        grid=(B, T // tt, n_cols // tc),
        in_specs=[pl.BlockSpec((1, tt, tc), lambda b, i, j: (b, i, off + j)),
                  pl.BlockSpec((tt, D), lambda b, i, j: (i, 0)),
                  pl.BlockSpec((tt, D), lambda b, i, j: (i, 0))],
        out_specs=pl.BlockSpec((1, tt, tc), lambda b, i, j: (b, i, j)),
        out_shape=jax.ShapeDtypeStruct((B, T, n_cols), jnp.bfloat16),
        compiler_params=pltpu.CompilerParams(
            dimension_semantics=("parallel", "parallel", "parallel"), vmem_limit_bytes=VMEM_LIMIT),
        name="rope_cast",
    )(x, cos2, sin2)


def _compress_kernel(x_ref, pe_ref, w1_ref, w2_ref, o_ref):
    S, D = CMP_STRIDE, NSA_HEAD_DIM
    n_half = x_ref.shape[1] // S
    n_cmp = (x_ref.shape[1] - CMP_BLOCK) // S + 1
    top = jnp.zeros((n_half, CMP_HIDDEN), jnp.float32)
    bot = jnp.zeros((n_half, CMP_HIDDEN), jnp.float32)
    for s in range(S):
        xs = x_ref[0, pl.ds(s, n_half, stride=S), :]
        top = top + _bdot(xs + pe_ref[0, s:s + 1, :], w1_ref[0, s * D:(s + 1) * D, :])
        bot = bot + _bdot(xs + pe_ref[0, S + s:S + s + 1, :], w1_ref[0, (S + s) * D:(S + s + 1) * D, :])
    nxt = jnp.concatenate([bot[1:], jnp.zeros((1, CMP_HIDDEN), jnp.float32)], axis=0)
    hid = top + nxt
    hid = hid * jax.nn.sigmoid(hid)
    out = _bdot(hid, w2_ref[0])
    live = lax.broadcasted_iota(jnp.int32, out.shape, 0) < n_cmp
    o_ref[0, 0] = jnp.where(live, out, 0.0).astype(o_ref.dtype)


def compress_blocks(kv, part, pe, w1, w2):
    B, T, _ = kv.shape
    G, D = NSA_KV_GROUPS, NSA_HEAD_DIM
    return pl.pallas_call(
        _compress_kernel,
        grid=(B, G),
        in_specs=[pl.BlockSpec((1, T, D), lambda b, g: (b, 0, part * G + g)),
                  pl.BlockSpec((1, CMP_BLOCK, D), lambda b, g: (part, 0, 0)),
                  pl.BlockSpec((1, CMP_BLOCK * D, CMP_HIDDEN), lambda b, g: (part, 0, 0)),
                  pl.BlockSpec((1, CMP_HIDDEN, D), lambda b, g: (part, 0, 0))],
        out_specs=pl.BlockSpec((1, 1, T // CMP_STRIDE, D), lambda b, g: (b, g, 0, 0)),
        out_shape=jax.ShapeDtypeStruct((B, G, T // CMP_STRIDE, D), jnp.bfloat16),
        compiler_params=pltpu.CompilerParams(
            dimension_semantics=("parallel", "parallel"), vmem_limit_bytes=VMEM_LIMIT),
        name="compress_blocks",
    )(kv, pe.astype(jnp.float32), w1.astype(jnp.float32), w2.astype(jnp.float32))


def nsa_shared_kv(h, kv_w, cmp_pe, cmp_w1, cmp_w2):
    W = NSA_KV_GROUPS * NSA_HEAD_DIM
    kv = dense(h, kv_w)
    k_cmp = compress_blocks(kv, 0, cmp_pe, cmp_w1, cmp_w2)
    v_cmp = compress_blocks(kv, 1, cmp_pe, cmp_w1, cmp_w2)
    k_slc = rope_cast(kv, 2 * W, W, True)
    v_slc = rope_cast(kv, 3 * W, W, False)
    k_win = rope_cast(kv, 4 * W, W, True)
    v_win = rope_cast(kv, 5 * W, W, False)
    return (k_cmp, v_cmp, k_slc, v_slc, k_win, v_win)


def _branch_gate(gate_ref, branch, r):
    row = branch * NSA_HEADS + pl.program_id(1) * NSA_HEADS_PER_GROUP + r
    return jax.nn.sigmoid(gate_ref[0, pl.ds(row, 1), :])


def _cmp_select_kernel(q_ref, kc_ref, vc_ref, ov_ref, gate_ref, o_ref, sel_ref):
    i = pl.program_id(2)
    tq, D, n_c, n_s = ATTN_TILE, NSA_HEAD_DIM, kc_ref.shape[2], T_BLOCKS
    tpos = i * tq + lax.broadcasted_iota(jnp.int32, (n_c, tq), 1)
    block_end = lax.broadcasted_iota(jnp.int32, (n_c, tq), 0) * CMP_STRIDE + (CMP_BLOCK - 1)
    visible = block_end <= tpos
    kc = kc_ref[0, 0]
    vc = vc_ref[0, 0]
    p_sum = jnp.zeros((n_c, tq), jnp.float32)
    for r in range(NSA_HEADS_PER_GROUP):
        s = lax.dot_general(kc, q_ref[0, :, r * D:(r + 1) * D], (((1,), (1,)), ((), ())),
                            preferred_element_type=jnp.float32)
        s = jnp.where(visible, s, NEG_INF)
        e = jnp.exp2(s - jnp.max(s, axis=0, keepdims=True))
        p = jnp.where(visible, e / jnp.sum(e, axis=0, keepdims=True), 0.0)
        o_t = lax.dot_general(vc, p.astype(jnp.bfloat16), (((0,), (0,)), ((), ())),
                              preferred_element_type=jnp.float32)
        o_ref[0, :, r * D:(r + 1) * D] = (o_t * _branch_gate(gate_ref, 0, r)).T
        p_sum = p_sum + p
    p_slc = jnp.dot(ov_ref[...], p_sum.astype(jnp.bfloat16), preferred_element_type=jnp.float32)
    blk = lax.broadcasted_iota(jnp.int32, (n_s, tq), 0)
    cur = (i * tq + lax.broadcasted_iota(jnp.int32, (n_s, tq), 1)) // SLC_BLOCK
    causal_blk = blk <= cur
    forced = (blk == 0) | (causal_blk & (blk > cur - SLC_LOCAL))
    score = jnp.where(causal_blk, jnp.where(forced, FORCE_SCORE, p_slc), -1.0)
    rank = jnp.zeros((n_s, tq), jnp.float32)
    for other in range(n_s):
        row = score[other:other + 1, :]
        ahead = (row > score) | ((row == score) & (blk > other))
        rank = rank + jnp.where(ahead, 1.0, 0.0)
    picked = (rank < float(min(SLC_TOPK, n_s))) & (score >= 0.0)
    sel_ref[0, 0] = jnp.where(picked, 1.0, 0.0).astype(sel_ref.dtype)


def nsa_compressed_select(q_cmp, k_cmp, v_cmp, gate_logits_t):
    B, T, _ = q_cmp.shape
    G, R, D, tq = NSA_KV_GROUPS, NSA_HEADS_PER_GROUP, NSA_HEAD_DIM, ATTN_TILE
    n_c = k_cmp.shape[2]
    n_s = T // SLC_BLOCK
    c0 = np.arange(n_c) * CMP_STRIDE
    s0 = np.arange(n_s) * SLC_BLOCK
    ov = np.clip(np.minimum(c0[None, :] + CMP_BLOCK, s0[:, None] + SLC_BLOCK)
                 - np.maximum(c0[None, :], s0[:, None]), 0, None) / CMP_BLOCK
    cmp_spec = pl.BlockSpec((1, 1, n_c, D), lambda b, g, i: (b, g, 0, 0))
    return pl.pallas_call(
        _cmp_select_kernel,
        grid=(B, G, T // tq),
        in_specs=[pl.BlockSpec((1, tq, R * D), lambda b, g, i: (b, i, g)), cmp_spec, cmp_spec,
                  pl.BlockSpec((n_s, n_c), lambda b, g, i: (0, 0)),
                  pl.BlockSpec((1, 3 * NSA_HEADS, tq), lambda b, g, i: (b, 0, i))],
        out_specs=[pl.BlockSpec((1, tq, R * D), lambda b, g, i: (b, i, g)),
                   pl.BlockSpec((1, 1, n_s, tq), lambda b, g, i: (b, g, 0, i))],
        out_shape=[jax.ShapeDtypeStruct((B, T, G * R * D), jnp.float32),
                   jax.ShapeDtypeStruct((B, G, n_s, T), jnp.bfloat16)],
        compiler_params=pltpu.CompilerParams(
            dimension_semantics=("parallel", "parallel", "parallel"), vmem_limit_bytes=VMEM_LIMIT),
        name="compressed_attention_select",
    )(q_cmp, k_cmp, v_cmp, jnp.asarray(ov, jnp.bfloat16), gate_logits_t)


def _attn_kernel(*refs, windowed):
    if windowed:
        q_ref, k_ref, v_ref, gate_ref, prev_ref, o_ref, m_s, l_s, acc_s = refs
    else:
        q_ref, k_ref, v_ref, sel_ref, gate_ref, prev_ref, o_ref, m_s, l_s, acc_s = refs
    i = pl.program_id(2)
    tq, tk, D = ATTN_TILE, ATTN_TK, NSA_HEAD_DIM
    j_hi = (i + 1) * (tq // tk) - 1
    m_s[...] = jnp.full(m_s.shape, NEG_INF, jnp.float32)
    l_s[...] = jnp.zeros(l_s.shape, jnp.float32)
    acc_s[...] = jnp.zeros(acc_s.shape, jnp.float32)
    tpos = i * tq + lax.broadcasted_iota(jnp.int32, (tk, tq), 1)

    def body(jj, carry):
        j = j_hi - jj
        start = pl.multiple_of(j * tk, tk)
        kblk = k_ref[0, pl.ds(start, tk), :]
        vblk = v_ref[0, pl.ds(start, tk), :]
        kpos = j * tk + lax.broadcasted_iota(jnp.int32, (tk, tq), 0)
        ok = kpos <= tpos
        if windowed:
            ok = ok & (kpos > tpos - WINDOW)
        else:
            blk_of_key = j * (tk // SLC_BLOCK) + lax.broadcasted_iota(
                jnp.int32, (tk, T_BLOCKS), 0) // SLC_BLOCK
            expand = (lax.broadcasted_iota(jnp.int32, (tk, T_BLOCKS), 1) == blk_of_key)
            selm = jnp.dot(expand.astype(jnp.bfloat16), sel_ref[0, 0],
                           preferred_element_type=jnp.float32)
            ok = ok & (selm > 0.5)
        def scores(r):
            return lax.dot_general(kblk, q_ref[0, :, r * D:(r + 1) * D], (((1,), (1,)), ((), ())),
                                   preferred_element_type=jnp.float32)

        s_next = scores(0)
        for r in range(NSA_HEADS_PER_GROUP):
            s = jnp.where(ok, s_next, NEG_INF)
            if r + 1 < NSA_HEADS_PER_GROUP:
                s_next = scores(r + 1)
            m_prev = m_s[r]
            m_new = jnp.maximum(m_prev, jnp.max(s, axis=0, keepdims=True))
            alpha = jnp.exp2(m_prev - m_new)
            p = jnp.exp2(s - m_new)
            l_s[r] = alpha * l_s[r] + jnp.sum(p, axis=0, keepdims=True)
            acc_s[r] = alpha * acc_s[r] + lax.dot_general(
                vblk, p.astype(jnp.bfloat16), (((0,), (0,)), ((), ())),
                preferred_element_type=jnp.float32)
            m_s[r] = m_new
        return carry

    j_lo = jnp.maximum(i * tq - WINDOW + 1, 0) // tk if windowed else 0
    lax.fori_loop(0, j_hi - j_lo + 1, body, 0)
    branch = 2 if windowed else 1
    for r in range(NSA_HEADS_PER_GROUP):
        gated = (acc_s[r] / l_s[r] * _branch_gate(gate_ref, branch, r)).T
        o_ref[0, :, r * D:(r + 1) * D] = (prev_ref[0, :, r * D:(r + 1) * D] + gated).astype(o_ref.dtype)


def masked_attention(qs, kb, vb, sel, gate_logits_t, prev, out_dtype):
    B, T, _ = qs.shape
    G, R, D = NSA_KV_GROUPS, NSA_HEADS_PER_GROUP, NSA_HEAD_DIM
    tq = ATTN_TILE
    windowed = sel is None
    q_spec = pl.BlockSpec((1, tq, R * D), lambda b, g, i: (b, i, g))
    in_specs = [q_spec,
                pl.BlockSpec((1, T, D), lambda b, g, i: (b, 0, g)),
                pl.BlockSpec((1, T, D), lambda b, g, i: (b, 0, g))]
    args = [qs, kb, vb]
    if not windowed:
        in_specs.append(pl.BlockSpec((1, 1, T // SLC_BLOCK, tq), lambda b, g, i: (b, g, 0, i)))
        args.append(sel)
    in_specs += [pl.BlockSpec((1, 3 * NSA_HEADS, tq), lambda b, g, i: (b, 0, i)), q_spec]
    args += [gate_logits_t, prev]
    return pl.pallas_call(
        functools.partial(_attn_kernel, windowed=windowed),
        grid=(B, G, T // tq),
        in_specs=in_specs,
        out_specs=pl.BlockSpec((1, tq, R * D), lambda b, g, i: (b, i, g)),
        out_shape=jax.ShapeDtypeStruct((B, T, G * R * D), out_dtype),
        scratch_shapes=[pltpu.VMEM((R, 1, tq), jnp.float32),
                        pltpu.VMEM((R, 1, tq), jnp.float32),
                        pltpu.VMEM((R, D, tq), jnp.float32)],
        compiler_params=pltpu.CompilerParams(
            dimension_semantics=("parallel", "parallel", "arbitrary"), vmem_limit_bytes=VMEM_LIMIT),
        name="window_attention" if windowed else "selected_attention",
    )(*args)


def nsa_mixer(x, w_q, w_out, k_cmp, v_cmp, k_slc, v_slc, k_win, v_win):
    B, T, _ = x.shape
    H, D = NSA_HEADS, NSA_HEAD_DIM
    q = dense(x, w_q, 0, NSA_Q_WIDTH)
    gate_t = jnp.swapaxes(dense(x, w_q, NSA_Q_WIDTH), 1, 2)
    q_scale = D ** -0.5 * math.log2(math.e)
    q_cmp = rope_cast(q, 0, NSA_Q_WIDTH, False, q_scale)
    q_rot = rope_cast(q, 0, NSA_Q_WIDTH, True, q_scale)
    o, sel = nsa_compressed_select(q_cmp, k_cmp, v_cmp, gate_t)
    o = masked_attention(q_rot, k_slc, v_slc, sel, gate_t, o, jnp.float32)
    o = masked_attention(q_rot, k_win, v_win, None, gate_t, o, jnp.bfloat16)
    return dense(o, w_out)


def kernel(x, a_w_in, a_conv_w, a_a_log, a_dt_bias, a_norm_w, a_w_out, kv_w, cmp_pe, cmp_w1, cmp_w2,
           b_w_q, b_w_out, router_w, router_bias, moe_w_gate, moe_w_up, moe_w_down, ln_g, ln_b):
    B, T, D = x.shape
    xf = x.astype(jnp.float32).reshape(B * T, D)
    x16 = xf.astype(jnp.bfloat16)
    shared_kv = None
    for layer in range(DEPTH):
        xin = x16.reshape(B, T, D)
        if layer < N_A_LAYERS:
            mix = gdn_mixer(xin, a_w_in[layer], a_conv_w[layer], a_a_log[layer], a_dt_bias[layer],
                            a_norm_w[layer], a_w_out[layer])
        else:
            if shared_kv is None:
                shared_kv = nsa_shared_kv(xin, kv_w, cmp_pe, cmp_w1, cmp_w2)
            j = layer - N_A_LAYERS
            mix = nsa_mixer(xin, b_w_q[j], b_w_out[j], *shared_kv)
        xf, x16, logits = residual_layer_norm(xf, mix.reshape(B * T, D), ln_g[layer, 0], ln_b[layer, 0],
                                              router_w)
        ffn = moe_ffn(xf, logits, router_bias, moe_w_gate, moe_w_up, moe_w_down, layer)
        xf, x16 = residual_layer_norm(xf, ffn, ln_g[layer, 1], ln_b[layer, 1])
    return xf.reshape(B, T, D).astype(x.dtype)
```

```python
import functools
import math

import jax
import jax.numpy as jnp
import numpy as np
from jax import lax
from jax.experimental import pallas as pl
from jax.experimental.pallas import tpu as pltpu

D_MODEL = 2048
BATCH = 2
SEQ = 4096
DEPTH = 2
N_A_LAYERS = DEPTH // 2

GDN_HEAD_DIM = 128
GDN_QK_HEADS = D_MODEL // GDN_HEAD_DIM
GDN_V_HEADS = 2 * GDN_QK_HEADS
GDN_QK_WIDTH = GDN_QK_HEADS * GDN_HEAD_DIM
GDN_V_WIDTH = GDN_V_HEADS * GDN_HEAD_DIM
GDN_CONV_CH = 2 * GDN_QK_WIDTH + GDN_V_WIDTH
GDN_CHUNK = 64
GDN_CONV = 4
GDN_TILE = 256
GDN_HK_STEP = 4

NSA_HEAD_DIM = 128
NSA_HEADS = D_MODEL // NSA_HEAD_DIM
NSA_KV_GROUPS = 4
NSA_HEADS_PER_GROUP = NSA_HEADS // NSA_KV_GROUPS
NSA_Q_WIDTH = NSA_HEADS * NSA_HEAD_DIM
CMP_BLOCK = 32
CMP_STRIDE = 16
CMP_HIDDEN = 512
SLC_BLOCK = 64
SLC_TOPK = 16
SLC_LOCAL = 2
WINDOW = 512
WIN_Q_BLOCK = 128
SLC_Q_BLOCK = 64
ROPE_THETA = 10000.0

N_EXPERTS = 32
N_GROUPS = 8
EXPERTS_PER_GROUP = N_EXPERTS // N_GROUPS
TOP_K = 2
D_EXPERT = D_MODEL // 4

DEEPNORM_ALPHA = (2 * DEPTH) ** 0.25
LN_EPS = 1e-5
RMS_EPS = 1e-6
NEG_INF = -1e30
FORCE_SCORE = 1e6

LANE = 128
VMEM_LIMIT = 48 * 1024 * 1024
MOE_VMEM_LIMIT = 56 * 1024 * 1024
MOE_TILE = 256
ATTN_TILE = 256
ATTN_TK = 256
T_BLOCKS = SEQ // SLC_BLOCK


def _mm_kernel(a_ref, b_ref, o_ref, b16_s):
    @pl.when(pl.program_id(1) == 0)
    def _():
        b16_s[...] = b_ref[...].astype(jnp.bfloat16)

    o_ref[...] = jnp.dot(a_ref[...], b16_s[...], preferred_element_type=jnp.float32)


def pmatmul(a, b, col_start=0, n_cols=None, row_block=0, tm=512):
    m, k = a.shape
    n_cols = b.shape[1] - col_start if n_cols is None else n_cols
    if n_cols % LANE or col_start % LANE:
        b = jnp.pad(b[:, col_start:col_start + n_cols], ((0, 0), (0, -n_cols % LANE)))
        return pmatmul(a, b)[:, :n_cols]
    tn = math.gcd(n_cols, 1024 if k <= 2048 else 512)
    assert col_start % tn == 0 and m % tm == 0 and b.shape[0] % k == 0
    off = col_start // tn
    return pl.pallas_call(
        _mm_kernel,
        grid=(n_cols // tn, m // tm),
        in_specs=[pl.BlockSpec((tm, k), lambda j, i: (i, 0)),
                  pl.BlockSpec((k, tn), lambda j, i: (row_block, j + off))],
        out_specs=pl.BlockSpec((tm, tn), lambda j, i: (i, j)),
        out_shape=jax.ShapeDtypeStruct((m, n_cols), jnp.float32),
        scratch_shapes=[pltpu.VMEM((k, tn), jnp.bfloat16)],
        compiler_params=pltpu.CompilerParams(
            dimension_semantics=("parallel", "arbitrary"), vmem_limit_bytes=VMEM_LIMIT),
        name="dense_matmul",
    )(a.astype(jnp.bfloat16), b.astype(jnp.float32))


def dense(x, w, col_start=0, n_cols=None):
    lead = x.shape[:-1]
    out = pmatmul(x.reshape(-1, x.shape[-1]), w, col_start, n_cols)
    return out.reshape(lead + (out.shape[-1],))


def _moe_kernel(tile_expert_ref, n_tiles_ref, tok_ref, tok_next_ref, dst_prev_ref, x_hbm, gate_ref,
                wg_ref, wu_ref, wd_ref, out_hbm, xbuf, ybuf, wg16, wu16, wd16, sem_in, sem_out):
    i = pl.program_id(0)
    n_tiles = n_tiles_ref[0]
    slot = i % 2
    spare = out_hbm.shape[0] - MOE_TILE

    def gather_row(idx_ref, s, r):
        return pltpu.make_async_copy(x_hbm.at[pl.ds(idx_ref[0, 0, r], 1)], xbuf.at[s, pl.ds(r, 1)],
                                     sem_in.at[s])

    def scatter_row(s, r, dst_row):
        return pltpu.make_async_copy(ybuf.at[s, pl.ds(r, 1)], out_hbm.at[pl.ds(dst_row, 1)], sem_out)

    def wait_gather(s):
        pltpu.make_async_copy(x_hbm.at[pl.ds(0, MOE_TILE)], xbuf.at[s], sem_in.at[s]).wait()

    def wait_scatter():
        pltpu.make_async_copy(ybuf.at[0], out_hbm.at[pl.ds(0, MOE_TILE)], sem_out).wait()

    @pl.when(i == 0)
    def _():
        ybuf[...] = jnp.zeros(ybuf.shape, jnp.float32)

        def first_rows(r, c):
            gather_row(tok_ref, 0, r).start()
            scatter_row(0, r, spare + r).start()
            return c
        lax.fori_loop(0, MOE_TILE, first_rows, 0, unroll=8)

    @pl.when(i < n_tiles)
    def _():
        @pl.when(jnp.logical_or(i == 0, tile_expert_ref[i] != tile_expert_ref[jnp.maximum(i - 1, 0)]))
        def _():
            wg16[...] = wg_ref[0, 0].astype(jnp.bfloat16)
            wu16[...] = wu_ref[0, 0].astype(jnp.bfloat16)
            wd16[...] = wd_ref[0, 0].astype(jnp.bfloat16)

        wait_gather(slot)
        wait_scatter()
        x = xbuf[slot].astype(jnp.bfloat16)
        for r in range(MOE_TILE):
            gather_row(tok_next_ref, 1 - slot, r).start()
            scatter_row(1 - slot, r, dst_prev_ref[0, 0, r]).start()
        g = jnp.dot(x, wg16[...], preferred_element_type=jnp.float32)
        u = jnp.dot(x, wu16[...], preferred_element_type=jnp.float32)
        h = (g * jax.nn.sigmoid(g)) * u * gate_ref[...]
        ybuf[slot] = jnp.dot(h.astype(jnp.bfloat16), wd16[...], preferred_element_type=jnp.float32)

    @pl.when(i == n_tiles)
    def _():
        wait_scatter()

        def last_rows(r, c):
            scatter_row(1 - slot, r, dst_prev_ref[0, 0, r]).start()
            return c
        lax.fori_loop(0, MOE_TILE, last_rows, 0, unroll=8)
        wait_scatter()
        wait_gather(slot)


def moe_ffn(h, router_logits, router_bias, w_gate, w_up, w_down, layer):
    n_tok, D = h.shape
    aff = jax.nn.sigmoid(router_logits.astype(jnp.float32))
    biased = (aff + router_bias.astype(jnp.float32)).reshape(-1, N_GROUPS, EXPERTS_PER_GROUP)

    def top2(v):
        i1 = jnp.argmax(v, axis=-1)
        rest = jnp.where(jnp.arange(v.shape[-1]) == i1[..., None], -jnp.inf, v)
        i2 = jnp.argmax(rest, axis=-1)
        return jnp.max(v, axis=-1), jnp.max(rest, axis=-1), i1, i2

    g1, g2, _, _ = top2(biased)
    best_group = jnp.argmax(g1 + g2, axis=-1)
    cand = jnp.take_along_axis(biased, best_group[:, None, None], axis=1)[:, 0]
    _, _, i1, i2 = top2(cand)
    top_idx = best_group[:, None] * EXPERTS_PER_GROUP + jnp.stack([i1, i2], axis=-1)
    top_aff = jnp.take_along_axis(aff, top_idx, axis=-1)
    top_w = top_aff / jnp.sum(top_aff, axis=-1, keepdims=True)

    n_asg = n_tok * TOP_K
    max_tiles = n_asg // MOE_TILE + N_EXPERTS + 1
    n_rows = max_tiles * MOE_TILE
    e_flat = top_idx.reshape(-1).astype(jnp.int32)
    hot = (e_flat[:, None] == jnp.arange(N_EXPERTS, dtype=jnp.int32)[None, :]).astype(jnp.int32)
    running = jnp.cumsum(hot, axis=0)
    rank = jnp.sum(hot * (running - 1), axis=1)
    counts = running[-1]
    tiles_per = (counts + MOE_TILE - 1) // MOE_TILE
    tile_end = jnp.cumsum(tiles_per)
    row_start = (tile_end - tiles_per) * MOE_TILE
    row_of_asg = jnp.sum(hot * row_start[None, :], axis=1) + rank
    asg_of_row = jnp.full((n_rows,), -1, jnp.int32).at[row_of_asg].set(jnp.arange(n_asg, dtype=jnp.int32))
    token_of_row = jnp.maximum(asg_of_row, 0) // TOP_K
    gate_of_row = jnp.where(asg_of_row >= 0, top_w.reshape(-1)[jnp.maximum(asg_of_row, 0)], 0.0)
    spare_rows = n_asg + jnp.arange(MOE_TILE, dtype=jnp.int32)
    dst3 = jnp.where(asg_of_row >= 0, asg_of_row, jnp.tile(spare_rows, max_tiles)).reshape(max_tiles, 1, MOE_TILE)
    dst_prev3 = jnp.concatenate([spare_rows.reshape(1, 1, MOE_TILE), dst3[:-1]], axis=0)
    n_tiles = tile_end[-1:].astype(jnp.int32)
    tile_expert = jnp.minimum(
        jnp.searchsorted(tile_end, jnp.arange(max_tiles, dtype=jnp.int32), side="right"),
        N_EXPERTS - 1).astype(jnp.int32)
    tile_expert = jnp.where(jnp.arange(max_tiles) < n_tiles[0], tile_expert,
                            tile_expert[jnp.maximum(n_tiles[0] - 1, 0)])

    wspec_in = pl.BlockSpec((1, 1, D, D_EXPERT), lambda i, te, nt: (layer, te[i], 0, 0))
    idx_spec = pl.BlockSpec((1, 1, MOE_TILE), lambda i, te, nt: (i, 0, 0), memory_space=pltpu.SMEM)
    next_spec = pl.BlockSpec((1, 1, MOE_TILE), lambda i, te, nt: (jnp.minimum(i + 1, max_tiles - 1), 0, 0),
                             memory_space=pltpu.SMEM)
    tok3 = token_of_row.reshape(max_tiles, 1, MOE_TILE)
    y = pl.pallas_call(
        _moe_kernel,
        grid_spec=pltpu.PrefetchScalarGridSpec(
            num_scalar_prefetch=2,
            grid=(max_tiles,),
            in_specs=[idx_spec, next_spec, idx_spec,
                      pl.BlockSpec(memory_space=pl.ANY),
                      pl.BlockSpec((MOE_TILE, 1), lambda i, te, nt: (i, 0)),
                      wspec_in, wspec_in,
                      pl.BlockSpec((1, 1, D_EXPERT, D), lambda i, te, nt: (layer, te[i], 0, 0))],
            out_specs=pl.BlockSpec(memory_space=pl.ANY),
            scratch_shapes=[pltpu.VMEM((2, MOE_TILE, D), jnp.float32),
                            pltpu.VMEM((2, MOE_TILE, D), jnp.float32),
                            pltpu.VMEM((D, D_EXPERT), jnp.bfloat16),
                            pltpu.VMEM((D, D_EXPERT), jnp.bfloat16),
                            pltpu.VMEM((D_EXPERT, D), jnp.bfloat16),
                            pltpu.SemaphoreType.DMA((2,)), pltpu.SemaphoreType.DMA(())],
        ),
        out_shape=jax.ShapeDtypeStruct((n_asg + MOE_TILE, D), jnp.float32),
        compiler_params=pltpu.CompilerParams(
            dimension_semantics=("arbitrary",), vmem_limit_bytes=MOE_VMEM_LIMIT),
        name="routed_moe",
    )(tile_expert, n_tiles, tok3, tok3, dst_prev3, h, gate_of_row[:, None], w_gate, w_up, w_down)
    return y.reshape(-1, TOP_K * D)


def _residual_ln_kernel(x_ref, mix_ref, g_ref, b_ref, *rest, n_mix, with_router):
    if with_router:
        rw_ref, o_ref, o16_ref, logit_ref = rest
    else:
        o_ref, o16_ref = rest
    d = x_ref.shape[1]
    h = DEEPNORM_ALPHA * x_ref[...]
    for s in range(n_mix):
        h = h + mix_ref[:, s * d:(s + 1) * d]
    mu = jnp.mean(h, axis=-1, keepdims=True)
    var = jnp.mean(jnp.square(h - mu), axis=-1, keepdims=True)
    y = (h - mu) * lax.rsqrt(var + LN_EPS) * g_ref[...] + b_ref[...]
    o_ref[...] = y
    y16 = y.astype(jnp.bfloat16)
    o16_ref[...] = y16
    if with_router:
        logit_ref[...] = jnp.dot(y16, rw_ref[...].astype(jnp.bfloat16), preferred_element_type=jnp.float32)


def residual_layer_norm(x, mix, g, b, router_w=None):
    n, d = x.shape
    n_mix = mix.shape[1] // d
    tm = 256
    row = lambda i: (i, 0)
    fixed = lambda i: (0, 0)
    in_specs = [pl.BlockSpec((tm, d), row), pl.BlockSpec((tm, n_mix * d), row),
                pl.BlockSpec((1, d), fixed), pl.BlockSpec((1, d), fixed)]
    args = [x, mix, g.astype(jnp.float32).reshape(1, d), b.astype(jnp.float32).reshape(1, d)]
    out_specs = [pl.BlockSpec((tm, d), row), pl.BlockSpec((tm, d), row)]
    out_shape = [jax.ShapeDtypeStruct((n, d), jnp.float32), jax.ShapeDtypeStruct((n, d), jnp.bfloat16)]
    if router_w is not None:
        e_pad = -router_w.shape[1] % LANE
        rw = jnp.pad(router_w.astype(jnp.float32), ((0, 0), (0, e_pad)))
        in_specs.append(pl.BlockSpec(rw.shape, fixed))
        args.append(rw)
        out_specs.append(pl.BlockSpec((tm, rw.shape[1]), row))
        out_shape.append(jax.ShapeDtypeStruct((n, rw.shape[1]), jnp.float32))
    outs = pl.pallas_call(
        functools.partial(_residual_ln_kernel, n_mix=n_mix, with_router=router_w is not None),
        grid=(n // tm,),
        in_specs=in_specs, out_specs=out_specs, out_shape=out_shape,
        compiler_params=pltpu.CompilerParams(
            dimension_semantics=("parallel",), vmem_limit_bytes=VMEM_LIMIT),
        name="residual_layer_norm",
    )(*args)
    if router_w is not None:
        return outs[0], outs[1], outs[2][:, :router_w.shape[1]]
    return outs[0], outs[1]


def _softplus(x):
    return jnp.maximum(x, 0.0) + jnp.log1p(jnp.exp(-jnp.abs(x)))


def _bdot(a, b):
    return jnp.dot(a.astype(jnp.bfloat16), b.astype(jnp.bfloat16), preferred_element_type=jnp.float32)


def _bdot_nt(a, b):
    return lax.dot_general(a.astype(jnp.bfloat16), b.astype(jnp.bfloat16), (((1,), (1,)), ((), ())),
                           preferred_element_type=jnp.float32)


def _bdot_tn(a, b):
    return lax.dot_general(a.astype(jnp.bfloat16), b.astype(jnp.bfloat16), (((0,), (0,)), ((), ())),
                           preferred_element_type=jnp.float32)


def _gdn_prep_kernel(x_ref, halo_ref, w_ref, o_ref):
    i = pl.program_id(1)
    j = pl.program_id(2)
    tt = x_ref.shape[1]
    x = x_ref[0]
    halo = jnp.where(i == 0, 0.0, halo_ref[0])
    xx = jnp.concatenate([halo, x], axis=0)
    w = w_ref[...]
    y = w[3:4] * x
    for tap in range(GDN_CONV - 1):
        lo = 8 - (GDN_CONV - 1) + tap
        y = y + w[tap:tap + 1] * xx[lo:lo + tt]
    y = y * jax.nn.sigmoid(y)
    q_blocks = GDN_QK_WIDTH // x_ref.shape[2]

    @pl.when(j >= 2 * q_blocks)
    def _():
        o_ref[0] = y

    @pl.when(j < 2 * q_blocks)
    def _():
        scale = jnp.where(j < q_blocks, GDN_HEAD_DIM ** -0.5, 1.0)
        for h in range(x_ref.shape[2] // GDN_HEAD_DIM):
            yh = y[:, h * GDN_HEAD_DIM:(h + 1) * GDN_HEAD_DIM]
            inv = lax.rsqrt(jnp.sum(yh * yh, axis=-1, keepdims=True) + RMS_EPS)
            o_ref[0, :, h * GDN_HEAD_DIM:(h + 1) * GDN_HEAD_DIM] = yh * inv * scale


def gdn_prep(qkv, conv_w):
    B, T, CH = qkv.shape
    tt, tc = GDN_TILE, 512
    return pl.pallas_call(
        _gdn_prep_kernel,
        grid=(B, T // tt, CH // tc),
        in_specs=[pl.BlockSpec((1, tt, tc), lambda b, i, j: (b, i, j)),
                  pl.BlockSpec((1, 8, tc), lambda b, i, j: (b, jnp.maximum(i * (tt // 8) - 1, 0), j)),
                  pl.BlockSpec((GDN_CONV, tc), lambda b, i, j: (0, j))],
        out_specs=pl.BlockSpec((1, tt, tc), lambda b, i, j: (b, i, j)),
        out_shape=jax.ShapeDtypeStruct((B, T, CH), jnp.float32),
        compiler_params=pltpu.CompilerParams(
            dimension_semantics=("parallel", "parallel", "parallel"), vmem_limit_bytes=VMEM_LIMIT),
        name="gdn_conv_silu_norm",
    )(qkv, qkv, conv_w)


def _gdn_kernel(q_ref, k_ref, v_ref, z_ref, acol_ref, arow_ref, bcol_ref, alog_ref, dt_ref, nw_ref,
                o_ref, state_s):
    C, Dh = GDN_CHUNK, GDN_HEAD_DIM

    @pl.when(pl.program_id(2) == 0)
    def _():
        state_s[...] = jnp.zeros(state_s.shape, jnp.float32)

    row = lax.broadcasted_iota(jnp.int32, (C, C), 0)
    col = lax.broadcasted_iota(jnp.int32, (C, C), 1)
    causal = row >= col
    strict = row > col
    nw = nw_ref[...]
    probs = []
    for n, kh in [(n, kh) for n in range(GDN_TILE // C) for kh in range(GDN_HK_STEP)]:
        sl = slice(n * C, (n + 1) * C)
        q = q_ref[0, sl, kh * Dh:(kh + 1) * Dh]
        k = k_ref[0, sl, kh * Dh:(kh + 1) * Dh]
        kk = _bdot_nt(k, k)
        qk = _bdot_nt(q, k)
        for hh in range(2):
            vh = 2 * kh + hh
            head = pl.program_id(1) * (2 * GDN_HK_STEP) + vh
            is_head = lax.broadcasted_iota(jnp.int32, (1, GDN_V_HEADS), 1) == head

            def pick(a):
                return jnp.sum(jnp.where(is_head, a, 0.0), axis=1, keepdims=True)

            neg_a = -jnp.exp(pick(alog_ref[...]))
            dt = pick(dt_ref[...])
            g_col = neg_a * _softplus(pick(acol_ref[0, sl, :]) + dt)
            g_row = neg_a * _softplus(arow_ref[0, pl.ds(head, 1), :][:, sl] + dt)
            beta = jax.nn.sigmoid(pick(bcol_ref[0, sl, :]))
            gc_col = jnp.sum(jnp.where(causal, g_row, 0.0), axis=1, keepdims=True)
            gc_row = jnp.sum(jnp.where(row <= col, g_col, 0.0), axis=0, keepdims=True)
            g_last = jnp.sum(g_row, axis=1, keepdims=True)
            decay = jnp.where(causal, jnp.exp(jnp.where(causal, gc_col - gc_row, 0.0)), 0.0)
            m = jnp.where(strict, beta * kk * decay, 0.0)
            e_col = jnp.exp(gc_col)
            v = v_ref[0, sl, vh * Dh:(vh + 1) * Dh]
            probs.append(dict(
                sl=sl, hh=vh, x=-m, p=m, a=qk * decay,
                rhs=jnp.concatenate([beta * v, beta * e_col * k], axis=1),
                qe=q * e_col, ke=k * jnp.exp(g_last - gc_col), s_decay=jnp.exp(g_last)))
    for _ in range(int(math.log2(C)) - 1):
        for pr in probs:
            pr["p"] = _bdot(pr["p"], pr["p"])
        for pr in probs:
            pr["x"] = pr["x"] + pr["p"] + _bdot(pr["x"], pr["p"])
    for pr in probs:
        pr["sol"] = pr["rhs"] + _bdot(pr["x"], pr["rhs"])
    for pr in probs:
        sl, hh = pr["sl"], pr["hh"]
        u, w = pr["sol"][:, :Dh], pr["sol"][:, Dh:]
        state = state_s[hh]
        ws = _bdot(jnp.concatenate([w, pr["qe"]], axis=0), state)
        v_new = u - ws[:C]
        o = ws[C:] + _bdot(pr["a"], v_new)
        state_s[hh] = state * pr["s_decay"] + _bdot_tn(pr["ke"], v_new)
        z = z_ref[0, sl, hh * Dh:(hh + 1) * Dh]
        o = o * lax.rsqrt(jnp.mean(o * o, axis=-1, keepdims=True) + RMS_EPS) * nw
        o_ref[0, sl, hh * Dh:(hh + 1) * Dh] = (o * (z * jax.nn.sigmoid(z))).astype(o_ref.dtype)


def gdn_core(qkv, z, a_raw, b_raw, a_log, dt_bias, norm_w):
    B, T, _ = qkv.shape
    Hk, Dh, tt = GDN_QK_HEADS, GDN_HEAD_DIM, GDN_TILE
    Hv = GDN_V_HEADS
    hs = GDN_HK_STEP
    qw, vw = hs * Dh, 2 * hs * Dh
    k_off, v_off = GDN_QK_WIDTH // qw, 2 * GDN_QK_WIDTH // vw
    col_spec = pl.BlockSpec((1, tt, Hv), lambda b, h, c: (b, c, 0))
    head_spec = pl.BlockSpec((1, Hv), lambda b, h, c: (0, 0))
    return pl.pallas_call(
        _gdn_kernel,
        grid=(B, Hk // hs, T // tt),
        in_specs=[pl.BlockSpec((1, tt, qw), lambda b, h, c: (b, c, h)),
                  pl.BlockSpec((1, tt, qw), lambda b, h, c: (b, c, k_off + h)),
                  pl.BlockSpec((1, tt, vw), lambda b, h, c: (b, c, v_off + h)),
                  pl.BlockSpec((1, tt, vw), lambda b, h, c: (b, c, h)),
                  col_spec,
                  pl.BlockSpec((1, Hv, tt), lambda b, h, c: (b, 0, c)),
                  col_spec, head_spec, head_spec,
                  pl.BlockSpec((1, Dh), lambda b, h, c: (0, 0))],
        out_specs=pl.BlockSpec((1, tt, vw), lambda b, h, c: (b, c, h)),
        out_shape=jax.ShapeDtypeStruct((B, T, GDN_V_WIDTH), jnp.bfloat16),
        scratch_shapes=[pltpu.VMEM((2 * hs, Dh, Dh), jnp.float32)],
        compiler_params=pltpu.CompilerParams(
            dimension_semantics=("parallel", "parallel", "arbitrary"), vmem_limit_bytes=VMEM_LIMIT),
        name="gated_delta_rule",
    )(qkv, qkv, qkv, z, a_raw, jnp.swapaxes(a_raw, 1, 2), b_raw,
      a_log.astype(jnp.float32).reshape(1, Hv), dt_bias.astype(jnp.float32).reshape(1, Hv),
      norm_w.astype(jnp.float32).reshape(1, Dh))


def gdn_mixer(x, w_in, conv_w, a_log, dt_bias, norm_w, w_out):
    B, T, _ = x.shape
    Hk, Hv, Dh = GDN_QK_HEADS, GDN_V_HEADS, GDN_HEAD_DIM
    qkv = dense(x, w_in, 0, GDN_CONV_CH)
    z = dense(x, w_in, GDN_CONV_CH, GDN_V_WIDTH)
    a_raw, b_raw = jnp.split(dense(x, w_in, GDN_CONV_CH + GDN_V_WIDTH), [Hv], axis=-1)
    o = gdn_core(gdn_prep(qkv, conv_w), z, a_raw, b_raw, a_log, dt_bias, norm_w)
    return dense(o, w_out)


def _rope_cast_kernel(x_ref, cos_ref, sin_ref, o_ref, *, rotate, scale):
    D = NSA_HEAD_DIM
    for h in range(x_ref.shape[2] // D):
        x = x_ref[0, :, h * D:(h + 1) * D]
        if rotate:
            x = x * cos_ref[...] + pltpu.roll(x, D // 2, axis=1) * sin_ref[...]
        if scale != 1.0:
            x = x * scale
        o_ref[0, :, h * D:(h + 1) * D] = x.astype(o_ref.dtype)


def rope_cast(x, col_start, n_cols, rotate, scale=1.0):
    B, T, _ = x.shape
    tt, tc, D = 512, 512, NSA_HEAD_DIM
    half = D // 2
    inv_freq = ROPE_THETA ** (-jnp.arange(half, dtype=jnp.float32) / half)
    ang = jnp.arange(T, dtype=jnp.float32)[:, None] * inv_freq[None, :]
    cos2 = jnp.concatenate([jnp.cos(ang), jnp.cos(ang)], axis=1)
    sin2 = jnp.concatenate([-jnp.sin(ang), jnp.sin(ang)], axis=1)
    off = col_start // tc
    return pl.pallas_call(
        functools.partial(_rope_cast_kernel, rotate=rotate, scale=scale),
        grid=(B, T // tt, n_cols // tc),
        in_specs=[pl.BlockSpec((1, tt, tc), lambda b, i, j: (b, i, off + j)),
                  pl.BlockSpec((tt, D), lambda b, i, j: (i, 0)),
                  pl.BlockSpec((tt, D), lambda b, i, j: (i, 0))],
        out_specs=pl.BlockSpec((1, tt, tc), lambda b, i, j: (b, i, j)),
        out_shape=jax.ShapeDtypeStruct((B, T, n_cols), jnp.bfloat16),
        compiler_params=pltpu.CompilerParams(
            dimension_semantics=("parallel", "parallel", "parallel"), vmem_limit_bytes=VMEM_LIMIT),
        name="rope_cast",
    )(x, cos2, sin2)


def _compress_kernel(x_ref, pe_ref, w1_ref, w2_ref, o_ref):
    S, D = CMP_STRIDE, NSA_HEAD_DIM
    n_half = x_ref.shape[1] // S
    n_cmp = (x_ref.shape[1] - CMP_BLOCK) // S + 1
    top = jnp.zeros((n_half, CMP_HIDDEN), jnp.float32)
    bot = jnp.zeros((n_half, CMP_HIDDEN), jnp.float32)
    for s in range(S):
        xs = x_ref[0, pl.ds(s, n_half, stride=S), :]
        top = top + _bdot(xs + pe_ref[0, s:s + 1, :], w1_ref[0, s * D:(s + 1) * D, :])
        bot = bot + _bdot(xs + pe_ref[0, S + s:S + s + 1, :], w1_ref[0, (S + s) * D:(S + s + 1) * D, :])
    nxt = jnp.concatenate([bot[1:], jnp.zeros((1, CMP_HIDDEN), jnp.float32)], axis=0)
    hid = top + nxt
    hid = hid * jax.nn.sigmoid(hid)
    out = _bdot(hid, w2_ref[0])
    live = lax.broadcasted_iota(jnp.int32, out.shape, 0) < n_cmp
    o_ref[0, 0] = jnp.where(live, out, 0.0).astype(o_ref.dtype)


def compress_blocks(kv, part, pe, w1, w2):
    B, T, _ = kv.shape
    G, D = NSA_KV_GROUPS, NSA_HEAD_DIM
    return pl.pallas_call(
        _compress_kernel,
        grid=(B, G),
        in_specs=[pl.BlockSpec((1, T, D), lambda b, g: (b, 0, part * G + g)),
                  pl.BlockSpec((1, CMP_BLOCK, D), lambda b, g: (part, 0, 0)),
                  pl.BlockSpec((1, CMP_BLOCK * D, CMP_HIDDEN), lambda b, g: (part, 0, 0)),
                  pl.BlockSpec((1, CMP_HIDDEN, D), lambda b, g: (part, 0, 0))],
        out_specs=pl.BlockSpec((1, 1, T // CMP_STRIDE, D), lambda b, g: (b, g, 0, 0)),
        out_shape=jax.ShapeDtypeStruct((B, G, T // CMP_STRIDE, D), jnp.bfloat16),
        compiler_params=pltpu.CompilerParams(
            dimension_semantics=("parallel", "parallel"), vmem_limit_bytes=VMEM_LIMIT),
        name="compress_blocks",
    )(kv, pe.astype(jnp.float32), w1.astype(jnp.float32), w2.astype(jnp.float32))


def nsa_shared_kv(h, kv_w, cmp_pe, cmp_w1, cmp_w2):
    W = NSA_KV_GROUPS * NSA_HEAD_DIM
    kv = dense(h, kv_w)
    k_cmp = compress_blocks(kv, 0, cmp_pe, cmp_w1, cmp_w2)
    v_cmp = compress_blocks(kv, 1, cmp_pe, cmp_w1, cmp_w2)
    k_slc = rope_cast(kv, 2 * W, W, True)
    v_slc = rope_cast(kv, 3 * W, W, False)
    k_win = rope_cast(kv, 4 * W, W, True)
    v_win = rope_cast(kv, 5 * W, W, False)
    return (k_cmp, v_cmp, k_slc, v_slc, k_win, v_win)


def _branch_gate(gate_ref, branch, r):
    row = branch * NSA_HEADS + pl.program_id(1) * NSA_HEADS_PER_GROUP + r
    return jax.nn.sigmoid(gate_ref[0, pl.ds(row, 1), :])


def _cmp_select_kernel(q_ref, kc_ref, vc_ref, ov_ref, gate_ref, o_ref, sel_ref):
    i = pl.program_id(2)
    tq, D, n_c, n_s = ATTN_TILE, NSA_HEAD_DIM, kc_ref.shape[2], T_BLOCKS
    tpos = i * tq + lax.broadcasted_iota(jnp.int32, (n_c, tq), 1)
    block_end = lax.broadcasted_iota(jnp.int32, (n_c, tq), 0) * CMP_STRIDE + (CMP_BLOCK - 1)
    visible = block_end <= tpos
    kc = kc_ref[0, 0]
    vc = vc_ref[0, 0]
    p_sum = jnp.zeros((n_c, tq), jnp.float32)
    for r in range(NSA_HEADS_PER_GROUP):
        s = lax.dot_general(kc, q_ref[0, :, r * D:(r + 1) * D], (((1,), (1,)), ((), ())),
                            preferred_element_type=jnp.float32)
        s = jnp.where(visible, s, NEG_INF)
        e = jnp.exp2(s - jnp.max(s, axis=0, keepdims=True))
        p = jnp.where(visible, e / jnp.sum(e, axis=0, keepdims=True), 0.0)
        o_t = lax.dot_general(vc, p.astype(jnp.bfloat16), (((0,), (0,)), ((), ())),
                              preferred_element_type=jnp.float32)
        o_ref[0, :, r * D:(r + 1) * D] = (o_t * _branch_gate(gate_ref, 0, r)).T
        p_sum = p_sum + p
    p_slc = jnp.dot(ov_ref[...], p_sum.astype(jnp.bfloat16), preferred_element_type=jnp.float32)
    blk = lax.broadcasted_iota(jnp.int32, (n_s, tq), 0)
    cur = (i * tq + lax.broadcasted_iota(jnp.int32, (n_s, tq), 1)) // SLC_BLOCK
    causal_blk = blk <= cur
    forced = (blk == 0) | (causal_blk & (blk > cur - SLC_LOCAL))
    score = jnp.where(causal_blk, jnp.where(forced, FORCE_SCORE, p_slc), -1.0)
    rank = jnp.zeros((n_s, tq), jnp.float32)
    for other in range(n_s):
        row = score[other:other + 1, :]
        ahead = (row > score) | ((row == score) & (blk > other))
        rank = rank + jnp.where(ahead, 1.0, 0.0)
    picked = (rank < float(min(SLC_TOPK, n_s))) & (score >= 0.0)
    sel_ref[0, 0] = jnp.where(picked, 1.0, 0.0).astype(sel_ref.dtype)


def nsa_compressed_select(q_cmp, k_cmp, v_cmp, gate_logits_t):
    B, T, _ = q_cmp.shape
    G, R, D, tq = NSA_KV_GROUPS, NSA_HEADS_PER_GROUP, NSA_HEAD_DIM, ATTN_TILE
    n_c = k_cmp.shape[2]
    n_s = T // SLC_BLOCK
    c0 = np.arange(n_c) * CMP_STRIDE
    s0 = np.arange(n_s) * SLC_BLOCK
    ov = np.clip(np.minimum(c0[None, :] + CMP_BLOCK, s0[:, None] + SLC_BLOCK)
                 - np.maximum(c0[None, :], s0[:, None]), 0, None) / CMP_BLOCK
    cmp_spec = pl.BlockSpec((1, 1, n_c, D), lambda b, g, i: (b, g, 0, 0))
    return pl.pallas_call(
        _cmp_select_kernel,
        grid=(B, G, T // tq),
        in_specs=[pl.BlockSpec((1, tq, R * D), lambda b, g, i: (b, i, g)), cmp_spec, cmp_spec,
                  pl.BlockSpec((n_s, n_c), lambda b, g, i: (0, 0)),
                  pl.BlockSpec((1, 3 * NSA_HEADS, tq), lambda b, g, i: (b, 0, i))],
        out_specs=[pl.BlockSpec((1, tq, R * D), lambda b, g, i: (b, i, g)),
                   pl.BlockSpec((1, 1, n_s, tq), lambda b, g, i: (b, g, 0, i))],
        out_shape=[jax.ShapeDtypeStruct((B, T, G * R * D), jnp.float32),
                   jax.ShapeDtypeStruct((B, G, n_s, T), jnp.bfloat16)],
        compiler_params=pltpu.CompilerParams(
            dimension_semantics=("parallel", "parallel", "parallel"), vmem_limit_bytes=VMEM_LIMIT),
        name="compressed_attention_select",
    )(q_cmp, k_cmp, v_cmp, jnp.asarray(ov, jnp.bfloat16), gate_logits_t)


def _attn_kernel(*refs, windowed):
    if windowed:
        q_ref, k_ref, v_ref, gate_ref, prev_ref, o_ref, m_s, l_s, acc_s = refs
    else:
        q_ref, k_ref, v_ref, sel_ref, gate_ref, prev_ref, o_ref, m_s, l_s, acc_s = refs
    i = pl.program_id(2)
    tq, tk, D = ATTN_TILE, ATTN_TK, NSA_HEAD_DIM
    j_hi = (i + 1) * (tq // tk) - 1
    m_s[...] = jnp.full(m_s.shape, NEG_INF, jnp.float32)
    l_s[...] = jnp.zeros(l_s.shape, jnp.float32)
    acc_s[...] = jnp.zeros(acc_s.shape, jnp.float32)
    tpos = i * tq + lax.broadcasted_iota(jnp.int32, (tk, tq), 1)

    def body(jj, carry):
        j = j_hi - jj
        start = pl.multiple_of(j * tk, tk)
        kblk = k_ref[0, pl.ds(start, tk), :]
        vblk = v_ref[0, pl.ds(start, tk), :]
        kpos = j * tk + lax.broadcasted_iota(jnp.int32, (tk, tq), 0)
        ok = kpos <= tpos
        if windowed:
            ok = ok & (kpos > tpos - WINDOW)
        else:
            blk_of_key = j * (tk // SLC_BLOCK) + lax.broadcasted_iota(
                jnp.int32, (tk, T_BLOCKS), 0) // SLC_BLOCK
            expand = (lax.broadcasted_iota(jnp.int32, (tk, T_BLOCKS), 1) == blk_of_key)
            selm = jnp.dot(expand.astype(jnp.bfloat16), sel_ref[0, 0],
                           preferred_element_type=jnp.float32)
            ok = ok & (selm > 0.5)
        def scores(r):
            return lax.dot_general(kblk, q_ref[0, :, r * D:(r + 1) * D], (((1,), (1,)), ((), ())),
                                   preferred_element_type=jnp.float32)

        s_next = scores(0)
        for r in range(NSA_HEADS_PER_GROUP):
            s = jnp.where(ok, s_next, NEG_INF)
            if r + 1 < NSA_HEADS_PER_GROUP:
                s_next = scores(r + 1)
            m_prev = m_s[r]
            m_new = jnp.maximum(m_prev, jnp.max(s, axis=0, keepdims=True))
            alpha = jnp.exp2(m_prev - m_new)
            p = jnp.exp2(s - m_new)
            l_s[r] = alpha * l_s[r] + jnp.sum(p, axis=0, keepdims=True)
            acc_s[r] = alpha * acc_s[r] + lax.dot_general(
                vblk, p.astype(jnp.bfloat16), (((0,), (0,)), ((), ())),
                preferred_element_type=jnp.float32)
            m_s[r] = m_new
        return carry

    j_lo = jnp.maximum(i * tq - WINDOW + 1, 0) // tk if windowed else 0
    lax.fori_loop(0, j_hi - j_lo + 1, body, 0)
    branch = 2 if windowed else 1
    for r in range(NSA_HEADS_PER_GROUP):
        gated = (acc_s[r] / l_s[r] * _branch_gate(gate_ref, branch, r)).T
        o_ref[0, :, r * D:(r + 1) * D] = (prev_ref[0, :, r * D:(r + 1) * D] + gated).astype(o_ref.dtype)


def masked_attention(qs, kb, vb, sel, gate_logits_t, prev, out_dtype):
    B, T, _ = qs.shape
    G, R, D = NSA_KV_GROUPS, NSA_HEADS_PER_GROUP, NSA_HEAD_DIM
    tq = ATTN_TILE
    windowed = sel is None
    q_spec = pl.BlockSpec((1, tq, R * D), lambda b, g, i: (b, i, g))
    in_specs = [q_spec,
                pl.BlockSpec((1, T, D), lambda b, g, i: (b, 0, g)),
                pl.BlockSpec((1, T, D), lambda b, g, i: (b, 0, g))]
    args = [qs, kb, vb]
    if not windowed:
        in_specs.append(pl.BlockSpec((1, 1, T // SLC_BLOCK, tq), lambda b, g, i: (b, g, 0, i)))
        args.append(sel)
    in_specs += [pl.BlockSpec((1, 3 * NSA_HEADS, tq), lambda b, g, i: (b, 0, i)), q_spec]
    args += [gate_logits_t, prev]
    return pl.pallas_call(
        functools.partial(_attn_kernel, windowed=windowed),
        grid=(B, G, T // tq),
        in_specs=in_specs,
        out_specs=pl.BlockSpec((1, tq, R * D), lambda b, g, i: (b, i, g)),
        out_shape=jax.ShapeDtypeStruct((B, T, G * R * D), out_dtype),
        scratch_shapes=[pltpu.VMEM((R, 1, tq), jnp.float32),
                        pltpu.VMEM((R, 1, tq), jnp.float32),
                        pltpu.VMEM((R, D, tq), jnp.float32)],
        compiler_params=pltpu.CompilerParams(
            dimension_semantics=("parallel", "parallel", "arbitrary"), vmem_limit_bytes=VMEM_LIMIT),
        name="window_attention" if windowed else "selected_attention",
    )(*args)


def nsa_mixer(x, w_q, w_out, k_cmp, v_cmp, k_slc, v_slc, k_win, v_win):
    B, T, _ = x.shape
    H, D = NSA_HEADS, NSA_HEAD_DIM
    q = dense(x, w_q, 0, NSA_Q_WIDTH)
    gate_t = jnp.swapaxes(dense(x, w_q, NSA_Q_WIDTH), 1, 2)
    q_scale = D ** -0.5 * math.log2(math.e)
    q_cmp = rope_cast(q, 0, NSA_Q_WIDTH, False, q_scale)
    q_rot = rope_cast(q, 0, NSA_Q_WIDTH, True, q_scale)
    o, sel = nsa_compressed_select(q_cmp, k_cmp, v_cmp, gate_t)
    o = masked_attention(q_rot, k_slc, v_slc, sel, gate_t, o, jnp.float32)
    o = masked_attention(q_rot, k_win, v_win, None, gate_t, o, jnp.bfloat16)
    return dense(o, w_out)


def kernel(x, a_w_in, a_conv_w, a_a_log, a_dt_bias, a_norm_w, a_w_out, kv_w, cmp_pe, cmp_w1, cmp_w2,
           b_w_q, b_w_out, router_w, router_bias, moe_w_gate, moe_w_up, moe_w_down, ln_g, ln_b):
    B, T, D = x.shape
    xf = x.astype(jnp.float32).reshape(B * T, D)
    x16 = xf.astype(jnp.bfloat16)
    shared_kv = None
    for layer in range(DEPTH):
        xin = x16.reshape(B, T, D)
        if layer < N_A_LAYERS:
            mix = gdn_mixer(xin, a_w_in[layer], a_conv_w[layer], a_a_log[layer], a_dt_bias[layer],
                            a_norm_w[layer], a_w_out[layer])
        else:
            if shared_kv is None:
                shared_kv = nsa_shared_kv(xin, kv_w, cmp_pe, cmp_w1, cmp_w2)
            j = layer - N_A_LAYERS
            mix = nsa_mixer(xin, b_w_q[j], b_w_out[j], *shared_kv)
        xf, x16, logits = residual_layer_norm(xf, mix.reshape(B * T, D), ln_g[layer, 0], ln_b[layer, 0],
                                              router_w)
        ffn = moe_ffn(xf, logits, router_bias, moe_w_gate, moe_w_up, moe_w_down, layer)
        xf, x16 = residual_layer_norm(xf, ffn, ln_g[layer, 1], ln_b[layer, 1])
    return xf.reshape(B, T, D).astype(x.dtype)
```

```python
import functools
import math

import jax
import jax.numpy as jnp
import numpy as np
from jax import lax
from jax.experimental import pallas as pl
from jax.experimental.pallas import tpu as pltpu

D_MODEL = 2048
BATCH = 2
SEQ = 4096
DEPTH = 2
N_A_LAYERS = DEPTH // 2

GDN_HEAD_DIM = 128
GDN_QK_HEADS = D_MODEL // GDN_HEAD_DIM
GDN_V_HEADS = 2 * GDN_QK_HEADS
GDN_QK_WIDTH = GDN_QK_HEADS * GDN_HEAD_DIM
GDN_V_WIDTH = GDN_V_HEADS * GDN_HEAD_DIM
GDN_CONV_CH = 2 * GDN_QK_WIDTH + GDN_V_WIDTH
GDN_CHUNK = 64
GDN_CONV = 4
GDN_TILE = 256
GDN_HK_STEP = 4

NSA_HEAD_DIM = 128
NSA_HEADS = D_MODEL // NSA_HEAD_DIM
NSA_KV_GROUPS = 4
NSA_HEADS_PER_GROUP = NSA_HEADS // NSA_KV_GROUPS
NSA_Q_WIDTH = NSA_HEADS * NSA_HEAD_DIM
CMP_BLOCK = 32
CMP_STRIDE = 16
CMP_HIDDEN = 512
SLC_BLOCK = 64
SLC_TOPK = 16
SLC_LOCAL = 2
WINDOW = 512
WIN_Q_BLOCK = 128
SLC_Q_BLOCK = 64
ROPE_THETA = 10000.0

N_EXPERTS = 32
N_GROUPS = 8
EXPERTS_PER_GROUP = N_EXPERTS // N_GROUPS
TOP_K = 2
D_EXPERT = D_MODEL // 4

DEEPNORM_ALPHA = (2 * DEPTH) ** 0.25
LN_EPS = 1e-5
RMS_EPS = 1e-6
NEG_INF = -1e30
FORCE_SCORE = 1e6

LANE = 128
VMEM_LIMIT = 48 * 1024 * 1024
MOE_VMEM_LIMIT = 56 * 1024 * 1024
MOE_TILE = 256
ATTN_TILE = 256
ATTN_TK = 256
T_BLOCKS = SEQ // SLC_BLOCK


def _mm_kernel(a_ref, b_ref, o_ref, b16_s):
    @pl.when(pl.program_id(1) == 0)
    def _():
        b16_s[...] = b_ref[...].astype(jnp.bfloat16)

    o_ref[...] = jnp.dot(a_ref[...], b16_s[...], preferred_element_type=jnp.float32)


def pmatmul(a, b, col_start=0, n_cols=None, row_block=0, tm=512):
    m, k = a.shape
    n_cols = b.shape[1] - col_start if n_cols is None else n_cols
    if n_cols % LANE or col_start % LANE:
        b = jnp.pad(b[:, col_start:col_start + n_cols], ((0, 0), (0, -n_cols % LANE)))
        return pmatmul(a, b)[:, :n_cols]
    tn = math.gcd(n_cols, 1024 if k <= 2048 else 512)
    assert col_start % tn == 0 and m % tm == 0 and b.shape[0] % k == 0
    off = col_start // tn
    return pl.pallas_call(
        _mm_kernel,
        grid=(n_cols // tn, m // tm),
        in_specs=[pl.BlockSpec((tm, k), lambda j, i: (i, 0)),
                  pl.BlockSpec((k, tn), lambda j, i: (row_block, j + off))],
        out_specs=pl.BlockSpec((tm, tn), lambda j, i: (i, j)),
        out_shape=jax.ShapeDtypeStruct((m, n_cols), jnp.float32),
        scratch_shapes=[pltpu.VMEM((k, tn), jnp.bfloat16)],
        compiler_params=pltpu.CompilerParams(
            dimension_semantics=("parallel", "arbitrary"), vmem_limit_bytes=VMEM_LIMIT),
        name="dense_matmul",
    )(a.astype(jnp.bfloat16), b.astype(jnp.float32))


def dense(x, w, col_start=0, n_cols=None):
    lead = x.shape[:-1]
    out = pmatmul(x.reshape(-1, x.shape[-1]), w, col_start, n_cols)
    return out.reshape(lead + (out.shape[-1],))


def _moe_kernel(tile_expert_ref, n_tiles_ref, tok_ref, tok_next_ref, dst_prev_ref, x_hbm, gate_ref,
                wg_ref, wu_ref, wd_ref, out_hbm, xbuf, ybuf, wg16, wu16, wd16, sem_in, sem_out):
    i = pl.program_id(0)
    n_tiles = n_tiles_ref[0]
    slot = i % 2
    spare = out_hbm.shape[0] - MOE_TILE

    def gather_row(idx_ref, s, r):
        return pltpu.make_async_copy(x_hbm.at[pl.ds(idx_ref[0, 0, r], 1)], xbuf.at[s, pl.ds(r, 1)],
                                     sem_in.at[s])

    def scatter_row(s, r, dst_row):
        return pltpu.make_async_copy(ybuf.at[s, pl.ds(r, 1)], out_hbm.at[pl.ds(dst_row, 1)], sem_out)

    def wait_gather(s):
        pltpu.make_async_copy(x_hbm.at[pl.ds(0, MOE_TILE)], xbuf.at[s], sem_in.at[s]).wait()

    def wait_scatter():
        pltpu.make_async_copy(ybuf.at[0], out_hbm.at[pl.ds(0, MOE_TILE)], sem_out).wait()

    @pl.when(i == 0)
    def _():
        ybuf[...] = jnp.zeros(ybuf.shape, jnp.float32)

        def first_rows(r, c):
            gather_row(tok_ref, 0, r).start()
            scatter_row(0, r, spare + r).start()
            return c
        lax.fori_loop(0, MOE_TILE, first_rows, 0, unroll=8)

    @pl.when(i < n_tiles)
    def _():
        @pl.when(jnp.logical_or(i == 0, tile_expert_ref[i] != tile_expert_ref[jnp.maximum(i - 1, 0)]))
        def _():
            wg16[...] = wg_ref[0, 0].astype(jnp.bfloat16)
            wu16[...] = wu_ref[0, 0].astype(jnp.bfloat16)
            wd16[...] = wd_ref[0, 0].astype(jnp.bfloat16)

        wait_gather(slot)
        wait_scatter()
        x = xbuf[slot].astype(jnp.bfloat16)
        for r in range(MOE_TILE):
            gather_row(tok_next_ref, 1 - slot, r).start()
            scatter_row(1 - slot, r, dst_prev_ref[0, 0, r]).start()
        g = jnp.dot(x, wg16[...], preferred_element_type=jnp.float32)
        u = jnp.dot(x, wu16[...], preferred_element_type=jnp.float32)
        h = (g * jax.nn.sigmoid(g)) * u * gate_ref[...]
        ybuf[slot] = jnp.dot(h.astype(jnp.bfloat16), wd16[...], preferred_element_type=jnp.float32)

    @pl.when(i == n_tiles)
    def _():
        wait_scatter()

        def last_rows(r, c):
            scatter_row(1 - slot, r, dst_prev_ref[0, 0, r]).start()
            return c
        lax.fori_loop(0, MOE_TILE, last_rows, 0, unroll=8)
        wait_scatter()
        wait_gather(slot)


def moe_ffn(h, router_logits, router_bias, w_gate, w_up, w_down, layer):
    n_tok, D = h.shape
    aff = jax.nn.sigmoid(router_logits.astype(jnp.float32))
    biased = (aff + router_bias.astype(jnp.float32)).reshape(-1, N_GROUPS, EXPERTS_PER_GROUP)

    def top2(v):
        i1 = jnp.argmax(v, axis=-1)
        rest = jnp.where(jnp.arange(v.shape[-1]) == i1[..., None], -jnp.inf, v)
        i2 = jnp.argmax(rest, axis=-1)
        return jnp.max(v, axis=-1), jnp.max(rest, axis=-1), i1, i2

    g1, g2, _, _ = top2(biased)
    best_group = jnp.argmax(g1 + g2, axis=-1)
    cand = jnp.take_along_axis(biased, best_group[:, None, None], axis=1)[:, 0]
    _, _, i1, i2 = top2(cand)
    top_idx = best_group[:, None] * EXPERTS_PER_GROUP + jnp.stack([i1, i2], axis=-1)
    top_aff = jnp.take_along_axis(aff, top_idx, axis=-1)
    top_w = top_aff / jnp.sum(top_aff, axis=-1, keepdims=True)

    n_asg = n_tok * TOP_K
    max_tiles = n_asg // MOE_TILE + N_EXPERTS + 1
    n_rows = max_tiles * MOE_TILE
    e_flat = top_idx.reshape(-1).astype(jnp.int32)
    hot = (e_flat[:, None] == jnp.arange(N_EXPERTS, dtype=jnp.int32)[None, :]).astype(jnp.int32)
    running = jnp.cumsum(hot, axis=0)
    rank = jnp.sum(hot * (running - 1), axis=1)
    counts = running[-1]
    tiles_per = (counts + MOE_TILE - 1) // MOE_TILE
    tile_end = jnp.cumsum(tiles_per)
    row_start = (tile_end - tiles_per) * MOE_TILE
    row_of_asg = jnp.sum(hot * row_start[None, :], axis=1) + rank
    asg_of_row = jnp.full((n_rows,), -1, jnp.int32).at[row_of_asg].set(jnp.arange(n_asg, dtype=jnp.int32))
    token_of_row = jnp.maximum(asg_of_row, 0) // TOP_K
    gate_of_row = jnp.where(asg_of_row >= 0, top_w.reshape(-1)[jnp.maximum(asg_of_row, 0)], 0.0)
    spare_rows = n_asg + jnp.arange(MOE_TILE, dtype=jnp.int32)
    slot_major = (asg_of_row % TOP_K) * n_tok + asg_of_row // TOP_K
    dst3 = jnp.where(asg_of_row >= 0, slot_major, jnp.tile(spare_rows, max_tiles)).reshape(max_tiles, 1, MOE_TILE)
    dst_prev3 = jnp.concatenate([spare_rows.reshape(1, 1, MOE_TILE), dst3[:-1]], axis=0)
    n_tiles = tile_end[-1:].astype(jnp.int32)
    tile_expert = jnp.minimum(
        jnp.searchsorted(tile_end, jnp.arange(max_tiles, dtype=jnp.int32), side="right"),
        N_EXPERTS - 1).astype(jnp.int32)
    tile_expert = jnp.where(jnp.arange(max_tiles) < n_tiles[0], tile_expert,
                            tile_expert[jnp.maximum(n_tiles[0] - 1, 0)])

    wspec_in = pl.BlockSpec((1, 1, D, D_EXPERT), lambda i, te, nt: (layer, te[i], 0, 0))
    idx_spec = pl.BlockSpec((1, 1, MOE_TILE), lambda i, te, nt: (i, 0, 0), memory_space=pltpu.SMEM)
    next_spec = pl.BlockSpec((1, 1, MOE_TILE), lambda i, te, nt: (jnp.minimum(i + 1, max_tiles - 1), 0, 0),
                             memory_space=pltpu.SMEM)
    tok3 = token_of_row.reshape(max_tiles, 1, MOE_TILE)
    y = pl.pallas_call(
        _moe_kernel,
        grid_spec=pltpu.PrefetchScalarGridSpec(
            num_scalar_prefetch=2,
            grid=(max_tiles,),
            in_specs=[idx_spec, next_spec, idx_spec,
                      pl.BlockSpec(memory_space=pl.ANY),
                      pl.BlockSpec((MOE_TILE, 1), lambda i, te, nt: (i, 0)),
                      wspec_in, wspec_in,
                      pl.BlockSpec((1, 1, D_EXPERT, D), lambda i, te, nt: (layer, te[i], 0, 0))],
            out_specs=pl.BlockSpec(memory_space=pl.ANY),
            scratch_shapes=[pltpu.VMEM((2, MOE_TILE, D), jnp.float32),
                            pltpu.VMEM((2, MOE_TILE, D), jnp.float32),
                            pltpu.VMEM((D, D_EXPERT), jnp.bfloat16),
                            pltpu.VMEM((D, D_EXPERT), jnp.bfloat16),
                            pltpu.VMEM((D_EXPERT, D), jnp.bfloat16),
                            pltpu.SemaphoreType.DMA((2,)), pltpu.SemaphoreType.DMA(())],
        ),
        out_shape=jax.ShapeDtypeStruct((n_asg + MOE_TILE, D), jnp.float32),
        compiler_params=pltpu.CompilerParams(
            dimension_semantics=("arbitrary",), vmem_limit_bytes=MOE_VMEM_LIMIT),
        name="routed_moe",
    )(tile_expert, n_tiles, tok3, tok3, dst_prev3, h, gate_of_row[:, None], w_gate, w_up, w_down)
    return y


def _residual_ln_kernel(x_ref, *rest, n_mix, with_router):
    mix_refs, rest = rest[:n_mix], rest[n_mix:]
    if with_router:
        g_ref, b_ref, rw_ref, o_ref, o16_ref, logit_ref = rest
    else:
        g_ref, b_ref, o_ref, o16_ref = rest
    h = DEEPNORM_ALPHA * x_ref[...]
    for mix_ref in mix_refs:
        h = h + mix_ref[...]
    mu = jnp.mean(h, axis=-1, keepdims=True)
    var = jnp.mean(jnp.square(h - mu), axis=-1, keepdims=True)
    y = (h - mu) * lax.rsqrt(var + LN_EPS) * g_ref[...] + b_ref[...]
    o_ref[...] = y
    y16 = y.astype(jnp.bfloat16)
    o16_ref[...] = y16
    if with_router:
        logit_ref[...] = jnp.dot(y16, rw_ref[...].astype(jnp.bfloat16), preferred_element_type=jnp.float32)


def residual_layer_norm(x, mix, g, b, router_w=None, n_mix=1):
    n, d = x.shape
    tm = 256
    row = lambda i: (i, 0)
    fixed = lambda i: (0, 0)
    in_specs = ([pl.BlockSpec((tm, d), row)]
                + [pl.BlockSpec((tm, d), functools.partial(lambda i, s: (i + s * (n // tm), 0), s=s))
                   for s in range(n_mix)]
                + [pl.BlockSpec((1, d), fixed), pl.BlockSpec((1, d), fixed)])
    args = [x] + [mix] * n_mix + [g.astype(jnp.float32).reshape(1, d), b.astype(jnp.float32).reshape(1, d)]
    out_specs = [pl.BlockSpec((tm, d), row), pl.BlockSpec((tm, d), row)]
    out_shape = [jax.ShapeDtypeStruct((n, d), jnp.float32), jax.ShapeDtypeStruct((n, d), jnp.bfloat16)]
    if router_w is not None:
        e_pad = -router_w.shape[1] % LANE
        rw = jnp.pad(router_w.astype(jnp.float32), ((0, 0), (0, e_pad)))
        in_specs.append(pl.BlockSpec(rw.shape, fixed))
        args.append(rw)
        out_specs.append(pl.BlockSpec((tm, rw.shape[1]), row))
        out_shape.append(jax.ShapeDtypeStruct((n, rw.shape[1]), jnp.float32))
    outs = pl.pallas_call(
        functools.partial(_residual_ln_kernel, n_mix=n_mix, with_router=router_w is not None),
        grid=(n // tm,),
        in_specs=in_specs, out_specs=out_specs, out_shape=out_shape,
        compiler_params=pltpu.CompilerParams(
            dimension_semantics=("parallel",), vmem_limit_bytes=VMEM_LIMIT),
        name="residual_layer_norm",
    )(*args)
    if router_w is not None:
        return outs[0], outs[1], outs[2][:, :router_w.shape[1]]
    return outs[0], outs[1]


def _softplus(x):
    return jnp.maximum(x, 0.0) + jnp.log1p(jnp.exp(-jnp.abs(x)))


def _bdot(a, b):
    return jnp.dot(a.astype(jnp.bfloat16), b.astype(jnp.bfloat16), preferred_element_type=jnp.float32)


def _bdot_nt(a, b):
    return lax.dot_general(a.astype(jnp.bfloat16), b.astype(jnp.bfloat16), (((1,), (1,)), ((), ())),
                           preferred_element_type=jnp.float32)


def _bdot_tn(a, b):
    return lax.dot_general(a.astype(jnp.bfloat16), b.astype(jnp.bfloat16), (((0,), (0,)), ((), ())),
                           preferred_element_type=jnp.float32)


def _gdn_prep_kernel(x_ref, halo_ref, w_ref, o_ref):
    i = pl.program_id(1)
    j = pl.program_id(2)
    tt = x_ref.shape[1]
    x = x_ref[0]
    halo = jnp.where(i == 0, 0.0, halo_ref[0])
    xx = jnp.concatenate([halo, x], axis=0)
    w = w_ref[...]
    y = w[3:4] * x
    for tap in range(GDN_CONV - 1):
        lo = 8 - (GDN_CONV - 1) + tap
        y = y + w[tap:tap + 1] * xx[lo:lo + tt]
    y = y * jax.nn.sigmoid(y)
    q_blocks = GDN_QK_WIDTH // x_ref.shape[2]

    @pl.when(j >= 2 * q_blocks)
    def _():
        o_ref[0] = y

    @pl.when(j < 2 * q_blocks)
    def _():
        scale = jnp.where(j < q_blocks, GDN_HEAD_DIM ** -0.5, 1.0)
        for h in range(x_ref.shape[2] // GDN_HEAD_DIM):
            yh = y[:, h * GDN_HEAD_DIM:(h + 1) * GDN_HEAD_DIM]
            inv = lax.rsqrt(jnp.sum(yh * yh, axis=-1, keepdims=True) + RMS_EPS)
            o_ref[0, :, h * GDN_HEAD_DIM:(h + 1) * GDN_HEAD_DIM] = yh * inv * scale


def gdn_prep(qkv, conv_w):
    B, T, CH = qkv.shape
    tt, tc = GDN_TILE, 512
    return pl.pallas_call(
        _gdn_prep_kernel,
        grid=(B, T // tt, CH // tc),
        in_specs=[pl.BlockSpec((1, tt, tc), lambda b, i, j: (b, i, j)),
                  pl.BlockSpec((1, 8, tc), lambda b, i, j: (b, jnp.maximum(i * (tt // 8) - 1, 0), j)),
                  pl.BlockSpec((GDN_CONV, tc), lambda b, i, j: (0, j))],
        out_specs=pl.BlockSpec((1, tt, tc), lambda b, i, j: (b, i, j)),
        out_shape=jax.ShapeDtypeStruct((B, T, CH), jnp.float32),
        compiler_params=pltpu.CompilerParams(
            dimension_semantics=("parallel", "parallel", "parallel"), vmem_limit_bytes=VMEM_LIMIT),
        name="gdn_conv_silu_norm",
    )(qkv, qkv, conv_w)


def _gdn_kernel(q_ref, k_ref, v_ref, z_ref, acol_ref, arow_ref, bcol_ref, alog_ref, dt_ref, nw_ref,
                o_ref, state_s):
    C, Dh = GDN_CHUNK, GDN_HEAD_DIM

    @pl.when(pl.program_id(2) == 0)
    def _():
        state_s[...] = jnp.zeros(state_s.shape, jnp.float32)

    row = lax.broadcasted_iota(jnp.int32, (C, C), 0)
    col = lax.broadcasted_iota(jnp.int32, (C, C), 1)
    causal = row >= col
    strict = row > col
    nw = nw_ref[...]
    probs = []
    for n, kh in [(n, kh) for n in range(GDN_TILE // C) for kh in range(GDN_HK_STEP)]:
        sl = slice(n * C, (n + 1) * C)
        q = q_ref[0, sl, kh * Dh:(kh + 1) * Dh]
        k = k_ref[0, sl, kh * Dh:(kh + 1) * Dh]
        kk = _bdot_nt(k, k)
        qk = _bdot_nt(q, k)
        for hh in range(2):
            vh = 2 * kh + hh
            head = pl.program_id(1) * (2 * GDN_HK_STEP) + vh
            is_head = lax.broadcasted_iota(jnp.int32, (1, GDN_V_HEADS), 1) == head

            def pick(a):
                return jnp.sum(jnp.where(is_head, a, 0.0), axis=1, keepdims=True)

            neg_a = -jnp.exp(pick(alog_ref[...]))
            dt = pick(dt_ref[...])
            g_col = neg_a * _softplus(pick(acol_ref[0, sl, :]) + dt)
            g_row = neg_a * _softplus(arow_ref[0, pl.ds(head, 1), :][:, sl] + dt)
            beta = jax.nn.sigmoid(pick(bcol_ref[0, sl, :]))
            gc_col = jnp.sum(jnp.where(causal, g_row, 0.0), axis=1, keepdims=True)
            gc_row = jnp.sum(jnp.where(row <= col, g_col, 0.0), axis=0, keepdims=True)
            g_last = jnp.sum(g_row, axis=1, keepdims=True)
            decay = jnp.where(causal, jnp.exp(jnp.where(causal, gc_col - gc_row, 0.0)), 0.0)
            m = jnp.where(strict, beta * kk * decay, 0.0)
            e_col = jnp.exp(gc_col)
            v = v_ref[0, sl, vh * Dh:(vh + 1) * Dh]
            probs.append(dict(
                sl=sl, hh=vh, x=-m, p=m, a=qk * decay,
                rhs=jnp.concatenate([beta * v, beta * e_col * k], axis=1),
                qe=q * e_col, ke=k * jnp.exp(g_last - gc_col), s_decay=jnp.exp(g_last)))
    for _ in range(int(math.log2(C)) - 1):
        for pr in probs:
            pr["p"] = _bdot(pr["p"], pr["p"])
        for pr in probs:
            pr["x"] = pr["x"] + pr["p"] + _bdot(pr["x"], pr["p"])
    for pr in probs:
        pr["sol"] = pr["rhs"] + _bdot(pr["x"], pr["rhs"])
    for pr in probs:
        sl, hh = pr["sl"], pr["hh"]
        u, w = pr["sol"][:, :Dh], pr["sol"][:, Dh:]
        state = state_s[hh]
        ws = _bdot(jnp.concatenate([w, pr["qe"]], axis=0), state)
        v_new = u - ws[:C]
        o = ws[C:] + _bdot(pr["a"], v_new)
        state_s[hh] = state * pr["s_decay"] + _bdot_tn(pr["ke"], v_new)
        z = z_ref[0, sl, hh * Dh:(hh + 1) * Dh]
        o = o * lax.rsqrt(jnp.mean(o * o, axis=-1, keepdims=True) + RMS_EPS) * nw
        o_ref[0, sl, hh * Dh:(hh + 1) * Dh] = (o * (z * jax.nn.sigmoid(z))).astype(o_ref.dtype)


def gdn_core(qkv, z, a_raw, b_raw, a_log, dt_bias, norm_w):
    B, T, _ = qkv.shape
    Hk, Dh, tt = GDN_QK_HEADS, GDN_HEAD_DIM, GDN_TILE
    Hv = GDN_V_HEADS
    hs = GDN_HK_STEP
    qw, vw = hs * Dh, 2 * hs * Dh
    k_off, v_off = GDN_QK_WIDTH // qw, 2 * GDN_QK_WIDTH // vw
    col_spec = pl.BlockSpec((1, tt, Hv), lambda b, h, c: (b, c, 0))
    head_spec = pl.BlockSpec((1, Hv), lambda b, h, c: (0, 0))
    return pl.pallas_call(
        _gdn_kernel,
        grid=(B, Hk // hs, T // tt),
        in_specs=[pl.BlockSpec((1, tt, qw), lambda b, h, c: (b, c, h)),
                  pl.BlockSpec((1, tt, qw), lambda b, h, c: (b, c, k_off + h)),
                  pl.BlockSpec((1, tt, vw), lambda b, h, c: (b, c, v_off + h)),
                  pl.BlockSpec((1, tt, vw), lambda b, h, c: (b, c, h)),
                  col_spec,
                  pl.BlockSpec((1, Hv, tt), lambda b, h, c: (b, 0, c)),
                  col_spec, head_spec, head_spec,
                  pl.BlockSpec((1, Dh), lambda b, h, c: (0, 0))],
        out_specs=pl.BlockSpec((1, tt, vw), lambda b, h, c: (b, c, h)),
        out_shape=jax.ShapeDtypeStruct((B, T, GDN_V_WIDTH), jnp.bfloat16),
        scratch_shapes=[pltpu.VMEM((2 * hs, Dh, Dh), jnp.float32)],
        compiler_params=pltpu.CompilerParams(
            dimension_semantics=("parallel", "parallel", "arbitrary"), vmem_limit_bytes=VMEM_LIMIT),
        name="gated_delta_rule",
    )(qkv, qkv, qkv, z, a_raw, jnp.swapaxes(a_raw, 1, 2), b_raw,
      a_log.astype(jnp.float32).reshape(1, Hv), dt_bias.astype(jnp.float32).reshape(1, Hv),
      norm_w.astype(jnp.float32).reshape(1, Dh))


def gdn_mixer(x, w_in, conv_w, a_log, dt_bias, norm_w, w_out):
    B, T, _ = x.shape
    Hk, Hv, Dh = GDN_QK_HEADS, GDN_V_HEADS, GDN_HEAD_DIM
    qkv = dense(x, w_in, 0, GDN_CONV_CH)
    z = dense(x, w_in, GDN_CONV_CH, GDN_V_WIDTH)
    a_raw, b_raw = jnp.split(dense(x, w_in, GDN_CONV_CH + GDN_V_WIDTH), [Hv], axis=-1)
    o = gdn_core(gdn_prep(qkv, conv_w), z, a_raw, b_raw, a_log, dt_bias, norm_w)
    return dense(o, w_out)


def _rope_cast_kernel(x_ref, cos_ref, sin_ref, o_ref, *, rotate, scale):
    D = NSA_HEAD_DIM
    for h in range(x_ref.shape[2] // D):
        x = x_ref[0, :, h * D:(h + 1) * D]
        if rotate:
            x = x * cos_ref[...] + pltpu.roll(x, D // 2, axis=1) * sin_ref[...]
        if scale != 1.0:
            x = x * scale
        o_ref[0, :, h * D:(h + 1) * D] = x.astype(o_ref.dtype)


def rope_cast(x, col_start, n_cols, rotate, scale=1.0):
    B, T, _ = x.shape
    tt, tc, D = 512, 512, NSA_HEAD_DIM
    half = D // 2
    inv_freq = ROPE_THETA ** (-jnp.arange(half, dtype=jnp.float32) / half)
    ang = jnp.arange(T, dtype=jnp.float32)[:, None] * inv_freq[None, :]
    cos2 = jnp.concatenate([jnp.cos(ang), jnp.cos(ang)], axis=1)
    sin2 = jnp.concatenate([-jnp.sin(ang), jnp.sin(ang)], axis=1)
    off = col_start // tc
    return pl.pallas_call(
        functools.partial(_rope_cast_kernel, rotate=rotate, scale=scale),
        grid=(B, T // tt, n_cols // tc),
        in_specs=[pl.BlockSpec((1, tt, tc), lambda b, i, j: (b, i, off + j)),
                  pl.BlockSpec((tt, D), lambda b, i, j: (i, 0)),
                  pl.BlockSpec((tt, D), lambda b, i, j: (i, 0))],
        out_specs=pl.BlockSpec((1, tt, tc), lambda b, i, j: (b, i, j)),
        out_shape=jax.ShapeDtypeStruct((B, T, n_cols), jnp.bfloat16),
        compiler_params=pltpu.CompilerParams(
            dimension_semantics=("parallel", "parallel", "parallel"), vmem_limit_bytes=VMEM_LIMIT),
        name="rope_cast",
    )(x, cos2, sin2)


def _compress_kernel(x_ref, pe_ref, w1_ref, w2_ref, o_ref):
    S, D = CMP_STRIDE, NSA_HEAD_DIM
    n_half = x_ref.shape[1] // S
    n_cmp = (x_ref.shape[1] - CMP_BLOCK) // S + 1
    top = jnp.zeros((n_half, CMP_HIDDEN), jnp.float32)
    bot = jnp.zeros((n_half, CMP_HIDDEN), jnp.float32)
    for s in range(S):
        xs = x_ref[0, pl.ds(s, n_half, stride=S), :]
        top = top + _bdot(xs + pe_ref[0, s:s + 1, :], w1_ref[0, s * D:(s + 1) * D, :])
        bot = bot + _bdot(xs + pe_ref[0, S + s:S + s + 1, :], w1_ref[0, (S + s) * D:(S + s + 1) * D, :])
    nxt = jnp.concatenate([bot[1:], jnp.zeros((1, CMP_HIDDEN), jnp.float32)], axis=0)
    hid = top + nxt
    hid = hid * jax.nn.sigmoid(hid)
    out = _bdot(hid, w2_ref[0])
    live = lax.broadcasted_iota(jnp.int32, out.shape, 0) < n_cmp
    o_ref[0, 0] = jnp.where(live, out, 0.0).astype(o_ref.dtype)


def compress_blocks(kv, part, pe, w1, w2):
    B, T, _ = kv.shape
    G, D = NSA_KV_GROUPS, NSA_HEAD_DIM
    return pl.pallas_call(
        _compress_kernel,
        grid=(B, G),
        in_specs=[pl.BlockSpec((1, T, D), lambda b, g: (b, 0, part * G + g)),
                  pl.BlockSpec((1, CMP_BLOCK, D), lambda b, g: (part, 0, 0)),
                  pl.BlockSpec((1, CMP_BLOCK * D, CMP_HIDDEN), lambda b, g: (part, 0, 0)),
                  pl.BlockSpec((1, CMP_HIDDEN, D), lambda b, g: (part, 0, 0))],
        out_specs=pl.BlockSpec((1, 1, T // CMP_STRIDE, D), lambda b, g: (b, g, 0, 0)),
        out_shape=jax.ShapeDtypeStruct((B, G, T // CMP_STRIDE, D), jnp.bfloat16),
        compiler_params=pltpu.CompilerParams(
            dimension_semantics=("parallel", "parallel"), vmem_limit_bytes=VMEM_LIMIT),
        name="compress_blocks",
    )(kv, pe.astype(jnp.float32), w1.astype(jnp.float32), w2.astype(jnp.float32))


def nsa_shared_kv(h, kv_w, cmp_pe, cmp_w1, cmp_w2):
    W = NSA_KV_GROUPS * NSA_HEAD_DIM
    kv = dense(h, kv_w)
    k_cmp = compress_blocks(kv, 0, cmp_pe, cmp_w1, cmp_w2)
    v_cmp = compress_blocks(kv, 1, cmp_pe, cmp_w1, cmp_w2)
    k_slc = rope_cast(kv, 2 * W, W, True)
    v_slc = rope_cast(kv, 3 * W, W, False)
    k_win = rope_cast(kv, 4 * W, W, True)
    v_win = rope_cast(kv, 5 * W, W, False)
    return (k_cmp, v_cmp, k_slc, v_slc, k_win, v_win)


def _branch_gate(gate_ref, branch, r):
    row = branch * NSA_HEADS + pl.program_id(1) * NSA_HEADS_PER_GROUP + r
    return jax.nn.sigmoid(gate_ref[0, pl.ds(row, 1), :])


def _cmp_select_kernel(q_ref, kc_ref, vc_ref, ov_ref, gate_ref, o_ref, sel_ref):
    i = pl.program_id(2)
    tq, D, n_c, n_s = ATTN_TILE, NSA_HEAD_DIM, kc_ref.shape[2], T_BLOCKS
    tpos = i * tq + lax.broadcasted_iota(jnp.int32, (n_c, tq), 1)
    block_end = lax.broadcasted_iota(jnp.int32, (n_c, tq), 0) * CMP_STRIDE + (CMP_BLOCK - 1)
    visible = block_end <= tpos
    kc = kc_ref[0, 0]
    vc = vc_ref[0, 0]
    p_sum = jnp.zeros((n_c, tq), jnp.float32)
    for r in range(NSA_HEADS_PER_GROUP):
        s = lax.dot_general(kc, q_ref[0, :, r * D:(r + 1) * D], (((1,), (1,)), ((), ())),
                            preferred_element_type=jnp.float32)
        s = jnp.where(visible, s, NEG_INF)
        e = jnp.exp2(s - jnp.max(s, axis=0, keepdims=True))
        p = jnp.where(visible, e / jnp.sum(e, axis=0, keepdims=True), 0.0)
        o_t = lax.dot_general(vc, p.astype(jnp.bfloat16), (((0,), (0,)), ((), ())),
                              preferred_element_type=jnp.float32)
        o_ref[0, :, r * D:(r + 1) * D] = (o_t * _branch_gate(gate_ref, 0, r)).T
        p_sum = p_sum + p
    p_slc = jnp.dot(ov_ref[...], p_sum.astype(jnp.bfloat16), preferred_element_type=jnp.float32)
    blk = lax.broadcasted_iota(jnp.int32, (n_s, tq), 0)
    cur = (i * tq + lax.broadcasted_iota(jnp.int32, (n_s, tq), 1)) // SLC_BLOCK
    causal_blk = blk <= cur
    forced = (blk == 0) | (causal_blk & (blk > cur - SLC_LOCAL))
    score = jnp.where(causal_blk, jnp.where(forced, FORCE_SCORE, p_slc), -1.0)
    rank = jnp.zeros((n_s, tq), jnp.float32)
    for other in range(n_s):
        row = score[other:other + 1, :]
        ahead = (row > score) | ((row == score) & (blk > other))
        rank = rank + jnp.where(ahead, 1.0, 0.0)
    picked = (rank < float(min(SLC_TOPK, n_s))) & (score >= 0.0)
    sel_ref[0, 0] = jnp.where(picked, 1.0, 0.0).astype(sel_ref.dtype)


def nsa_compressed_select(q_cmp, k_cmp, v_cmp, gate_logits_t):
    B, T, _ = q_cmp.shape
    G, R, D, tq = NSA_KV_GROUPS, NSA_HEADS_PER_GROUP, NSA_HEAD_DIM, ATTN_TILE
    n_c = k_cmp.shape[2]
    n_s = T // SLC_BLOCK
    c0 = np.arange(n_c) * CMP_STRIDE
    s0 = np.arange(n_s) * SLC_BLOCK
    ov = np.clip(np.minimum(c0[None, :] + CMP_BLOCK, s0[:, None] + SLC_BLOCK)
                 - np.maximum(c0[None, :], s0[:, None]), 0, None) / CMP_BLOCK
    cmp_spec = pl.BlockSpec((1, 1, n_c, D), lambda b, g, i: (b, g, 0, 0))
    return pl.pallas_call(
        _cmp_select_kernel,
        grid=(B, G, T // tq),
        in_specs=[pl.BlockSpec((1, tq, R * D), lambda b, g, i: (b, i, g)), cmp_spec, cmp_spec,
                  pl.BlockSpec((n_s, n_c), lambda b, g, i: (0, 0)),
                  pl.BlockSpec((1, 3 * NSA_HEADS, tq), lambda b, g, i: (b, 0, i))],
        out_specs=[pl.BlockSpec((1, tq, R * D), lambda b, g, i: (b, i, g)),
                   pl.BlockSpec((1, 1, n_s, tq), lambda b, g, i: (b, g, 0, i))],
        out_shape=[jax.ShapeDtypeStruct((B, T, G * R * D), jnp.float32),
                   jax.ShapeDtypeStruct((B, G, n_s, T), jnp.bfloat16)],
        compiler_params=pltpu.CompilerParams(
            dimension_semantics=("parallel", "parallel", "parallel"), vmem_limit_bytes=VMEM_LIMIT),
        name="compressed_attention_select",
    )(q_cmp, k_cmp, v_cmp, jnp.asarray(ov, jnp.bfloat16), gate_logits_t)


def _attn_kernel(*refs, windowed):
    if windowed:
        q_ref, k_ref, v_ref, gate_ref, prev_ref, o_ref, m_s, l_s, acc_s = refs
    else:
        q_ref, k_ref, v_ref, sel_ref, gate_ref, prev_ref, o_ref, m_s, l_s, acc_s = refs
    i = pl.program_id(2)
    tq, tk, D = ATTN_TILE, ATTN_TK, NSA_HEAD_DIM
    j_hi = (i + 1) * (tq // tk) - 1
    m_s[...] = jnp.full(m_s.shape, NEG_INF, jnp.float32)
    l_s[...] = jnp.zeros(l_s.shape, jnp.float32)
    acc_s[...] = jnp.zeros(acc_s.shape, jnp.float32)
    tpos = i * tq + lax.broadcasted_iota(jnp.int32, (tk, tq), 1)

    def body(jj, carry):
        j = j_hi - jj
        start = pl.multiple_of(j * tk, tk)
        kblk = k_ref[0, pl.ds(start, tk), :]
        vblk = v_ref[0, pl.ds(start, tk), :]
        kpos = j * tk + lax.broadcasted_iota(jnp.int32, (tk, tq), 0)
        ok = kpos <= tpos
        if windowed:
            ok = ok & (kpos > tpos - WINDOW)
        else:
            blk_of_key = j * (tk // SLC_BLOCK) + lax.broadcasted_iota(
                jnp.int32, (tk, T_BLOCKS), 0) // SLC_BLOCK
            expand = (lax.broadcasted_iota(jnp.int32, (tk, T_BLOCKS), 1) == blk_of_key)
            selm = jnp.dot(expand.astype(jnp.bfloat16), sel_ref[0, 0],
                           preferred_element_type=jnp.float32)
            ok = ok & (selm > 0.5)
        def scores(r):
            return lax.dot_general(kblk, q_ref[0, :, r * D:(r + 1) * D], (((1,), (1,)), ((), ())),
                                   preferred_element_type=jnp.float32)

        s_next = scores(0)
        for r in range(NSA_HEADS_PER_GROUP):
            s = jnp.where(ok, s_next, NEG_INF)
            if r + 1 < NSA_HEADS_PER_GROUP:
                s_next = scores(r + 1)
            m_prev = m_s[r]
            m_new = jnp.maximum(m_prev, jnp.max(s, axis=0, keepdims=True))
            alpha = jnp.exp2(m_prev - m_new)
            p = jnp.exp2(s - m_new)
            l_s[r] = alpha * l_s[r] + jnp.sum(p, axis=0, keepdims=True)
            acc_s[r] = alpha * acc_s[r] + lax.dot_general(
                vblk, p.astype(jnp.bfloat16), (((0,), (0,)), ((), ())),
                preferred_element_type=jnp.float32)
            m_s[r] = m_new
        return carry

    j_lo = jnp.maximum(i * tq - WINDOW + 1, 0) // tk if windowed else 0
    lax.fori_loop(0, j_hi - j_lo + 1, body, 0)
    branch = 2 if windowed else 1
    for r in range(NSA_HEADS_PER_GROUP):
        gated = (acc_s[r] / l_s[r] * _branch_gate(gate_ref, branch, r)).T
        o_ref[0, :, r * D:(r + 1) * D] = (prev_ref[0, :, r * D:(r + 1) * D] + gated).astype(o_ref.dtype)


def masked_attention(qs, kb, vb, sel, gate_logits_t, prev, out_dtype):
    B, T, _ = qs.shape
    G, R, D = NSA_KV_GROUPS, NSA_HEADS_PER_GROUP, NSA_HEAD_DIM
    tq = ATTN_TILE
    windowed = sel is None
    q_spec = pl.BlockSpec((1, tq, R * D), lambda b, g, i: (b, i, g))
    in_specs = [q_spec,
                pl.BlockSpec((1, T, D), lambda b, g, i: (b, 0, g)),
                pl.BlockSpec((1, T, D), lambda b, g, i: (b, 0, g))]
    args = [qs, kb, vb]
    if not windowed:
        in_specs.append(pl.BlockSpec((1, 1, T // SLC_BLOCK, tq), lambda b, g, i: (b, g, 0, i)))
        args.append(sel)
    in_specs += [pl.BlockSpec((1, 3 * NSA_HEADS, tq), lambda b, g, i: (b, 0, i)), q_spec]
    args += [gate_logits_t, prev]
    return pl.pallas_call(
        functools.partial(_attn_kernel, windowed=windowed),
        grid=(B, G, T // tq),
        in_specs=in_specs,
        out_specs=pl.BlockSpec((1, tq, R * D), lambda b, g, i: (b, i, g)),
        out_shape=jax.ShapeDtypeStruct((B, T, G * R * D), out_dtype),
        scratch_shapes=[pltpu.VMEM((R, 1, tq), jnp.float32),
                        pltpu.VMEM((R, 1, tq), jnp.float32),
                        pltpu.VMEM((R, D, tq), jnp.float32)],
        compiler_params=pltpu.CompilerParams(
            dimension_semantics=("parallel", "parallel", "arbitrary"), vmem_limit_bytes=VMEM_LIMIT),
        name="window_attention" if windowed else "selected_attention",
    )(*args)


def nsa_mixer(x, w_q, w_out, k_cmp, v_cmp, k_slc, v_slc, k_win, v_win):
    B, T, _ = x.shape
    H, D = NSA_HEADS, NSA_HEAD_DIM
    q = dense(x, w_q, 0, NSA_Q_WIDTH)
    gate_t = jnp.swapaxes(dense(x, w_q, NSA_Q_WIDTH), 1, 2)
    q_scale = D ** -0.5 * math.log2(math.e)
    q_cmp = rope_cast(q, 0, NSA_Q_WIDTH, False, q_scale)
    q_rot = rope_cast(q, 0, NSA_Q_WIDTH, True, q_scale)
    o, sel = nsa_compressed_select(q_cmp, k_cmp, v_cmp, gate_t)
    o = masked_attention(q_rot, k_slc, v_slc, sel, gate_t, o, jnp.float32)
    o = masked_attention(q_rot, k_win, v_win, None, gate_t, o, jnp.bfloat16)
    return dense(o, w_out)


def kernel(x, a_w_in, a_conv_w, a_a_log, a_dt_bias, a_norm_w, a_w_out, kv_w, cmp_pe, cmp_w1, cmp_w2,
           b_w_q, b_w_out, router_w, router_bias, moe_w_gate, moe_w_up, moe_w_down, ln_g, ln_b):
    B, T, D = x.shape
    xf = x.astype(jnp.float32).reshape(B * T, D)
    x16 = xf.astype(jnp.bfloat16)
    shared_kv = None
    for layer in range(DEPTH):
        xin = x16.reshape(B, T, D)
        if layer < N_A_LAYERS:
            mix = gdn_mixer(xin, a_w_in[layer], a_conv_w[layer], a_a_log[layer], a_dt_bias[layer],
                            a_norm_w[layer], a_w_out[layer])
        else:
            if shared_kv is None:
                shared_kv = nsa_shared_kv(xin, kv_w, cmp_pe, cmp_w1, cmp_w2)
            j = layer - N_A_LAYERS
            mix = nsa_mixer(xin, b_w_q[j], b_w_out[j], *shared_kv)
        xf, x16, logits = residual_layer_norm(xf, mix.reshape(B * T, D), ln_g[layer, 0], ln_b[layer, 0],
                                              router_w)
        ffn = moe_ffn(xf, logits, router_bias, moe_w_gate, moe_w_up, moe_w_down, layer)
        xf, x16 = residual_layer_norm(xf, ffn, ln_g[layer, 1], ln_b[layer, 1], n_mix=TOP_K)
    return xf.reshape(B, T, D).astype(x.dtype)
```

```python
import functools
import math

import jax
import jax.numpy as jnp
import numpy as np
from jax import lax
from jax.experimental import pallas as pl
from jax.experimental.pallas import tpu as pltpu

D_MODEL = 2048
BATCH = 2
SEQ = 4096
DEPTH = 2
N_A_LAYERS = DEPTH // 2

GDN_HEAD_DIM = 128
GDN_QK_HEADS = D_MODEL // GDN_HEAD_DIM
GDN_V_HEADS = 2 * GDN_QK_HEADS
GDN_QK_WIDTH = GDN_QK_HEADS * GDN_HEAD_DIM
GDN_V_WIDTH = GDN_V_HEADS * GDN_HEAD_DIM
GDN_CONV_CH = 2 * GDN_QK_WIDTH + GDN_V_WIDTH
GDN_CHUNK = 64
GDN_CONV = 4
GDN_TILE = 256
GDN_HK_STEP = 4

NSA_HEAD_DIM = 128
NSA_HEADS = D_MODEL // NSA_HEAD_DIM
NSA_KV_GROUPS = 4
NSA_HEADS_PER_GROUP = NSA_HEADS // NSA_KV_GROUPS
NSA_Q_WIDTH = NSA_HEADS * NSA_HEAD_DIM
CMP_BLOCK = 32
CMP_STRIDE = 16
CMP_HIDDEN = 512
SLC_BLOCK = 64
SLC_TOPK = 16
SLC_LOCAL = 2
WINDOW = 512
WIN_Q_BLOCK = 128
SLC_Q_BLOCK = 64
ROPE_THETA = 10000.0

N_EXPERTS = 32
N_GROUPS = 8
EXPERTS_PER_GROUP = N_EXPERTS // N_GROUPS
TOP_K = 2
D_EXPERT = D_MODEL // 4

DEEPNORM_ALPHA = (2 * DEPTH) ** 0.25
LN_EPS = 1e-5
RMS_EPS = 1e-6
NEG_INF = -1e30
FORCE_SCORE = 1e6

LANE = 128
VMEM_LIMIT = 48 * 1024 * 1024
MOE_VMEM_LIMIT = 56 * 1024 * 1024
MOE_TILE = 256
ATTN_TILE = 256
ATTN_TK = 256
T_BLOCKS = SEQ // SLC_BLOCK


def _mm_kernel(a_ref, b_ref, o_ref, b16_s):
    @pl.when(pl.program_id(1) == 0)
    def _():
        b16_s[...] = b_ref[...].astype(jnp.bfloat16)

    o_ref[...] = jnp.dot(a_ref[...], b16_s[...], preferred_element_type=jnp.float32)


def pmatmul(a, b, col_start=0, n_cols=None, row_block=0, tm=512):
    m, k = a.shape
    n_cols = b.shape[1] - col_start if n_cols is None else n_cols
    if n_cols % LANE or col_start % LANE:
        b = jnp.pad(b[:, col_start:col_start + n_cols], ((0, 0), (0, -n_cols % LANE)))
        return pmatmul(a, b)[:, :n_cols]
    tn = math.gcd(n_cols, 1024 if k <= 2048 else 512)
    assert col_start % tn == 0 and m % tm == 0 and b.shape[0] % k == 0
    off = col_start // tn
    return pl.pallas_call(
        _mm_kernel,
        grid=(n_cols // tn, m // tm),
        in_specs=[pl.BlockSpec((tm, k), lambda j, i: (i, 0)),
                  pl.BlockSpec((k, tn), lambda j, i: (row_block, j + off))],
        out_specs=pl.BlockSpec((tm, tn), lambda j, i: (i, j)),
        out_shape=jax.ShapeDtypeStruct((m, n_cols), jnp.float32),
        scratch_shapes=[pltpu.VMEM((k, tn), jnp.bfloat16)],
        compiler_params=pltpu.CompilerParams(
            dimension_semantics=("parallel", "arbitrary"), vmem_limit_bytes=VMEM_LIMIT),
        name="dense_matmul",
    )(a.astype(jnp.bfloat16), b.astype(jnp.float32))


def dense(x, w, col_start=0, n_cols=None):
    lead = x.shape[:-1]
    out = pmatmul(x.reshape(-1, x.shape[-1]), w, col_start, n_cols)
    return out.reshape(lead + (out.shape[-1],))


def _moe_kernel(tile_expert_ref, n_tiles_ref, tok_ref, tok_next_ref, dst_prev_ref, x_hbm, gate_ref,
                wg_ref, wu_ref, wd_ref, out_hbm, xbuf, ybuf, wg16, wu16, wd16, sem_in, sem_out):
    i = pl.program_id(0)
    n_tiles = n_tiles_ref[0]
    slot = i % 2
    spare = out_hbm.shape[0] - MOE_TILE

    def gather_row(idx_ref, s, r):
        return pltpu.make_async_copy(x_hbm.at[pl.ds(idx_ref[0, 0, r], 1)], xbuf.at[s, pl.ds(r, 1)],
                                     sem_in.at[s])

    def scatter_row(s, r, dst_row):
        return pltpu.make_async_copy(ybuf.at[s, pl.ds(r, 1)], out_hbm.at[pl.ds(dst_row, 1)], sem_out)

    def wait_gather(s):
        pltpu.make_async_copy(x_hbm.at[pl.ds(0, MOE_TILE)], xbuf.at[s], sem_in.at[s]).wait()

    def wait_scatter():
        pltpu.make_async_copy(ybuf.at[0], out_hbm.at[pl.ds(0, MOE_TILE)], sem_out).wait()

    @pl.when(i == 0)
    def _():
        ybuf[...] = jnp.zeros(ybuf.shape, jnp.float32)

        def first_rows(r, c):
            gather_row(tok_ref, 0, r).start()
            scatter_row(0, r, spare + r).start()
            return c
        lax.fori_loop(0, MOE_TILE, first_rows, 0, unroll=8)

    @pl.when(i < n_tiles)
    def _():
        @pl.when(jnp.logical_or(i == 0, tile_expert_ref[i] != tile_expert_ref[jnp.maximum(i - 1, 0)]))
        def _():
            wg16[...] = wg_ref[0, 0].astype(jnp.bfloat16)
            wu16[...] = wu_ref[0, 0].astype(jnp.bfloat16)
            wd16[...] = wd_ref[0, 0].astype(jnp.bfloat16)

        wait_gather(slot)
        wait_scatter()
        x = xbuf[slot].astype(jnp.bfloat16)
        for r in range(MOE_TILE):
            gather_row(tok_next_ref, 1 - slot, r).start()
            scatter_row(1 - slot, r, dst_prev_ref[0, 0, r]).start()
        g = jnp.dot(x, wg16[...], preferred_element_type=jnp.float32)
        u = jnp.dot(x, wu16[...], preferred_element_type=jnp.float32)
        h = (g * jax.nn.sigmoid(g)) * u * gate_ref[...]
        ybuf[slot] = jnp.dot(h.astype(jnp.bfloat16), wd16[...], preferred_element_type=jnp.float32)

    @pl.when(i == n_tiles)
    def _():
        wait_scatter()

        def last_rows(r, c):
            scatter_row(1 - slot, r, dst_prev_ref[0, 0, r]).start()
            return c
        lax.fori_loop(0, MOE_TILE, last_rows, 0, unroll=8)
        wait_scatter()
        wait_gather(slot)


def moe_ffn(h, router_logits, router_bias, w_gate, w_up, w_down, layer):
    n_tok, D = h.shape
    aff = jax.nn.sigmoid(router_logits.astype(jnp.float32))
    biased = (aff + router_bias.astype(jnp.float32)).reshape(-1, N_GROUPS, EXPERTS_PER_GROUP)

    def top2(v):
        i1 = jnp.argmax(v, axis=-1)
        rest = jnp.where(jnp.arange(v.shape[-1]) == i1[..., None], -jnp.inf, v)
        i2 = jnp.argmax(rest, axis=-1)
        return jnp.max(v, axis=-1), jnp.max(rest, axis=-1), i1, i2

    g1, g2, _, _ = top2(biased)
    best_group = jnp.argmax(g1 + g2, axis=-1)
    cand = jnp.take_along_axis(biased, best_group[:, None, None], axis=1)[:, 0]
    _, _, i1, i2 = top2(cand)
    top_idx = best_group[:, None] * EXPERTS_PER_GROUP + jnp.stack([i1, i2], axis=-1)
    top_aff = jnp.take_along_axis(aff, top_idx, axis=-1)
    top_w = top_aff / jnp.sum(top_aff, axis=-1, keepdims=True)

    n_asg = n_tok * TOP_K
    max_tiles = n_asg // MOE_TILE + N_EXPERTS + 1
    n_rows = max_tiles * MOE_TILE
    e_flat = top_idx.T.reshape(-1).astype(jnp.int32)
    hot = (e_flat[:, None] == jnp.arange(N_EXPERTS, dtype=jnp.int32)[None, :]).astype(jnp.int32)
    running = jnp.cumsum(hot, axis=0)
    rank = jnp.sum(hot * (running - 1), axis=1)
    counts = running[-1]
    tiles_per = (counts + MOE_TILE - 1) // MOE_TILE
    tile_end = jnp.cumsum(tiles_per)
    row_start = (tile_end - tiles_per) * MOE_TILE
    row_of_asg = jnp.sum(hot * row_start[None, :], axis=1) + rank
    asg_of_row = jnp.full((n_rows,), -1, jnp.int32).at[row_of_asg].set(jnp.arange(n_asg, dtype=jnp.int32))
    token_of_row = jnp.maximum(asg_of_row, 0) % n_tok
    gate_of_row = jnp.where(asg_of_row >= 0, top_w.T.reshape(-1)[jnp.maximum(asg_of_row, 0)], 0.0)
    spare_rows = n_asg + jnp.arange(MOE_TILE, dtype=jnp.int32)
    dst3 = jnp.where(asg_of_row >= 0, asg_of_row, jnp.tile(spare_rows, max_tiles)).reshape(max_tiles, 1, MOE_TILE)
    dst_prev3 = jnp.concatenate([spare_rows.reshape(1, 1, MOE_TILE), dst3[:-1]], axis=0)
    n_tiles = tile_end[-1:].astype(jnp.int32)
    tile_expert = jnp.minimum(
        jnp.searchsorted(tile_end, jnp.arange(max_tiles, dtype=jnp.int32), side="right"),
        N_EXPERTS - 1).astype(jnp.int32)
    tile_expert = jnp.where(jnp.arange(max_tiles) < n_tiles[0], tile_expert,
                            tile_expert[jnp.maximum(n_tiles[0] - 1, 0)])

    wspec_in = pl.BlockSpec((1, 1, D, D_EXPERT), lambda i, te, nt: (layer, te[i], 0, 0))
    idx_spec = pl.BlockSpec((1, 1, MOE_TILE), lambda i, te, nt: (i, 0, 0), memory_space=pltpu.SMEM)
    next_spec = pl.BlockSpec((1, 1, MOE_TILE), lambda i, te, nt: (jnp.minimum(i + 1, max_tiles - 1), 0, 0),
                             memory_space=pltpu.SMEM)
    tok3 = token_of_row.reshape(max_tiles, 1, MOE_TILE)
    y = pl.pallas_call(
        _moe_kernel,
        grid_spec=pltpu.PrefetchScalarGridSpec(
            num_scalar_prefetch=2,
            grid=(max_tiles,),
            in_specs=[idx_spec, next_spec, idx_spec,
                      pl.BlockSpec(memory_space=pl.ANY),
                      pl.BlockSpec((MOE_TILE, 1), lambda i, te, nt: (i, 0)),
                      wspec_in, wspec_in,
                      pl.BlockSpec((1, 1, D_EXPERT, D), lambda i, te, nt: (layer, te[i], 0, 0))],
            out_specs=pl.BlockSpec(memory_space=pl.ANY),
            scratch_shapes=[pltpu.VMEM((2, MOE_TILE, D), jnp.float32),
                            pltpu.VMEM((2, MOE_TILE, D), jnp.float32),
                            pltpu.VMEM((D, D_EXPERT), jnp.bfloat16),
                            pltpu.VMEM((D, D_EXPERT), jnp.bfloat16),
                            pltpu.VMEM((D_EXPERT, D), jnp.bfloat16),
                            pltpu.SemaphoreType.DMA((2,)), pltpu.SemaphoreType.DMA(())],
        ),
        out_shape=jax.ShapeDtypeStruct((n_asg + MOE_TILE, D), jnp.float32),
        compiler_params=pltpu.CompilerParams(
            dimension_semantics=("arbitrary",), vmem_limit_bytes=MOE_VMEM_LIMIT),
        name="routed_moe",
    )(tile_expert, n_tiles, tok3, tok3, dst_prev3, h, gate_of_row[:, None], w_gate, w_up, w_down)
    return y


def _residual_ln_kernel(x_ref, *rest, n_mix, with_router):
    mix_refs, rest = rest[:n_mix], rest[n_mix:]
    if with_router:
        g_ref, b_ref, rw_ref, o_ref, o16_ref, logit_ref = rest
    else:
        g_ref, b_ref, o_ref, o16_ref = rest
    h = DEEPNORM_ALPHA * x_ref[...]
    for mix_ref in mix_refs:
        h = h + mix_ref[...]
    mu = jnp.mean(h, axis=-1, keepdims=True)
    var = jnp.mean(jnp.square(h - mu), axis=-1, keepdims=True)
    y = (h - mu) * lax.rsqrt(var + LN_EPS) * g_ref[...] + b_ref[...]
    o_ref[...] = y
    y16 = y.astype(jnp.bfloat16)
    o16_ref[...] = y16
    if with_router:
        logit_ref[...] = jnp.dot(y16, rw_ref[...].astype(jnp.bfloat16), preferred_element_type=jnp.float32)


def residual_layer_norm(x, mix, g, b, router_w=None, n_mix=1):
    n, d = x.shape
    tm = 256
    row = lambda i: (i, 0)
    fixed = lambda i: (0, 0)
    in_specs = ([pl.BlockSpec((tm, d), row)]
                + [pl.BlockSpec((tm, d), functools.partial(lambda i, s: (i + s * (n // tm), 0), s=s))
                   for s in range(n_mix)]
                + [pl.BlockSpec((1, d), fixed), pl.BlockSpec((1, d), fixed)])
    args = [x] + [mix] * n_mix + [g.astype(jnp.float32).reshape(1, d), b.astype(jnp.float32).reshape(1, d)]
    out_specs = [pl.BlockSpec((tm, d), row), pl.BlockSpec((tm, d), row)]
    out_shape = [jax.ShapeDtypeStruct((n, d), jnp.float32), jax.ShapeDtypeStruct((n, d), jnp.bfloat16)]
    if router_w is not None:
        e_pad = -router_w.shape[1] % LANE
        rw = jnp.pad(router_w.astype(jnp.float32), ((0, 0), (0, e_pad)))
        in_specs.append(pl.BlockSpec(rw.shape, fixed))
        args.append(rw)
        out_specs.append(pl.BlockSpec((tm, rw.shape[1]), row))
        out_shape.append(jax.ShapeDtypeStruct((n, rw.shape[1]), jnp.float32))
    outs = pl.pallas_call(
        functools.partial(_residual_ln_kernel, n_mix=n_mix, with_router=router_w is not None),
        grid=(n // tm,),
        in_specs=in_specs, out_specs=out_specs, out_shape=out_shape,
        compiler_params=pltpu.CompilerParams(
            dimension_semantics=("parallel",), vmem_limit_bytes=VMEM_LIMIT),
        name="residual_layer_norm",
    )(*args)
    if router_w is not None:
        return outs[0], outs[1], outs[2][:, :router_w.shape[1]]
    return outs[0], outs[1]


def _softplus(x):
    return jnp.maximum(x, 0.0) + jnp.log1p(jnp.exp(-jnp.abs(x)))


def _bdot(a, b):
    return jnp.dot(a.astype(jnp.bfloat16), b.astype(jnp.bfloat16), preferred_element_type=jnp.float32)


def _bdot_nt(a, b):
    return lax.dot_general(a.astype(jnp.bfloat16), b.astype(jnp.bfloat16), (((1,), (1,)), ((), ())),
                           preferred_element_type=jnp.float32)


def _bdot_tn(a, b):
    return lax.dot_general(a.astype(jnp.bfloat16), b.astype(jnp.bfloat16), (((0,), (0,)), ((), ())),
                           preferred_element_type=jnp.float32)


def _gdn_prep_kernel(x_ref, halo_ref, w_ref, o_ref):
    i = pl.program_id(1)
    j = pl.program_id(2)
    tt = x_ref.shape[1]
    x = x_ref[0]
    halo = jnp.where(i == 0, 0.0, halo_ref[0])
    xx = jnp.concatenate([halo, x], axis=0)
    w = w_ref[...]
    y = w[3:4] * x
    for tap in range(GDN_CONV - 1):
        lo = 8 - (GDN_CONV - 1) + tap
        y = y + w[tap:tap + 1] * xx[lo:lo + tt]
    y = y * jax.nn.sigmoid(y)
    q_blocks = GDN_QK_WIDTH // x_ref.shape[2]

    @pl.when(j >= 2 * q_blocks)
    def _():
        o_ref[0] = y

    @pl.when(j < 2 * q_blocks)
    def _():
        scale = jnp.where(j < q_blocks, GDN_HEAD_DIM ** -0.5, 1.0)
        for h in range(x_ref.shape[2] // GDN_HEAD_DIM):
            yh = y[:, h * GDN_HEAD_DIM:(h + 1) * GDN_HEAD_DIM]
            inv = lax.rsqrt(jnp.sum(yh * yh, axis=-1, keepdims=True) + RMS_EPS)
            o_ref[0, :, h * GDN_HEAD_DIM:(h + 1) * GDN_HEAD_DIM] = yh * inv * scale


def gdn_prep(qkv, conv_w):
    B, T, CH = qkv.shape
    tt, tc = GDN_TILE, 512
    return pl.pallas_call(
        _gdn_prep_kernel,
        grid=(B, T // tt, CH // tc),
        in_specs=[pl.BlockSpec((1, tt, tc), lambda b, i, j: (b, i, j)),
                  pl.BlockSpec((1, 8, tc), lambda b, i, j: (b, jnp.maximum(i * (tt // 8) - 1, 0), j)),
                  pl.BlockSpec((GDN_CONV, tc), lambda b, i, j: (0, j))],
        out_specs=pl.BlockSpec((1, tt, tc), lambda b, i, j: (b, i, j)),
        out_shape=jax.ShapeDtypeStruct((B, T, CH), jnp.float32),
        compiler_params=pltpu.CompilerParams(
            dimension_semantics=("parallel", "parallel", "parallel"), vmem_limit_bytes=VMEM_LIMIT),
        name="gdn_conv_silu_norm",
    )(qkv, qkv, conv_w)


def _gdn_kernel(q_ref, k_ref, v_ref, z_ref, acol_ref, arow_ref, bcol_ref, alog_ref, dt_ref, nw_ref,
                o_ref, state_s):
    C, Dh = GDN_CHUNK, GDN_HEAD_DIM

    @pl.when(pl.program_id(2) == 0)
    def _():
        state_s[...] = jnp.zeros(state_s.shape, jnp.float32)

    row = lax.broadcasted_iota(jnp.int32, (C, C), 0)
    col = lax.broadcasted_iota(jnp.int32, (C, C), 1)
    causal = row >= col
    strict = row > col
    nw = nw_ref[...]
    probs = []
    for n, kh in [(n, kh) for n in range(GDN_TILE // C) for kh in range(GDN_HK_STEP)]:
        sl = slice(n * C, (n + 1) * C)
        q = q_ref[0, sl, kh * Dh:(kh + 1) * Dh]
        k = k_ref[0, sl, kh * Dh:(kh + 1) * Dh]
        kk = _bdot_nt(k, k)
        qk = _bdot_nt(q, k)
        for hh in range(2):
            vh = 2 * kh + hh
            head = pl.program_id(1) * (2 * GDN_HK_STEP) + vh
            is_head = lax.broadcasted_iota(jnp.int32, (1, GDN_V_HEADS), 1) == head

            def pick(a):
                return jnp.sum(jnp.where(is_head, a, 0.0), axis=1, keepdims=True)

            neg_a = -jnp.exp(pick(alog_ref[...]))
            dt = pick(dt_ref[...])
            g_col = neg_a * _softplus(pick(acol_ref[0, sl, :]) + dt)
            g_row = neg_a * _softplus(arow_ref[0, pl.ds(head, 1), :][:, sl] + dt)
            beta = jax.nn.sigmoid(pick(bcol_ref[0, sl, :]))
            gc_col = jnp.sum(jnp.where(causal, g_row, 0.0), axis=1, keepdims=True)
            gc_row = jnp.sum(jnp.where(row <= col, g_col, 0.0), axis=0, keepdims=True)
            g_last = jnp.sum(g_row, axis=1, keepdims=True)
            decay = jnp.where(causal, jnp.exp(jnp.where(causal, gc_col - gc_row, 0.0)), 0.0)
            m = jnp.where(strict, beta * kk * decay, 0.0)
            e_col = jnp.exp(gc_col)
            v = v_ref[0, sl, vh * Dh:(vh + 1) * Dh]
            probs.append(dict(
                sl=sl, hh=vh, x=-m, p=m, a=qk * decay,
                rhs=jnp.concatenate([beta * v, beta * e_col * k], axis=1),
                qe=q * e_col, ke=k * jnp.exp(g_last - gc_col), s_decay=jnp.exp(g_last)))
    for _ in range(int(math.log2(C)) - 1):
        for pr in probs:
            pr["p"] = _bdot(pr["p"], pr["p"])
        for pr in probs:
            pr["x"] = pr["x"] + pr["p"] + _bdot(pr["x"], pr["p"])
    for pr in probs:
        pr["sol"] = pr["rhs"] + _bdot(pr["x"], pr["rhs"])
    for pr in probs:
        sl, hh = pr["sl"], pr["hh"]
        u, w = pr["sol"][:, :Dh], pr["sol"][:, Dh:]
        state = state_s[hh]
        ws = _bdot(jnp.concatenate([w, pr["qe"]], axis=0), state)
        v_new = u - ws[:C]
        o = ws[C:] + _bdot(pr["a"], v_new)
        state_s[hh] = state * pr["s_decay"] + _bdot_tn(pr["ke"], v_new)
        z = z_ref[0, sl, hh * Dh:(hh + 1) * Dh]
        o = o * lax.rsqrt(jnp.mean(o * o, axis=-1, keepdims=True) + RMS_EPS) * nw
        o_ref[0, sl, hh * Dh:(hh + 1) * Dh] = (o * (z * jax.nn.sigmoid(z))).astype(o_ref.dtype)


def gdn_core(qkv, z, a_raw, b_raw, a_log, dt_bias, norm_w):
    B, T, _ = qkv.shape
    Hk, Dh, tt = GDN_QK_HEADS, GDN_HEAD_DIM, GDN_TILE
    Hv = GDN_V_HEADS
    hs = GDN_HK_STEP
    qw, vw = hs * Dh, 2 * hs * Dh
    k_off, v_off = GDN_QK_WIDTH // qw, 2 * GDN_QK_WIDTH // vw
    col_spec = pl.BlockSpec((1, tt, Hv), lambda b, h, c: (b, c, 0))
    head_spec = pl.BlockSpec((1, Hv), lambda b, h, c: (0, 0))
    return pl.pallas_call(
        _gdn_kernel,
        grid=(B, Hk // hs, T // tt),
        in_specs=[pl.BlockSpec((1, tt, qw), lambda b, h, c: (b, c, h)),
                  pl.BlockSpec((1, tt, qw), lambda b, h, c: (b, c, k_off + h)),
                  pl.BlockSpec((1, tt, vw), lambda b, h, c: (b, c, v_off + h)),
                  pl.BlockSpec((1, tt, vw), lambda b, h, c: (b, c, h)),
                  col_spec,
                  pl.BlockSpec((1, Hv, tt), lambda b, h, c: (b, 0, c)),
                  col_spec, head_spec, head_spec,
                  pl.BlockSpec((1, Dh), lambda b, h, c: (0, 0))],
        out_specs=pl.BlockSpec((1, tt, vw), lambda b, h, c: (b, c, h)),
        out_shape=jax.ShapeDtypeStruct((B, T, GDN_V_WIDTH), jnp.bfloat16),
        scratch_shapes=[pltpu.VMEM((2 * hs, Dh, Dh), jnp.float32)],
        compiler_params=pltpu.CompilerParams(
            dimension_semantics=("parallel", "parallel", "arbitrary"), vmem_limit_bytes=VMEM_LIMIT),
        name="gated_delta_rule",
    )(qkv, qkv, qkv, z, a_raw, jnp.swapaxes(a_raw, 1, 2), b_raw,
      a_log.astype(jnp.float32).reshape(1, Hv), dt_bias.astype(jnp.float32).reshape(1, Hv),
      norm_w.astype(jnp.float32).reshape(1, Dh))


def gdn_mixer(x, w_in, conv_w, a_log, dt_bias, norm_w, w_out):
    B, T, _ = x.shape
    Hk, Hv, Dh = GDN_QK_HEADS, GDN_V_HEADS, GDN_HEAD_DIM
    qkv = dense(x, w_in, 0, GDN_CONV_CH)
    z = dense(x, w_in, GDN_CONV_CH, GDN_V_WIDTH)
    a_raw, b_raw = jnp.split(dense(x, w_in, GDN_CONV_CH + GDN_V_WIDTH), [Hv], axis=-1)
    o = gdn_core(gdn_prep(qkv, conv_w), z, a_raw, b_raw, a_log, dt_bias, norm_w)
    return dense(o, w_out)


def _rope_cast_kernel(x_ref, cos_ref, sin_ref, o_ref, *, rotate, scale):
    D = NSA_HEAD_DIM
    for h in range(x_ref.shape[2] // D):
        x = x_ref[0, :, h * D:(h + 1) * D]
        if rotate:
            x = x * cos_ref[...] + pltpu.roll(x, D // 2, axis=1) * sin_ref[...]
        if scale != 1.0:
            x = x * scale
        o_ref[0, :, h * D:(h + 1) * D] = x.astype(o_ref.dtype)


def rope_cast(x, col_start, n_cols, rotate, scale=1.0):
    B, T, _ = x.shape
    tt, tc, D = 512, 512, NSA_HEAD_DIM
    half = D // 2
    inv_freq = ROPE_THETA ** (-jnp.arange(half, dtype=jnp.float32) / half)
    ang = jnp.arange(T, dtype=jnp.float32)[:, None] * inv_freq[None, :]
    cos2 = jnp.concatenate([jnp.cos(ang), jnp.cos(ang)], axis=1)
    sin2 = jnp.concatenate([-jnp.sin(ang), jnp.sin(ang)], axis=1)
    off = col_start // tc
    return pl.pallas_call(
        functools.partial(_rope_cast_kernel, rotate=rotate, scale=scale),
        grid=(B, T // tt, n_cols // tc),
        in_specs=[pl.BlockSpec((1, tt, tc), lambda b, i, j: (b, i, off + j)),
                  pl.BlockSpec((tt, D), lambda b, i, j: (i, 0)),
                  pl.BlockSpec((tt, D), lambda b, i, j: (i, 0))],
        out_specs=pl.BlockSpec((1, tt, tc), lambda b, i, j: (b, i, j)),
        out_shape=jax.ShapeDtypeStruct((B, T, n_cols), jnp.bfloat16),
        compiler_params=pltpu.CompilerParams(
            dimension_semantics=("parallel", "parallel", "parallel"), vmem_limit_bytes=VMEM_LIMIT),
        name="rope_cast",
    )(x, cos2, sin2)


def _compress_kernel(x_ref, pe_ref, w1_ref, w2_ref, o_ref):
    S, D = CMP_STRIDE, NSA_HEAD_DIM
    n_half = x_ref.shape[1] // S
    n_cmp = (x_ref.shape[1] - CMP_BLOCK) // S + 1
    top = jnp.zeros((n_half, CMP_HIDDEN), jnp.float32)
    bot = jnp.zeros((n_half, CMP_HIDDEN), jnp.float32)
    for s in range(S):
        xs = x_ref[0, pl.ds(s, n_half, stride=S), :]
        top = top + _bdot(xs + pe_ref[0, s:s + 1, :], w1_ref[0, s * D:(s + 1) * D, :])
        bot = bot + _bdot(xs + pe_ref[0, S + s:S + s + 1, :], w1_ref[0, (S + s) * D:(S + s + 1) * D, :])
    nxt = jnp.concatenate([bot[1:], jnp.zeros((1, CMP_HIDDEN), jnp.float32)], axis=0)
    hid = top + nxt
    hid = hid * jax.nn.sigmoid(hid)
    out = _bdot(hid, w2_ref[0])
    live = lax.broadcasted_iota(jnp.int32, out.shape, 0) < n_cmp
    o_ref[0, 0] = jnp.where(live, out, 0.0).astype(o_ref.dtype)


def compress_blocks(kv, part, pe, w1, w2):
    B, T, _ = kv.shape
    G, D = NSA_KV_GROUPS, NSA_HEAD_DIM
    return pl.pallas_call(
        _compress_kernel,
        grid=(B, G),
        in_specs=[pl.BlockSpec((1, T, D), lambda b, g: (b, 0, part * G + g)),
                  pl.BlockSpec((1, CMP_BLOCK, D), lambda b, g: (part, 0, 0)),
                  pl.BlockSpec((1, CMP_BLOCK * D, CMP_HIDDEN), lambda b, g: (part, 0, 0)),
                  pl.BlockSpec((1, CMP_HIDDEN, D), lambda b, g: (part, 0, 0))],
        out_specs=pl.BlockSpec((1, 1, T // CMP_STRIDE, D), lambda b, g: (b, g, 0, 0)),
        out_shape=jax.ShapeDtypeStruct((B, G, T // CMP_STRIDE, D), jnp.bfloat16),
        compiler_params=pltpu.CompilerParams(
            dimension_semantics=("parallel", "parallel"), vmem_limit_bytes=VMEM_LIMIT),
        name="compress_blocks",
    )(kv, pe.astype(jnp.float32), w1.astype(jnp.float32), w2.astype(jnp.float32))


def nsa_shared_kv(h, kv_w, cmp_pe, cmp_w1, cmp_w2):
    W = NSA_KV_GROUPS * NSA_HEAD_DIM
    kv = dense(h, kv_w)
    k_cmp = compress_blocks(kv, 0, cmp_pe, cmp_w1, cmp_w2)
    v_cmp = compress_blocks(kv, 1, cmp_pe, cmp_w1, cmp_w2)
    k_slc = rope_cast(kv, 2 * W, W, True)
    v_slc = rope_cast(kv, 3 * W, W, False)
    k_win = rope_cast(kv, 4 * W, W, True)
    v_win = rope_cast(kv, 5 * W, W, False)
    return (k_cmp, v_cmp, k_slc, v_slc, k_win, v_win)


def _branch_gate(gate_ref, branch, r):
    row = branch * NSA_HEADS + pl.program_id(1) * NSA_HEADS_PER_GROUP + r
    return jax.nn.sigmoid(gate_ref[0, pl.ds(row, 1), :])


def _cmp_select_kernel(q_ref, kc_ref, vc_ref, ov_ref, gate_ref, o_ref, sel_ref):
    i = pl.program_id(2)
    tq, D, n_c, n_s = ATTN_TILE, NSA_HEAD_DIM, kc_ref.shape[2], T_BLOCKS
    tpos = i * tq + lax.broadcasted_iota(jnp.int32, (n_c, tq), 1)
    block_end = lax.broadcasted_iota(jnp.int32, (n_c, tq), 0) * CMP_STRIDE + (CMP_BLOCK - 1)
    visible = block_end <= tpos
    kc = kc_ref[0, 0]
    vc = vc_ref[0, 0]
    p_sum = jnp.zeros((n_c, tq), jnp.float32)
    for r in range(NSA_HEADS_PER_GROUP):
        s = lax.dot_general(kc, q_ref[0, :, r * D:(r + 1) * D], (((1,), (1,)), ((), ())),
                            preferred_element_type=jnp.float32)
        s = jnp.where(visible, s, NEG_INF)
        e = jnp.exp2(s - jnp.max(s, axis=0, keepdims=True))
        p = jnp.where(visible, e / jnp.sum(e, axis=0, keepdims=True), 0.0)
        o_t = lax.dot_general(vc, p.astype(jnp.bfloat16), (((0,), (0,)), ((), ())),
                              preferred_element_type=jnp.float32)
        o_ref[0, :, r * D:(r + 1) * D] = (o_t * _branch_gate(gate_ref, 0, r)).T
        p_sum = p_sum + p
    p_slc = jnp.dot(ov_ref[...], p_sum.astype(jnp.bfloat16), preferred_element_type=jnp.float32)
    blk = lax.broadcasted_iota(jnp.int32, (n_s, tq), 0)
    cur = (i * tq + lax.broadcasted_iota(jnp.int32, (n_s, tq), 1)) // SLC_BLOCK
    causal_blk = blk <= cur
    forced = (blk == 0) | (causal_blk & (blk > cur - SLC_LOCAL))
    score = jnp.where(causal_blk, jnp.where(forced, FORCE_SCORE, p_slc), -1.0)
    rank = jnp.zeros((n_s, tq), jnp.float32)
    for other in range(n_s):
        row = score[other:other + 1, :]
        ahead = (row > score) | ((row == score) & (blk > other))
        rank = rank + jnp.where(ahead, 1.0, 0.0)
    picked = (rank < float(min(SLC_TOPK, n_s))) & (score >= 0.0)
    sel_ref[0, 0] = jnp.where(picked, 1.0, 0.0).astype(sel_ref.dtype)


def nsa_compressed_select(q_cmp, k_cmp, v_cmp, gate_logits_t):
    B, T, _ = q_cmp.shape
    G, R, D, tq = NSA_KV_GROUPS, NSA_HEADS_PER_GROUP, NSA_HEAD_DIM, ATTN_TILE
    n_c = k_cmp.shape[2]
    n_s = T // SLC_BLOCK
    c0 = np.arange(n_c) * CMP_STRIDE
    s0 = np.arange(n_s) * SLC_BLOCK
    ov = np.clip(np.minimum(c0[None, :] + CMP_BLOCK, s0[:, None] + SLC_BLOCK)
                 - np.maximum(c0[None, :], s0[:, None]), 0, None) / CMP_BLOCK
    cmp_spec = pl.BlockSpec((1, 1, n_c, D), lambda b, g, i: (b, g, 0, 0))
    return pl.pallas_call(
        _cmp_select_kernel,
        grid=(B, G, T // tq),
        in_specs=[pl.BlockSpec((1, tq, R * D), lambda b, g, i: (b, i, g)), cmp_spec, cmp_spec,
                  pl.BlockSpec((n_s, n_c), lambda b, g, i: (0, 0)),
                  pl.BlockSpec((1, 3 * NSA_HEADS, tq), lambda b, g, i: (b, 0, i))],
        out_specs=[pl.BlockSpec((1, tq, R * D), lambda b, g, i: (b, i, g)),
                   pl.BlockSpec((1, 1, n_s, tq), lambda b, g, i: (b, g, 0, i))],
        out_shape=[jax.ShapeDtypeStruct((B, T, G * R * D), jnp.float32),
                   jax.ShapeDtypeStruct((B, G, n_s, T), jnp.bfloat16)],
        compiler_params=pltpu.CompilerParams(
            dimension_semantics=("parallel", "parallel", "parallel"), vmem_limit_bytes=VMEM_LIMIT),
        name="compressed_attention_select",
    )(q_cmp, k_cmp, v_cmp, jnp.asarray(ov, jnp.bfloat16), gate_logits_t)


def _attn_kernel(*refs, windowed):
    if windowed:
        q_ref, k_ref, v_ref, gate_ref, prev_ref, o_ref, m_s, l_s, acc_s = refs
    else:
        q_ref, k_ref, v_ref, sel_ref, gate_ref, prev_ref, o_ref, m_s, l_s, acc_s = refs
    i = pl.program_id(2)
    tq, tk, D = ATTN_TILE, ATTN_TK, NSA_HEAD_DIM
    j_hi = (i + 1) * (tq // tk) - 1
    m_s[...] = jnp.full(m_s.shape, NEG_INF, jnp.float32)
    l_s[...] = jnp.zeros(l_s.shape, jnp.float32)
    acc_s[...] = jnp.zeros(acc_s.shape, jnp.float32)
    tpos = i * tq + lax.broadcasted_iota(jnp.int32, (tk, tq), 1)

    def body(jj, carry):
        j = j_hi - jj
        start = pl.multiple_of(j * tk, tk)
        kblk = k_ref[0, pl.ds(start, tk), :]
        vblk = v_ref[0, pl.ds(start, tk), :]
        kpos = j * tk + lax.broadcasted_iota(jnp.int32, (tk, tq), 0)
        ok = kpos <= tpos
        if windowed:
            ok = ok & (kpos > tpos - WINDOW)
        else:
            blk_of_key = j * (tk // SLC_BLOCK) + lax.broadcasted_iota(
                jnp.int32, (tk, T_BLOCKS), 0) // SLC_BLOCK
            expand = (lax.broadcasted_iota(jnp.int32, (tk, T_BLOCKS), 1) == blk_of_key)
            selm = jnp.dot(expand.astype(jnp.bfloat16), sel_ref[0, 0],
                           preferred_element_type=jnp.float32)
            ok = ok & (selm > 0.5)
        def scores(r):
            return lax.dot_general(kblk, q_ref[0, :, r * D:(r + 1) * D], (((1,), (1,)), ((), ())),
                                   preferred_element_type=jnp.float32)

        s_next = scores(0)
        for r in range(NSA_HEADS_PER_GROUP):
            s = jnp.where(ok, s_next, NEG_INF)
            if r + 1 < NSA_HEADS_PER_GROUP:
                s_next = scores(r + 1)
            m_prev = m_s[r]
            m_new = jnp.maximum(m_prev, jnp.max(s, axis=0, keepdims=True))
            alpha = jnp.exp2(m_prev - m_new)
            p = jnp.exp2(s - m_new)
            l_s[r] = alpha * l_s[r] + jnp.sum(p, axis=0, keepdims=True)
            acc_s[r] = alpha * acc_s[r] + lax.dot_general(
                vblk, p.astype(jnp.bfloat16), (((0,), (0,)), ((), ())),
                preferred_element_type=jnp.float32)
            m_s[r] = m_new
        return carry

    j_lo = jnp.maximum(i * tq - WINDOW + 1, 0) // tk if windowed else 0
    lax.fori_loop(0, j_hi - j_lo + 1, body, 0)
    branch = 2 if windowed else 1
    for r in range(NSA_HEADS_PER_GROUP):
        gated = (acc_s[r] / l_s[r] * _branch_gate(gate_ref, branch, r)).T
        o_ref[0, :, r * D:(r + 1) * D] = (prev_ref[0, :, r * D:(r + 1) * D] + gated).astype(o_ref.dtype)


def masked_attention(qs, kb, vb, sel, gate_logits_t, prev, out_dtype):
    B, T, _ = qs.shape
    G, R, D = NSA_KV_GROUPS, NSA_HEADS_PER_GROUP, NSA_HEAD_DIM
    tq = ATTN_TILE
    windowed = sel is None
    q_spec = pl.BlockSpec((1, tq, R * D), lambda b, g, i: (b, i, g))
    in_specs = [q_spec,
                pl.BlockSpec((1, T, D), lambda b, g, i: (b, 0, g)),
                pl.BlockSpec((1, T, D), lambda b, g, i: (b, 0, g))]
    args = [qs, kb, vb]
    if not windowed:
        in_specs.append(pl.BlockSpec((1, 1, T // SLC_BLOCK, tq), lambda b, g, i: (b, g, 0, i)))
        args.append(sel)
    in_specs += [pl.BlockSpec((1, 3 * NSA_HEADS, tq), lambda b, g, i: (b, 0, i)), q_spec]
    args += [gate_logits_t, prev]
    return pl.pallas_call(
        functools.partial(_attn_kernel, windowed=windowed),
        grid=(B, G, T // tq),
        in_specs=in_specs,
        out_specs=pl.BlockSpec((1, tq, R * D), lambda b, g, i: (b, i, g)),
        out_shape=jax.ShapeDtypeStruct((B, T, G * R * D), out_dtype),
        scratch_shapes=[pltpu.VMEM((R, 1, tq), jnp.float32),
                        pltpu.VMEM((R, 1, tq), jnp.float32),
                        pltpu.VMEM((R, D, tq), jnp.float32)],
        compiler_params=pltpu.CompilerParams(
            dimension_semantics=("parallel", "parallel", "arbitrary"), vmem_limit_bytes=VMEM_LIMIT),
        name="window_attention" if windowed else "selected_attention",
    )(*args)


def nsa_mixer(x, w_q, w_out, k_cmp, v_cmp, k_slc, v_slc, k_win, v_win):
    B, T, _ = x.shape
    H, D = NSA_HEADS, NSA_HEAD_DIM
    q = dense(x, w_q, 0, NSA_Q_WIDTH)
    gate_t = jnp.swapaxes(dense(x, w_q, NSA_Q_WIDTH), 1, 2)
    q_scale = D ** -0.5 * math.log2(math.e)
    q_cmp = rope_cast(q, 0, NSA_Q_WIDTH, False, q_scale)
    q_rot = rope_cast(q, 0, NSA_Q_WIDTH, True, q_scale)
    o, sel = nsa_compressed_select(q_cmp, k_cmp, v_cmp, gate_t)
    o = masked_attention(q_rot, k_slc, v_slc, sel, gate_t, o, jnp.float32)
    o = masked_attention(q_rot, k_win, v_win, None, gate_t, o, jnp.bfloat16)
    return dense(o, w_out)


def kernel(x, a_w_in, a_conv_w, a_a_log, a_dt_bias, a_norm_w, a_w_out, kv_w, cmp_pe, cmp_w1, cmp_w2,
           b_w_q, b_w_out, router_w, router_bias, moe_w_gate, moe_w_up, moe_w_down, ln_g, ln_b):
    B, T, D = x.shape
    xf = x.astype(jnp.float32).reshape(B * T, D)
    x16 = xf.astype(jnp.bfloat16)
    shared_kv = None
    for layer in range(DEPTH):
        xin = x16.reshape(B, T, D)
        if layer < N_A_LAYERS:
            mix = gdn_mixer(xin, a_w_in[layer], a_conv_w[layer], a_a_log[layer], a_dt_bias[layer],
                            a_norm_w[layer], a_w_out[layer])
        else:
            if shared_kv is None:
                shared_kv = nsa_shared_kv(xin, kv_w, cmp_pe, cmp_w1, cmp_w2)
            j = layer - N_A_LAYERS
            mix = nsa_mixer(xin, b_w_q[j], b_w_out[j], *shared_kv)
        xf, x16, logits = residual_layer_norm(xf, mix.reshape(B * T, D), ln_g[layer, 0], ln_b[layer, 0],
                                              router_w)
        ffn = moe_ffn(xf, logits, router_bias, moe_w_gate, moe_w_up, moe_w_down, layer)
        xf, x16 = residual_layer_norm(xf, ffn, ln_g[layer, 1], ln_b[layer, 1], n_mix=TOP_K)
    return xf.reshape(B, T, D).astype(x.dtype)
```

```python
import functools
import math

import jax
import jax.numpy as jnp
import numpy as np
from jax import lax
from jax.experimental import pallas as pl
from jax.experimental.pallas import tpu as pltpu

D_MODEL = 2048
SEQ = 4096
DEPTH = 2
N_A_LAYERS = DEPTH // 2

GDN_HEAD_DIM = 128
GDN_QK_HEADS = D_MODEL // GDN_HEAD_DIM
GDN_V_HEADS = 2 * GDN_QK_HEADS
GDN_QK_WIDTH = GDN_QK_HEADS * GDN_HEAD_DIM
GDN_V_WIDTH = GDN_V_HEADS * GDN_HEAD_DIM
GDN_CONV_CH = 2 * GDN_QK_WIDTH + GDN_V_WIDTH
GDN_CHUNK = 64
GDN_CONV = 4
GDN_TILE = 256
GDN_HK_STEP = 4

NSA_HEAD_DIM = 128
NSA_HEADS = D_MODEL // NSA_HEAD_DIM
NSA_KV_GROUPS = 4
NSA_HEADS_PER_GROUP = NSA_HEADS // NSA_KV_GROUPS
NSA_Q_WIDTH = NSA_HEADS * NSA_HEAD_DIM
CMP_BLOCK = 32
CMP_STRIDE = 16
CMP_HIDDEN = 512
SLC_BLOCK = 64
SLC_TOPK = 16
SLC_LOCAL = 2
WINDOW = 512
ROPE_THETA = 10000.0

N_EXPERTS = 32
N_GROUPS = 8
EXPERTS_PER_GROUP = N_EXPERTS // N_GROUPS
TOP_K = 2
D_EXPERT = D_MODEL // 4

DEEPNORM_ALPHA = (2 * DEPTH) ** 0.25
LN_EPS = 1e-5
RMS_EPS = 1e-6
NEG_INF = -1e30
FORCE_SCORE = 1e6

LANE = 128
VMEM_LIMIT = 48 * 1024 * 1024
MOE_VMEM_LIMIT = 56 * 1024 * 1024
MOE_TILE = 256
ATTN_TILE = 256
ATTN_TK = 256
T_BLOCKS = SEQ // SLC_BLOCK


def _mm_kernel(a_ref, b_ref, o_ref, b16_s):
    @pl.when(pl.program_id(1) == 0)
    def _():
        b16_s[...] = b_ref[...].astype(jnp.bfloat16)

    o_ref[...] = jnp.dot(a_ref[...], b16_s[...], preferred_element_type=jnp.float32)


def pmatmul(a, b, col_start=0, n_cols=None, row_block=0, tm=1024):
    m, k = a.shape
    n_cols = b.shape[1] - col_start if n_cols is None else n_cols
    if n_cols % LANE or col_start % LANE:
        b = jnp.pad(b[:, col_start:col_start + n_cols], ((0, 0), (0, -n_cols % LANE)))
        return pmatmul(a, b)[:, :n_cols]
    tn = math.gcd(n_cols, 1024 if k <= 2048 else 512)
    assert col_start % tn == 0 and m % tm == 0 and b.shape[0] % k == 0
    off = col_start // tn
    return pl.pallas_call(
        _mm_kernel,
        grid=(n_cols // tn, m // tm),
        in_specs=[pl.BlockSpec((tm, k), lambda j, i: (i, 0)),
                  pl.BlockSpec((k, tn), lambda j, i: (row_block, j + off))],
        out_specs=pl.BlockSpec((tm, tn), lambda j, i: (i, j)),
        out_shape=jax.ShapeDtypeStruct((m, n_cols), jnp.float32),
        scratch_shapes=[pltpu.VMEM((k, tn), jnp.bfloat16)],
        compiler_params=pltpu.CompilerParams(
            dimension_semantics=("parallel", "arbitrary"), vmem_limit_bytes=VMEM_LIMIT),
        name="dense_matmul",
    )(a.astype(jnp.bfloat16), b.astype(jnp.float32))


def dense(x, w, col_start=0, n_cols=None):
    lead = x.shape[:-1]
    out = pmatmul(x.reshape(-1, x.shape[-1]), w, col_start, n_cols)
    return out.reshape(lead + (out.shape[-1],))


def _moe_kernel(tile_expert_ref, n_tiles_ref, tok_ref, tok_next_ref, dst_prev_ref, x_hbm, gate_ref,
                wg_ref, wu_ref, wd_ref, out_hbm, xbuf, ybuf, wg16, wu16, wd16, sem_in, sem_out):
    i = pl.program_id(0)
    n_tiles = n_tiles_ref[0]
    slot = i % 2
    spare = out_hbm.shape[0] - MOE_TILE

    def gather_row(idx_ref, s, r):
        return pltpu.make_async_copy(x_hbm.at[pl.ds(idx_ref[0, 0, r], 1)], xbuf.at[s, pl.ds(r, 1)],
                                     sem_in.at[s])

    def scatter_row(s, r, dst_row):
        return pltpu.make_async_copy(ybuf.at[s, pl.ds(r, 1)], out_hbm.at[pl.ds(dst_row, 1)], sem_out)

    def wait_gather(s):
        pltpu.make_async_copy(x_hbm.at[pl.ds(0, MOE_TILE)], xbuf.at[s], sem_in.at[s]).wait()

    def wait_scatter():
        pltpu.make_async_copy(ybuf.at[0], out_hbm.at[pl.ds(0, MOE_TILE)], sem_out).wait()

    @pl.when(i == 0)
    def _():
        ybuf[...] = jnp.zeros(ybuf.shape, jnp.float32)

        def first_rows(r, c):
            gather_row(tok_ref, 0, r).start()
            scatter_row(0, r, spare + r).start()
            return c
        lax.fori_loop(0, MOE_TILE, first_rows, 0, unroll=8)

    @pl.when(i < n_tiles)
    def _():
        @pl.when(jnp.logical_or(i == 0, tile_expert_ref[i] != tile_expert_ref[jnp.maximum(i - 1, 0)]))
        def _():
            wg16[...] = wg_ref[0, 0].astype(jnp.bfloat16)
            wu16[...] = wu_ref[0, 0].astype(jnp.bfloat16)
            wd16[...] = wd_ref[0, 0].astype(jnp.bfloat16)

        wait_gather(slot)
        wait_scatter()
        x = xbuf[slot].astype(jnp.bfloat16)
        for r in range(MOE_TILE):
            gather_row(tok_next_ref, 1 - slot, r).start()
            scatter_row(1 - slot, r, dst_prev_ref[0, 0, r]).start()
        g = jnp.dot(x, wg16[...], preferred_element_type=jnp.float32)
        u = jnp.dot(x, wu16[...], preferred_element_type=jnp.float32)
        h = (g * jax.nn.sigmoid(g)) * u * gate_ref[...]
        ybuf[slot] = jnp.dot(h.astype(jnp.bfloat16), wd16[...], preferred_element_type=jnp.float32)

    @pl.when(i == n_tiles)
    def _():
        wait_scatter()

        def last_rows(r, c):
            scatter_row(1 - slot, r, dst_prev_ref[0, 0, r]).start()
            return c
        lax.fori_loop(0, MOE_TILE, last_rows, 0, unroll=8)
        wait_scatter()
        wait_gather(slot)


def moe_ffn(h, router_logits, router_bias, w_gate, w_up, w_down, layer):
    n_tok, D = h.shape
    aff = jax.nn.sigmoid(router_logits.astype(jnp.float32))
    biased = (aff + router_bias.astype(jnp.float32)).reshape(-1, N_GROUPS, EXPERTS_PER_GROUP)

    def top2(v):
        i1 = jnp.argmax(v, axis=-1)
        rest = jnp.where(jnp.arange(v.shape[-1]) == i1[..., None], -jnp.inf, v)
        i2 = jnp.argmax(rest, axis=-1)
        return jnp.max(v, axis=-1), jnp.max(rest, axis=-1), i1, i2

    g1, g2, _, _ = top2(biased)
    best_group = jnp.argmax(g1 + g2, axis=-1)
    cand = jnp.take_along_axis(biased, best_group[:, None, None], axis=1)[:, 0]
    _, _, i1, i2 = top2(cand)
    top_idx = best_group[:, None] * EXPERTS_PER_GROUP + jnp.stack([i1, i2], axis=-1)
    top_aff = jnp.take_along_axis(aff, top_idx, axis=-1)
    top_w = top_aff / jnp.sum(top_aff, axis=-1, keepdims=True)

    n_asg = n_tok * TOP_K
    max_tiles = n_asg // MOE_TILE + N_EXPERTS + 1
    n_rows = max_tiles * MOE_TILE
    e_flat = top_idx.T.reshape(-1).astype(jnp.int32)
    hot = (e_flat[:, None] == jnp.arange(N_EXPERTS, dtype=jnp.int32)[None, :]).astype(jnp.int32)
    running = jnp.cumsum(hot, axis=0)
    rank = jnp.sum(hot * (running - 1), axis=1)
    counts = running[-1]
    tiles_per = (counts + MOE_TILE - 1) // MOE_TILE
    tile_end = jnp.cumsum(tiles_per)
    row_start = (tile_end - tiles_per) * MOE_TILE
    row_of_asg = jnp.sum(hot * row_start[None, :], axis=1) + rank
    asg_of_row = jnp.full((n_rows,), -1, jnp.int32).at[row_of_asg].set(jnp.arange(n_asg, dtype=jnp.int32))
    token_of_row = jnp.maximum(asg_of_row, 0) % n_tok
    gate_of_row = jnp.where(asg_of_row >= 0, top_w.T.reshape(-1)[jnp.maximum(asg_of_row, 0)], 0.0)
    spare_rows = n_asg + jnp.arange(MOE_TILE, dtype=jnp.int32)
    dst3 = jnp.where(asg_of_row >= 0, asg_of_row, jnp.tile(spare_rows, max_tiles)).reshape(max_tiles, 1, MOE_TILE)
    dst_prev3 = jnp.concatenate([spare_rows.reshape(1, 1, MOE_TILE), dst3[:-1]], axis=0)
    n_tiles = tile_end[-1:].astype(jnp.int32)
    tile_expert = jnp.minimum(
        jnp.searchsorted(tile_end, jnp.arange(max_tiles, dtype=jnp.int32), side="right"),
        N_EXPERTS - 1).astype(jnp.int32)
    tile_expert = jnp.where(jnp.arange(max_tiles) < n_tiles[0], tile_expert,
                            tile_expert[jnp.maximum(n_tiles[0] - 1, 0)])

    wspec_in = pl.BlockSpec((1, 1, D, D_EXPERT), lambda i, te, nt: (layer, te[i], 0, 0))
    idx_spec = pl.BlockSpec((1, 1, MOE_TILE), lambda i, te, nt: (i, 0, 0), memory_space=pltpu.SMEM)
    next_spec = pl.BlockSpec((1, 1, MOE_TILE), lambda i, te, nt: (jnp.minimum(i + 1, max_tiles - 1), 0, 0),
                             memory_space=pltpu.SMEM)
    tok3 = token_of_row.reshape(max_tiles, 1, MOE_TILE)
    y = pl.pallas_call(
        _moe_kernel,
        grid_spec=pltpu.PrefetchScalarGridSpec(
            num_scalar_prefetch=2,
            grid=(max_tiles,),
            in_specs=[idx_spec, next_spec, idx_spec,
                      pl.BlockSpec(memory_space=pl.ANY),
                      pl.BlockSpec((MOE_TILE, 1), lambda i, te, nt: (i, 0)),
                      wspec_in, wspec_in,
                      pl.BlockSpec((1, 1, D_EXPERT, D), lambda i, te, nt: (layer, te[i], 0, 0))],
            out_specs=pl.BlockSpec(memory_space=pl.ANY),
            scratch_shapes=[pltpu.VMEM((2, MOE_TILE, D), jnp.float32),
                            pltpu.VMEM((2, MOE_TILE, D), jnp.float32),
                            pltpu.VMEM((D, D_EXPERT), jnp.bfloat16),
                            pltpu.VMEM((D, D_EXPERT), jnp.bfloat16),
                            pltpu.VMEM((D_EXPERT, D), jnp.bfloat16),
                            pltpu.SemaphoreType.DMA((2,)), pltpu.SemaphoreType.DMA(())],
        ),
        out_shape=jax.ShapeDtypeStruct((n_asg + MOE_TILE, D), jnp.float32),
        compiler_params=pltpu.CompilerParams(
            dimension_semantics=("arbitrary",), vmem_limit_bytes=MOE_VMEM_LIMIT),
        name="routed_moe",
    )(tile_expert, n_tiles, tok3, tok3, dst_prev3, h, gate_of_row[:, None], w_gate, w_up, w_down)
    return y


def _residual_ln_kernel(x_ref, *rest, n_mix, with_router):
    mix_refs, rest = rest[:n_mix], rest[n_mix:]
    if with_router:
        g_ref, b_ref, rw_ref, o_ref, o16_ref, logit_ref = rest
    else:
        g_ref, b_ref, o_ref, o16_ref = rest
    h = DEEPNORM_ALPHA * x_ref[...]
    for mix_ref in mix_refs:
        h = h + mix_ref[...]
    mu = jnp.mean(h, axis=-1, keepdims=True)
    var = jnp.mean(jnp.square(h - mu), axis=-1, keepdims=True)
    y = (h - mu) * lax.rsqrt(var + LN_EPS) * g_ref[...] + b_ref[...]
    o_ref[...] = y
    y16 = y.astype(jnp.bfloat16)
    o16_ref[...] = y16
    if with_router:
        logit_ref[...] = jnp.dot(y16, rw_ref[...].astype(jnp.bfloat16), preferred_element_type=jnp.float32)


def residual_layer_norm(x, mix, g, b, router_w=None, n_mix=1):
    n, d = x.shape
    tm = 256
    row = lambda i: (i, 0)
    fixed = lambda i: (0, 0)
    in_specs = ([pl.BlockSpec((tm, d), row)]
                + [pl.BlockSpec((tm, d), functools.partial(lambda i, s: (i + s * (n // tm), 0), s=s))
                   for s in range(n_mix)]
                + [pl.BlockSpec((1, d), fixed), pl.BlockSpec((1, d), fixed)])
    args = [x] + [mix] * n_mix + [g.astype(jnp.float32).reshape(1, d), b.astype(jnp.float32).reshape(1, d)]
    out_specs = [pl.BlockSpec((tm, d), row), pl.BlockSpec((tm, d), row)]
    out_shape = [jax.ShapeDtypeStruct((n, d), jnp.float32), jax.ShapeDtypeStruct((n, d), jnp.bfloat16)]
    if router_w is not None:
        e_pad = -router_w.shape[1] % LANE
        rw = jnp.pad(router_w.astype(jnp.float32), ((0, 0), (0, e_pad)))
        in_specs.append(pl.BlockSpec(rw.shape, fixed))
        args.append(rw)
        out_specs.append(pl.BlockSpec((tm, rw.shape[1]), row))
        out_shape.append(jax.ShapeDtypeStruct((n, rw.shape[1]), jnp.float32))
    outs = pl.pallas_call(
        functools.partial(_residual_ln_kernel, n_mix=n_mix, with_router=router_w is not None),
        grid=(n // tm,),
        in_specs=in_specs, out_specs=out_specs, out_shape=out_shape,
        compiler_params=pltpu.CompilerParams(
            dimension_semantics=("parallel",), vmem_limit_bytes=VMEM_LIMIT),
        name="residual_layer_norm",
    )(*args)
    if router_w is not None:
        return outs[0], outs[1], outs[2][:, :router_w.shape[1]]
    return outs[0], outs[1]


def _softplus(x):
    return jnp.maximum(x, 0.0) + jnp.log1p(jnp.exp(-jnp.abs(x)))


def _bdot(a, b):
    return jnp.dot(a.astype(jnp.bfloat16), b.astype(jnp.bfloat16), preferred_element_type=jnp.float32)


def _bdot_nt(a, b):
    return lax.dot_general(a.astype(jnp.bfloat16), b.astype(jnp.bfloat16), (((1,), (1,)), ((), ())),
                           preferred_element_type=jnp.float32)


def _bdot_tn(a, b):
    return lax.dot_general(a.astype(jnp.bfloat16), b.astype(jnp.bfloat16), (((0,), (0,)), ((), ())),
                           preferred_element_type=jnp.float32)


def _gdn_prep_kernel(x_ref, halo_ref, w_ref, o_ref):
    i = pl.program_id(1)
    j = pl.program_id(2)
    tt = x_ref.shape[1]
    x = x_ref[0]
    halo = jnp.where(i == 0, 0.0, halo_ref[0])
    xx = jnp.concatenate([halo, x], axis=0)
    w = w_ref[...]
    y = w[3:4] * x
    for tap in range(GDN_CONV - 1):
        lo = 8 - (GDN_CONV - 1) + tap
        y = y + w[tap:tap + 1] * xx[lo:lo + tt]
    y = y * jax.nn.sigmoid(y)
    q_blocks = GDN_QK_WIDTH // x_ref.shape[2]

    @pl.when(j >= 2 * q_blocks)
    def _():
        o_ref[0] = y

    @pl.when(j < 2 * q_blocks)
    def _():
        scale = jnp.where(j < q_blocks, GDN_HEAD_DIM ** -0.5, 1.0)
        for h in range(x_ref.shape[2] // GDN_HEAD_DIM):
            yh = y[:, h * GDN_HEAD_DIM:(h + 1) * GDN_HEAD_DIM]
            inv = lax.rsqrt(jnp.sum(yh * yh, axis=-1, keepdims=True) + RMS_EPS)
            o_ref[0, :, h * GDN_HEAD_DIM:(h + 1) * GDN_HEAD_DIM] = yh * inv * scale


def gdn_prep(qkv, conv_w):
    B, T, CH = qkv.shape
    tt, tc = GDN_TILE, 512
    return pl.pallas_call(
        _gdn_prep_kernel,
        grid=(B, T // tt, CH // tc),
        in_specs=[pl.BlockSpec((1, tt, tc), lambda b, i, j: (b, i, j)),
                  pl.BlockSpec((1, 8, tc), lambda b, i, j: (b, jnp.maximum(i * (tt // 8) - 1, 0), j)),
                  pl.BlockSpec((GDN_CONV, tc), lambda b, i, j: (0, j))],
        out_specs=pl.BlockSpec((1, tt, tc), lambda b, i, j: (b, i, j)),
        out_shape=jax.ShapeDtypeStruct((B, T, CH), jnp.float32),
        compiler_params=pltpu.CompilerParams(
            dimension_semantics=("parallel", "parallel", "parallel"), vmem_limit_bytes=VMEM_LIMIT),
        name="gdn_conv_silu_norm",
    )(qkv, qkv, conv_w)


def _gdn_kernel(q_ref, k_ref, v_ref, z_ref, acol_ref, arow_ref, bcol_ref, alog_ref, dt_ref, nw_ref,
                o_ref, state_s):
    C, Dh = GDN_CHUNK, GDN_HEAD_DIM

    @pl.when(pl.program_id(2) == 0)
    def _():
        state_s[...] = jnp.zeros(state_s.shape, jnp.float32)

    row = lax.broadcasted_iota(jnp.int32, (C, C), 0)
    col = lax.broadcasted_iota(jnp.int32, (C, C), 1)
    causal = row >= col
    strict = row > col
    nw = nw_ref[...]
    probs = []
    for n, kh in [(n, kh) for n in range(GDN_TILE // C) for kh in range(GDN_HK_STEP)]:
        sl = slice(n * C, (n + 1) * C)
        q = q_ref[0, sl, kh * Dh:(kh + 1) * Dh]
        k = k_ref[0, sl, kh * Dh:(kh + 1) * Dh]
        kk = _bdot_nt(k, k)
        qk = _bdot_nt(q, k)
        for hh in range(2):
            vh = 2 * kh + hh
            head = pl.program_id(1) * (2 * GDN_HK_STEP) + vh
            is_head = lax.broadcasted_iota(jnp.int32, (1, GDN_V_HEADS), 1) == head

            def pick(a):
                return jnp.sum(jnp.where(is_head, a, 0.0), axis=1, keepdims=True)

            neg_a = -jnp.exp(pick(alog_ref[...]))
            dt = pick(dt_ref[...])
            g_col = neg_a * _softplus(pick(acol_ref[0, sl, :]) + dt)
            g_row = neg_a * _softplus(arow_ref[0, pl.ds(head, 1), :][:, sl] + dt)
            beta = jax.nn.sigmoid(pick(bcol_ref[0, sl, :]))
            gc_col = jnp.sum(jnp.where(causal, g_row, 0.0), axis=1, keepdims=True)
            gc_row = jnp.sum(jnp.where(row <= col, g_col, 0.0), axis=0, keepdims=True)
            g_last = jnp.sum(g_row, axis=1, keepdims=True)
            decay = jnp.where(causal, jnp.exp(jnp.where(causal, gc_col - gc_row, 0.0)), 0.0)
            m = jnp.where(strict, beta * kk * decay, 0.0)
            e_col = jnp.exp(gc_col)
            v = v_ref[0, sl, vh * Dh:(vh + 1) * Dh]
            probs.append(dict(
                sl=sl, hh=vh, x=-m, p=m, a=qk * decay,
                rhs=jnp.concatenate([beta * v, beta * e_col * k], axis=1),
                qe=q * e_col, ke=k * jnp.exp(g_last - gc_col), s_decay=jnp.exp(g_last)))
    for _ in range(int(math.log2(C)) - 1):
        for pr in probs:
            pr["p"] = _bdot(pr["p"], pr["p"])
        for pr in probs:
            pr["x"] = pr["x"] + pr["p"] + _bdot(pr["x"], pr["p"])
    for pr in probs:
        pr["sol"] = pr["rhs"] + _bdot(pr["x"], pr["rhs"])
    for pr in probs:
        sl, hh = pr["sl"], pr["hh"]
        u, w = pr["sol"][:, :Dh], pr["sol"][:, Dh:]
        state = state_s[hh]
        ws = _bdot(jnp.concatenate([w, pr["qe"]], axis=0), state)
        v_new = u - ws[:C]
        o = ws[C:] + _bdot(pr["a"], v_new)
        state_s[hh] = state * pr["s_decay"] + _bdot_tn(pr["ke"], v_new)
        z = z_ref[0, sl, hh * Dh:(hh + 1) * Dh]
        o = o * lax.rsqrt(jnp.mean(o * o, axis=-1, keepdims=True) + RMS_EPS) * nw
        o_ref[0, sl, hh * Dh:(hh + 1) * Dh] = (o * (z * jax.nn.sigmoid(z))).astype(o_ref.dtype)


def gdn_core(qkv, z, a_raw, b_raw, a_log, dt_bias, norm_w):
    B, T, _ = qkv.shape
    Hk, Dh, tt = GDN_QK_HEADS, GDN_HEAD_DIM, GDN_TILE
    Hv = GDN_V_HEADS
    hs = GDN_HK_STEP
    qw, vw = hs * Dh, 2 * hs * Dh
    k_off, v_off = GDN_QK_WIDTH // qw, 2 * GDN_QK_WIDTH // vw
    col_spec = pl.BlockSpec((1, tt, Hv), lambda b, h, c: (b, c, 0))
    head_spec = pl.BlockSpec((1, Hv), lambda b, h, c: (0, 0))
    return pl.pallas_call(
        _gdn_kernel,
        grid=(B, Hk // hs, T // tt),
        in_specs=[pl.BlockSpec((1, tt, qw), lambda b, h, c: (b, c, h)),
                  pl.BlockSpec((1, tt, qw), lambda b, h, c: (b, c, k_off + h)),
                  pl.BlockSpec((1, tt, vw), lambda b, h, c: (b, c, v_off + h)),
                  pl.BlockSpec((1, tt, vw), lambda b, h, c: (b, c, h)),
                  col_spec,
                  pl.BlockSpec((1, Hv, tt), lambda b, h, c: (b, 0, c)),
                  col_spec, head_spec, head_spec,
                  pl.BlockSpec((1, Dh), lambda b, h, c: (0, 0))],
        out_specs=pl.BlockSpec((1, tt, vw), lambda b, h, c: (b, c, h)),
        out_shape=jax.ShapeDtypeStruct((B, T, GDN_V_WIDTH), jnp.bfloat16),
        scratch_shapes=[pltpu.VMEM((2 * hs, Dh, Dh), jnp.float32)],
        compiler_params=pltpu.CompilerParams(
            dimension_semantics=("parallel", "parallel", "arbitrary"), vmem_limit_bytes=VMEM_LIMIT),
        name="gated_delta_rule",
    )(qkv, qkv, qkv, z, a_raw, jnp.swapaxes(a_raw, 1, 2), b_raw,
      a_log.astype(jnp.float32).reshape(1, Hv), dt_bias.astype(jnp.float32).reshape(1, Hv),
      norm_w.astype(jnp.float32).reshape(1, Dh))


def gdn_mixer(x, w_in, conv_w, a_log, dt_bias, norm_w, w_out):
    B, T, _ = x.shape
    Hk, Hv, Dh = GDN_QK_HEADS, GDN_V_HEADS, GDN_HEAD_DIM
    qkv = dense(x, w_in, 0, GDN_CONV_CH)
    z = dense(x, w_in, GDN_CONV_CH, GDN_V_WIDTH)
    a_raw, b_raw = jnp.split(dense(x, w_in, GDN_CONV_CH + GDN_V_WIDTH), [Hv], axis=-1)
    o = gdn_core(gdn_prep(qkv, conv_w), z, a_raw, b_raw, a_log, dt_bias, norm_w)
    return dense(o, w_out)


def _rope_cast_kernel(x_ref, cos_ref, sin_ref, o_ref, *, rotate, scale):
    D = NSA_HEAD_DIM
    for h in range(x_ref.shape[2] // D):
        x = x_ref[0, :, h * D:(h + 1) * D]
        if rotate:
            x = x * cos_ref[...] + pltpu.roll(x, D // 2, axis=1) * sin_ref[...]
        if scale != 1.0:
            x = x * scale
        o_ref[0, :, h * D:(h + 1) * D] = x.astype(o_ref.dtype)


def rope_cast(x, col_start, n_cols, rotate, scale=1.0):
    B, T, _ = x.shape
    tt, tc, D = 512, 512, NSA_HEAD_DIM
    half = D // 2
    inv_freq = ROPE_THETA ** (-jnp.arange(half, dtype=jnp.float32) / half)
    ang = jnp.arange(T, dtype=jnp.float32)[:, None] * inv_freq[None, :]
    cos2 = jnp.concatenate([jnp.cos(ang), jnp.cos(ang)], axis=1)
    sin2 = jnp.concatenate([-jnp.sin(ang), jnp.sin(ang)], axis=1)
    off = col_start // tc
    return pl.pallas_call(
        functools.partial(_rope_cast_kernel, rotate=rotate, scale=scale),
        grid=(B, T // tt, n_cols // tc),
        in_specs=[pl.BlockSpec((1, tt, tc), lambda b, i, j: (b, i, off + j)),
                  pl.BlockSpec((tt, D), lambda b, i, j: (i, 0)),
                  pl.BlockSpec((tt, D), lambda b, i, j: (i, 0))],
        out_specs=pl.BlockSpec((1, tt, tc), lambda b, i, j: (b, i, j)),
        out_shape=jax.ShapeDtypeStruct((B, T, n_cols), jnp.bfloat16),
        compiler_params=pltpu.CompilerParams(
            dimension_semantics=("parallel", "parallel", "parallel"), vmem_limit_bytes=VMEM_LIMIT),
        name="rope_cast",
    )(x, cos2, sin2)


def _compress_kernel(x_ref, pe_ref, w1_ref, w2_ref, o_ref):
    S, D = CMP_STRIDE, NSA_HEAD_DIM
    n_half = x_ref.shape[1] // S
    n_cmp = (x_ref.shape[1] - CMP_BLOCK) // S + 1
    top = jnp.zeros((n_half, CMP_HIDDEN), jnp.float32)
    bot = jnp.zeros((n_half, CMP_HIDDEN), jnp.float32)
    for s in range(S):
        xs = x_ref[0, pl.ds(s, n_half, stride=S), :]
        top = top + _bdot(xs + pe_ref[0, s:s + 1, :], w1_ref[0, s * D:(s + 1) * D, :])
        bot = bot + _bdot(xs + pe_ref[0, S + s:S + s + 1, :], w1_ref[0, (S + s) * D:(S + s + 1) * D, :])
    nxt = jnp.concatenate([bot[1:], jnp.zeros((1, CMP_HIDDEN), jnp.float32)], axis=0)
    hid = top + nxt
    hid = hid * jax.nn.sigmoid(hid)
    out = _bdot(hid, w2_ref[0])
    live = lax.broadcasted_iota(jnp.int32, out.shape, 0) < n_cmp
    o_ref[0, 0] = jnp.where(live, out, 0.0).astype(o_ref.dtype)


def compress_blocks(kv, part, pe, w1, w2):
    B, T, _ = kv.shape
    G, D = NSA_KV_GROUPS, NSA_HEAD_DIM
    return pl.pallas_call(
        _compress_kernel,
        grid=(B, G),
        in_specs=[pl.BlockSpec((1, T, D), lambda b, g: (b, 0, part * G + g)),
                  pl.BlockSpec((1, CMP_BLOCK, D), lambda b, g: (part, 0, 0)),
                  pl.BlockSpec((1, CMP_BLOCK * D, CMP_HIDDEN), lambda b, g: (part, 0, 0)),
                  pl.BlockSpec((1, CMP_HIDDEN, D), lambda b, g: (part, 0, 0))],
        out_specs=pl.BlockSpec((1, 1, T // CMP_STRIDE, D), lambda b, g: (b, g, 0, 0)),
        out_shape=jax.ShapeDtypeStruct((B, G, T // CMP_STRIDE, D), jnp.bfloat16),
        compiler_params=pltpu.CompilerParams(
            dimension_semantics=("parallel", "parallel"), vmem_limit_bytes=VMEM_LIMIT),
        name="compress_blocks",
    )(kv, pe.astype(jnp.float32), w1.astype(jnp.float32), w2.astype(jnp.float32))


def nsa_shared_kv(h, kv_w, cmp_pe, cmp_w1, cmp_w2):
    W = NSA_KV_GROUPS * NSA_HEAD_DIM
    kv = dense(h, kv_w)
    k_cmp = compress_blocks(kv, 0, cmp_pe, cmp_w1, cmp_w2)
    v_cmp = compress_blocks(kv, 1, cmp_pe, cmp_w1, cmp_w2)
    k_slc = rope_cast(kv, 2 * W, W, True)
    v_slc = rope_cast(kv, 3 * W, W, False)
    k_win = rope_cast(kv, 4 * W, W, True)
    v_win = rope_cast(kv, 5 * W, W, False)
    return (k_cmp, v_cmp, k_slc, v_slc, k_win, v_win)


def _branch_gate(gate_ref, branch, r):
    row = branch * NSA_HEADS + pl.program_id(1) * NSA_HEADS_PER_GROUP + r
    return jax.nn.sigmoid(gate_ref[0, pl.ds(row, 1), :])


def _cmp_select_kernel(q_ref, kc_ref, vc_ref, ov_ref, gate_ref, o_ref, sel_ref):
    i = pl.program_id(2)
    tq, D, n_c, n_s = ATTN_TILE, NSA_HEAD_DIM, kc_ref.shape[2], T_BLOCKS
    tpos = i * tq + lax.broadcasted_iota(jnp.int32, (n_c, tq), 1)
    block_end = lax.broadcasted_iota(jnp.int32, (n_c, tq), 0) * CMP_STRIDE + (CMP_BLOCK - 1)
    visible = block_end <= tpos
    kc = kc_ref[0, 0]
    vc = vc_ref[0, 0]
    p_sum = jnp.zeros((n_c, tq), jnp.float32)
    for r in range(NSA_HEADS_PER_GROUP):
        s = lax.dot_general(kc, q_ref[0, :, r * D:(r + 1) * D], (((1,), (1,)), ((), ())),
                            preferred_element_type=jnp.float32)
        s = jnp.where(visible, s, NEG_INF)
        e = jnp.exp2(s - jnp.max(s, axis=0, keepdims=True))
        p = jnp.where(visible, e / jnp.sum(e, axis=0, keepdims=True), 0.0)
        o_t = lax.dot_general(vc, p.astype(jnp.bfloat16), (((0,), (0,)), ((), ())),
                              preferred_element_type=jnp.float32)
        o_ref[0, :, r * D:(r + 1) * D] = (o_t * _branch_gate(gate_ref, 0, r)).T
        p_sum = p_sum + p
    p_slc = jnp.dot(ov_ref[...], p_sum.astype(jnp.bfloat16), preferred_element_type=jnp.float32)
    blk = lax.broadcasted_iota(jnp.int32, (n_s, tq), 0)
    cur = (i * tq + lax.broadcasted_iota(jnp.int32, (n_s, tq), 1)) // SLC_BLOCK
    causal_blk = blk <= cur
    forced = (blk == 0) | (causal_blk & (blk > cur - SLC_LOCAL))
    score = jnp.where(causal_blk, jnp.where(forced, FORCE_SCORE, p_slc), -1.0)
    rank = jnp.zeros((n_s, tq), jnp.float32)
    for other in range(n_s):
        row = score[other:other + 1, :]
        ahead = (row > score) | ((row == score) & (blk > other))
        rank = rank + jnp.where(ahead, 1.0, 0.0)
    picked = (rank < float(min(SLC_TOPK, n_s))) & (score >= 0.0)
    sel_ref[0, 0] = jnp.where(picked, 1.0, 0.0).astype(sel_ref.dtype)


def nsa_compressed_select(q_cmp, k_cmp, v_cmp, gate_logits_t):
    B, T, _ = q_cmp.shape
    G, R, D, tq = NSA_KV_GROUPS, NSA_HEADS_PER_GROUP, NSA_HEAD_DIM, ATTN_TILE
    n_c = k_cmp.shape[2]
    n_s = T // SLC_BLOCK
    c0 = np.arange(n_c) * CMP_STRIDE
    s0 = np.arange(n_s) * SLC_BLOCK
    ov = np.clip(np.minimum(c0[None, :] + CMP_BLOCK, s0[:, None] + SLC_BLOCK)
                 - np.maximum(c0[None, :], s0[:, None]), 0, None) / CMP_BLOCK
    cmp_spec = pl.BlockSpec((1, 1, n_c, D), lambda b, g, i: (b, g, 0, 0))
    return pl.pallas_call(
        _cmp_select_kernel,
        grid=(B, G, T // tq),
        in_specs=[pl.BlockSpec((1, tq, R * D), lambda b, g, i: (b, i, g)), cmp_spec, cmp_spec,
                  pl.BlockSpec((n_s, n_c), lambda b, g, i: (0, 0)),
                  pl.BlockSpec((1, 3 * NSA_HEADS, tq), lambda b, g, i: (b, 0, i))],
        out_specs=[pl.BlockSpec((1, tq, R * D), lambda b, g, i: (b, i, g)),
                   pl.BlockSpec((1, 1, n_s, tq), lambda b, g, i: (b, g, 0, i))],
        out_shape=[jax.ShapeDtypeStruct((B, T, G * R * D), jnp.float32),
                   jax.ShapeDtypeStruct((B, G, n_s, T), jnp.bfloat16)],
        compiler_params=pltpu.CompilerParams(
            dimension_semantics=("parallel", "parallel", "parallel"), vmem_limit_bytes=VMEM_LIMIT),
        name="compressed_attention_select",
    )(q_cmp, k_cmp, v_cmp, jnp.asarray(ov, jnp.bfloat16), gate_logits_t)


def _attn_kernel(*refs, windowed):
    if windowed:
        q_ref, k_ref, v_ref, gate_ref, prev_ref, o_ref, m_s, l_s, acc_s = refs
    else:
        q_ref, k_ref, v_ref, sel_ref, gate_ref, prev_ref, o_ref, m_s, l_s, acc_s = refs
    i = pl.program_id(2)
    tq, tk, D = ATTN_TILE, ATTN_TK, NSA_HEAD_DIM
    j_hi = (i + 1) * (tq // tk) - 1
    m_s[...] = jnp.full(m_s.shape, NEG_INF, jnp.float32)
    l_s[...] = jnp.zeros(l_s.shape, jnp.float32)
    acc_s[...] = jnp.zeros(acc_s.shape, jnp.float32)
    tpos = i * tq + lax.broadcasted_iota(jnp.int32, (tk, tq), 1)

    def body(jj, carry):
        j = j_hi - jj
        start = pl.multiple_of(j * tk, tk)
        kblk = k_ref[0, pl.ds(start, tk), :]
        vblk = v_ref[0, pl.ds(start, tk), :]
        kpos = j * tk + lax.broadcasted_iota(jnp.int32, (tk, tq), 0)
        ok = kpos <= tpos
        if windowed:
            ok = ok & (kpos > tpos - WINDOW)
        else:
            blk_of_key = j * (tk // SLC_BLOCK) + lax.broadcasted_iota(
                jnp.int32, (tk, T_BLOCKS), 0) // SLC_BLOCK
            expand = (lax.broadcasted_iota(jnp.int32, (tk, T_BLOCKS), 1) == blk_of_key)
            selm = jnp.dot(expand.astype(jnp.bfloat16), sel_ref[0, 0],
                           preferred_element_type=jnp.float32)
            ok = ok & (selm > 0.5)
        def scores(r):
            return lax.dot_general(kblk, q_ref[0, :, r * D:(r + 1) * D], (((1,), (1,)), ((), ())),
                                   preferred_element_type=jnp.float32)

        s_next = scores(0)
        for r in range(NSA_HEADS_PER_GROUP):
            s = jnp.where(ok, s_next, NEG_INF)
            if r + 1 < NSA_HEADS_PER_GROUP:
                s_next = scores(r + 1)
            m_prev = m_s[r]
            m_new = jnp.maximum(m_prev, jnp.max(s, axis=0, keepdims=True))
            alpha = jnp.exp2(m_prev - m_new)
            p = jnp.exp2(s - m_new)
            l_s[r] = alpha * l_s[r] + jnp.sum(p, axis=0, keepdims=True)
            acc_s[r] = alpha * acc_s[r] + lax.dot_general(
                vblk, p.astype(jnp.bfloat16), (((0,), (0,)), ((), ())),
                preferred_element_type=jnp.float32)
            m_s[r] = m_new
        return carry

    j_lo = jnp.maximum(i * tq - WINDOW + 1, 0) // tk if windowed else 0
    lax.fori_loop(0, j_hi - j_lo + 1, body, 0)
    branch = 2 if windowed else 1
    for r in range(NSA_HEADS_PER_GROUP):
        gated = (acc_s[r] / l_s[r] * _branch_gate(gate_ref, branch, r)).T
        o_ref[0, :, r * D:(r + 1) * D] = (prev_ref[0, :, r * D:(r + 1) * D] + gated).astype(o_ref.dtype)


def masked_attention(qs, kb, vb, sel, gate_logits_t, prev, out_dtype):
    B, T, _ = qs.shape
    G, R, D = NSA_KV_GROUPS, NSA_HEADS_PER_GROUP, NSA_HEAD_DIM
    tq = ATTN_TILE
    windowed = sel is None
    q_spec = pl.BlockSpec((1, tq, R * D), lambda b, g, i: (b, i, g))
    in_specs = [q_spec,
                pl.BlockSpec((1, T, D), lambda b, g, i: (b, 0, g)),
                pl.BlockSpec((1, T, D), lambda b, g, i: (b, 0, g))]
    args = [qs, kb, vb]
    if not windowed:
        in_specs.append(pl.BlockSpec((1, 1, T // SLC_BLOCK, tq), lambda b, g, i: (b, g, 0, i)))
        args.append(sel)
    in_specs += [pl.BlockSpec((1, 3 * NSA_HEADS, tq), lambda b, g, i: (b, 0, i)), q_spec]
    args += [gate_logits_t, prev]
    return pl.pallas_call(
        functools.partial(_attn_kernel, windowed=windowed),
        grid=(B, G, T // tq),
        in_specs=in_specs,
        out_specs=pl.BlockSpec((1, tq, R * D), lambda b, g, i: (b, i, g)),
        out_shape=jax.ShapeDtypeStruct((B, T, G * R * D), out_dtype),
        scratch_shapes=[pltpu.VMEM((R, 1, tq), jnp.float32),
                        pltpu.VMEM((R, 1, tq), jnp.float32),
                        pltpu.VMEM((R, D, tq), jnp.float32)],
        compiler_params=pltpu.CompilerParams(
            dimension_semantics=("parallel", "parallel", "arbitrary"), vmem_limit_bytes=VMEM_LIMIT),
        name="window_attention" if windowed else "selected_attention",
    )(*args)


def nsa_mixer(x, w_q, w_out, k_cmp, v_cmp, k_slc, v_slc, k_win, v_win):
    B, T, _ = x.shape
    H, D = NSA_HEADS, NSA_HEAD_DIM
    q = dense(x, w_q, 0, NSA_Q_WIDTH)
    gate_t = jnp.swapaxes(dense(x, w_q, NSA_Q_WIDTH), 1, 2)
    q_scale = D ** -0.5 * math.log2(math.e)
    q_cmp = rope_cast(q, 0, NSA_Q_WIDTH, False, q_scale)
    q_rot = rope_cast(q, 0, NSA_Q_WIDTH, True, q_scale)
    o, sel = nsa_compressed_select(q_cmp, k_cmp, v_cmp, gate_t)
    o = masked_attention(q_rot, k_slc, v_slc, sel, gate_t, o, jnp.float32)
    o = masked_attention(q_rot, k_win, v_win, None, gate_t, o, jnp.bfloat16)
    return dense(o, w_out)


def kernel(x, a_w_in, a_conv_w, a_a_log, a_dt_bias, a_norm_w, a_w_out, kv_w, cmp_pe, cmp_w1, cmp_w2,
           b_w_q, b_w_out, router_w, router_bias, moe_w_gate, moe_w_up, moe_w_down, ln_g, ln_b):
    B, T, D = x.shape
    xf = x.astype(jnp.float32).reshape(B * T, D)
    x16 = xf.astype(jnp.bfloat16)
    shared_kv = None
    for layer in range(DEPTH):
        xin = x16.reshape(B, T, D)
        if layer < N_A_LAYERS:
            mix = gdn_mixer(xin, a_w_in[layer], a_conv_w[layer], a_a_log[layer], a_dt_bias[layer],
                            a_norm_w[layer], a_w_out[layer])
        else:
            if shared_kv is None:
                shared_kv = nsa_shared_kv(xin, kv_w, cmp_pe, cmp_w1, cmp_w2)
            j = layer - N_A_LAYERS
            mix = nsa_mixer(xin, b_w_q[j], b_w_out[j], *shared_kv)
        xf, x16, logits = residual_layer_norm(xf, mix.reshape(B * T, D), ln_g[layer, 0], ln_b[layer, 0],
                                              router_w)
        ffn = moe_ffn(xf, logits, router_bias, moe_w_gate, moe_w_up, moe_w_down, layer)
        xf, x16 = residual_layer_norm(xf, ffn, ln_g[layer, 1], ln_b[layer, 1], n_mix=TOP_K)
    return xf.reshape(B, T, D).astype(x.dtype)
```

```python
import functools
import math

import jax
import jax.numpy as jnp
import numpy as np
from jax import lax
from jax.experimental import pallas as pl
from jax.experimental.pallas import tpu as pltpu

D_MODEL = 2048
SEQ = 4096
DEPTH = 2
N_A_LAYERS = DEPTH // 2

GDN_HEAD_DIM = 128
GDN_QK_HEADS = D_MODEL // GDN_HEAD_DIM
GDN_V_HEADS = 2 * GDN_QK_HEADS
GDN_QK_WIDTH = GDN_QK_HEADS * GDN_HEAD_DIM
GDN_V_WIDTH = GDN_V_HEADS * GDN_HEAD_DIM
GDN_CONV_CH = 2 * GDN_QK_WIDTH + GDN_V_WIDTH
GDN_CHUNK = 64
GDN_CONV = 4
GDN_TILE = 256
GDN_HK_STEP = 4

NSA_HEAD_DIM = 128
NSA_HEADS = D_MODEL // NSA_HEAD_DIM
NSA_KV_GROUPS = 4
NSA_HEADS_PER_GROUP = NSA_HEADS // NSA_KV_GROUPS
NSA_Q_WIDTH = NSA_HEADS * NSA_HEAD_DIM
CMP_BLOCK = 32
CMP_STRIDE = 16
CMP_HIDDEN = 512
SLC_BLOCK = 64
SLC_TOPK = 16
SLC_LOCAL = 2
WINDOW = 512
ROPE_THETA = 10000.0

N_EXPERTS = 32
N_GROUPS = 8
EXPERTS_PER_GROUP = N_EXPERTS // N_GROUPS
TOP_K = 2
D_EXPERT = D_MODEL // 4

DEEPNORM_ALPHA = (2 * DEPTH) ** 0.25
LN_EPS = 1e-5
RMS_EPS = 1e-6
NEG_INF = -1e30
FORCE_SCORE = 1e6

LANE = 128
VMEM_LIMIT = 48 * 1024 * 1024
MOE_VMEM_LIMIT = 56 * 1024 * 1024
MOE_TILE = 256
ATTN_TILE = 256
ATTN_TK = 256
T_BLOCKS = SEQ // SLC_BLOCK


def _mm_kernel(a_ref, b_ref, o_ref, b16_s):
    @pl.when(pl.program_id(1) == 0)
    def _():
        b16_s[...] = b_ref[...].astype(jnp.bfloat16)

    o_ref[...] = jnp.dot(a_ref[...], b16_s[...], preferred_element_type=jnp.float32)


def pmatmul(a, b, col_start=0, n_cols=None, row_block=0, tm=1024):
    m, k = a.shape
    n_cols = b.shape[1] - col_start if n_cols is None else n_cols
    if n_cols % LANE or col_start % LANE:
        b = jnp.pad(b[:, col_start:col_start + n_cols], ((0, 0), (0, -n_cols % LANE)))
        return pmatmul(a, b)[:, :n_cols]
    tn = math.gcd(n_cols, 1024 if k <= 2048 else 512)
    assert col_start % tn == 0 and m % tm == 0 and b.shape[0] % k == 0
    off = col_start // tn
    return pl.pallas_call(
        _mm_kernel,
        grid=(n_cols // tn, m // tm),
        in_specs=[pl.BlockSpec((tm, k), lambda j, i: (i, 0)),
                  pl.BlockSpec((k, tn), lambda j, i: (row_block, j + off))],
        out_specs=pl.BlockSpec((tm, tn), lambda j, i: (i, j)),
        out_shape=jax.ShapeDtypeStruct((m, n_cols), jnp.float32),
        scratch_shapes=[pltpu.VMEM((k, tn), jnp.bfloat16)],
        compiler_params=pltpu.CompilerParams(
            dimension_semantics=("parallel", "arbitrary"), vmem_limit_bytes=VMEM_LIMIT),
        name="dense_matmul",
    )(a.astype(jnp.bfloat16), b.astype(jnp.float32))


def dense(x, w, col_start=0, n_cols=None):
    lead = x.shape[:-1]
    out = pmatmul(x.reshape(-1, x.shape[-1]), w, col_start, n_cols)
    return out.reshape(lead + (out.shape[-1],))


def _moe_kernel(tile_expert_ref, n_tiles_ref, tok_ref, tok_next_ref, dst_prev_ref, x_hbm, gate_ref,
                wg_ref, wu_ref, wd_ref, out_hbm, xbuf, ybuf, wg16, wu16, wd16, sem_in, sem_out):
    i = pl.program_id(0)
    n_tiles = n_tiles_ref[0]
    slot = i % 2
    spare = out_hbm.shape[0] - MOE_TILE

    def gather_row(idx_ref, s, r):
        return pltpu.make_async_copy(x_hbm.at[pl.ds(idx_ref[0, 0, r], 1)], xbuf.at[s, pl.ds(r, 1)],
                                     sem_in.at[s])

    def scatter_row(s, r, dst_row):
        return pltpu.make_async_copy(ybuf.at[s, pl.ds(r, 1)], out_hbm.at[pl.ds(dst_row, 1)], sem_out)

    def wait_gather(s):
        pltpu.make_async_copy(x_hbm.at[pl.ds(0, MOE_TILE)], xbuf.at[s], sem_in.at[s]).wait()

    def wait_scatter():
        pltpu.make_async_copy(ybuf.at[0], out_hbm.at[pl.ds(0, MOE_TILE)], sem_out).wait()

    @pl.when(i == 0)
    def _():
        ybuf[...] = jnp.zeros(ybuf.shape, jnp.float32)

        def first_rows(r, c):
            gather_row(tok_ref, 0, r).start()
            scatter_row(0, r, spare + r).start()
            return c
        lax.fori_loop(0, MOE_TILE, first_rows, 0, unroll=8)

    @pl.when(i < n_tiles)
    def _():
        @pl.when(jnp.logical_or(i == 0, tile_expert_ref[i] != tile_expert_ref[jnp.maximum(i - 1, 0)]))
        def _():
            wg16[...] = wg_ref[0, 0].astype(jnp.bfloat16)
            wu16[...] = wu_ref[0, 0].astype(jnp.bfloat16)
            wd16[...] = wd_ref[0, 0].astype(jnp.bfloat16)

        wait_gather(slot)
        wait_scatter()
        x = xbuf[slot].astype(jnp.bfloat16)
        for r in range(MOE_TILE):
            gather_row(tok_next_ref, 1 - slot, r).start()
            scatter_row(1 - slot, r, dst_prev_ref[0, 0, r]).start()
        g = jnp.dot(x, wg16[...], preferred_element_type=jnp.float32)
        u = jnp.dot(x, wu16[...], preferred_element_type=jnp.float32)
        h = (g * jax.nn.sigmoid(g)) * u * gate_ref[...]
        ybuf[slot] = jnp.dot(h.astype(jnp.bfloat16), wd16[...], preferred_element_type=jnp.float32)

    @pl.when(i == n_tiles)
    def _():
        wait_scatter()

        def last_rows(r, c):
            scatter_row(1 - slot, r, dst_prev_ref[0, 0, r]).start()
            return c
        lax.fori_loop(0, MOE_TILE, last_rows, 0, unroll=8)
        wait_scatter()
        wait_gather(slot)


def moe_ffn(h, router_logits, router_bias, w_gate, w_up, w_down, layer):
    n_tok, D = h.shape
    aff = jax.nn.sigmoid(router_logits.astype(jnp.float32))
    biased = (aff + router_bias.astype(jnp.float32)).reshape(-1, N_GROUPS, EXPERTS_PER_GROUP)

    def top2(v):
        i1 = jnp.argmax(v, axis=-1)
        rest = jnp.where(jnp.arange(v.shape[-1]) == i1[..., None], -jnp.inf, v)
        i2 = jnp.argmax(rest, axis=-1)
        return jnp.max(v, axis=-1), jnp.max(rest, axis=-1), i1, i2

    g1, g2, _, _ = top2(biased)
    best_group = jnp.argmax(g1 + g2, axis=-1)
    cand = jnp.take_along_axis(biased, best_group[:, None, None], axis=1)[:, 0]
    _, _, i1, i2 = top2(cand)
    top_idx = best_group[:, None] * EXPERTS_PER_GROUP + jnp.stack([i1, i2], axis=-1)
    top_aff = jnp.take_along_axis(aff, top_idx, axis=-1)
    top_w = top_aff / jnp.sum(top_aff, axis=-1, keepdims=True)

    n_asg = n_tok * TOP_K
    max_tiles = n_asg // MOE_TILE + N_EXPERTS + 1
    n_rows = max_tiles * MOE_TILE
    e_flat = top_idx.T.reshape(-1).astype(jnp.int32)
    hot = (e_flat[:, None] == jnp.arange(N_EXPERTS, dtype=jnp.int32)[None, :]).astype(jnp.int32)
    running = jnp.cumsum(hot, axis=0)
    rank = jnp.sum(hot * (running - 1), axis=1)
    counts = running[-1]
    tiles_per = (counts + MOE_TILE - 1) // MOE_TILE
    tile_end = jnp.cumsum(tiles_per)
    row_start = (tile_end - tiles_per) * MOE_TILE
    row_of_asg = jnp.sum(hot * row_start[None, :], axis=1) + rank
    asg_of_row = jnp.full((n_rows,), -1, jnp.int32).at[row_of_asg].set(jnp.arange(n_asg, dtype=jnp.int32))
    token_of_row = jnp.maximum(asg_of_row, 0) % n_tok
    gate_of_row = jnp.where(asg_of_row >= 0, top_w.T.reshape(-1)[jnp.maximum(asg_of_row, 0)], 0.0)
    spare_rows = n_asg + jnp.arange(MOE_TILE, dtype=jnp.int32)
    dst3 = jnp.where(asg_of_row >= 0, asg_of_row, jnp.tile(spare_rows, max_tiles)).reshape(max_tiles, 1, MOE_TILE)
    dst_prev3 = jnp.concatenate([spare_rows.reshape(1, 1, MOE_TILE), dst3[:-1]], axis=0)
    n_tiles = tile_end[-1:].astype(jnp.int32)
    tile_expert = jnp.minimum(
        jnp.searchsorted(tile_end, jnp.arange(max_tiles, dtype=jnp.int32), side="right"),
        N_EXPERTS - 1).astype(jnp.int32)
    tile_expert = jnp.where(jnp.arange(max_tiles) < n_tiles[0], tile_expert,
                            tile_expert[jnp.maximum(n_tiles[0] - 1, 0)])

    wspec_in = pl.BlockSpec((1, 1, D, D_EXPERT), lambda i, te, nt: (layer, te[i], 0, 0))
    idx_spec = pl.BlockSpec((1, 1, MOE_TILE), lambda i, te, nt: (i, 0, 0), memory_space=pltpu.SMEM)
    next_spec = pl.BlockSpec((1, 1, MOE_TILE), lambda i, te, nt: (jnp.minimum(i + 1, max_tiles - 1), 0, 0),
                             memory_space=pltpu.SMEM)
    tok3 = token_of_row.reshape(max_tiles, 1, MOE_TILE)
    y = pl.pallas_call(
        _moe_kernel,
        grid_spec=pltpu.PrefetchScalarGridSpec(
            num_scalar_prefetch=2,
            grid=(max_tiles,),
            in_specs=[idx_spec, next_spec, idx_spec,
                      pl.BlockSpec(memory_space=pl.ANY),
                      pl.BlockSpec((MOE_TILE, 1), lambda i, te, nt: (i, 0)),
                      wspec_in, wspec_in,
                      pl.BlockSpec((1, 1, D_EXPERT, D), lambda i, te, nt: (layer, te[i], 0, 0))],
            out_specs=pl.BlockSpec(memory_space=pl.ANY),
            scratch_shapes=[pltpu.VMEM((2, MOE_TILE, D), jnp.float32),
                            pltpu.VMEM((2, MOE_TILE, D), jnp.float32),
                            pltpu.VMEM((D, D_EXPERT), jnp.bfloat16),
                            pltpu.VMEM((D, D_EXPERT), jnp.bfloat16),
                            pltpu.VMEM((D_EXPERT, D), jnp.bfloat16),
                            pltpu.SemaphoreType.DMA((2,)), pltpu.SemaphoreType.DMA(())],
        ),
        out_shape=jax.ShapeDtypeStruct((n_asg + MOE_TILE, D), jnp.float32),
        compiler_params=pltpu.CompilerParams(
            dimension_semantics=("arbitrary",), vmem_limit_bytes=MOE_VMEM_LIMIT),
        name="routed_moe",
    )(tile_expert, n_tiles, tok3, tok3, dst_prev3, h, gate_of_row[:, None], w_gate, w_up, w_down)
    return y


def _residual_ln_kernel(x_ref, *rest, n_mix, with_router):
    mix_refs, rest = rest[:n_mix], rest[n_mix:]
    if with_router:
        g_ref, b_ref, rw_ref, o_ref, o16_ref, logit_ref = rest
    else:
        g_ref, b_ref, o_ref, o16_ref = rest
    h = DEEPNORM_ALPHA * x_ref[...]
    for mix_ref in mix_refs:
        h = h + mix_ref[...]
    mu = jnp.mean(h, axis=-1, keepdims=True)
    var = jnp.mean(jnp.square(h - mu), axis=-1, keepdims=True)
    y = (h - mu) * lax.rsqrt(var + LN_EPS) * g_ref[...] + b_ref[...]
    o_ref[...] = y
    y16 = y.astype(jnp.bfloat16)
    o16_ref[...] = y16
    if with_router:
        logit_ref[...] = jnp.dot(y16, rw_ref[...].astype(jnp.bfloat16), preferred_element_type=jnp.float32)


def residual_layer_norm(x, mix, g, b, router_w=None, n_mix=1):
    n, d = x.shape
    tm = 256
    row = lambda i: (i, 0)
    fixed = lambda i: (0, 0)
    in_specs = ([pl.BlockSpec((tm, d), row)]
                + [pl.BlockSpec((tm, d), functools.partial(lambda i, s: (i + s * (n // tm), 0), s=s))
                   for s in range(n_mix)]
                + [pl.BlockSpec((1, d), fixed), pl.BlockSpec((1, d), fixed)])
    args = [x] + [mix] * n_mix + [g.astype(jnp.float32).reshape(1, d), b.astype(jnp.float32).reshape(1, d)]
    out_specs = [pl.BlockSpec((tm, d), row), pl.BlockSpec((tm, d), row)]
    out_shape = [jax.ShapeDtypeStruct((n, d), jnp.float32), jax.ShapeDtypeStruct((n, d), jnp.bfloat16)]
    if router_w is not None:
        e_pad = -router_w.shape[1] % LANE
        rw = jnp.pad(router_w.astype(jnp.float32), ((0, 0), (0, e_pad)))
        in_specs.append(pl.BlockSpec(rw.shape, fixed))
        args.append(rw)
        out_specs.append(pl.BlockSpec((tm, rw.shape[1]), row))
        out_shape.append(jax.ShapeDtypeStruct((n, rw.shape[1]), jnp.float32))
    outs = pl.pallas_call(
        functools.partial(_residual_ln_kernel, n_mix=n_mix, with_router=router_w is not None),
        grid=(n // tm,),
        in_specs=in_specs, out_specs=out_specs, out_shape=out_shape,
        compiler_params=pltpu.CompilerParams(
            dimension_semantics=("parallel",), vmem_limit_bytes=VMEM_LIMIT),
        name="residual_layer_norm",
    )(*args)
    if router_w is not None:
        return outs[0], outs[1], outs[2][:, :router_w.shape[1]]
    return outs[0], outs[1]


def _softplus(x):
    return jnp.maximum(x, 0.0) + jnp.log1p(jnp.exp(-jnp.abs(x)))


def _bdot(a, b):
    return jnp.dot(a.astype(jnp.bfloat16), b.astype(jnp.bfloat16), preferred_element_type=jnp.float32)


def _bdot_nt(a, b):
    return lax.dot_general(a.astype(jnp.bfloat16), b.astype(jnp.bfloat16), (((1,), (1,)), ((), ())),
                           preferred_element_type=jnp.float32)


def _bdot_tn(a, b):
    return lax.dot_general(a.astype(jnp.bfloat16), b.astype(jnp.bfloat16), (((0,), (0,)), ((), ())),
                           preferred_element_type=jnp.float32)


def _gdn_prep_kernel(x_ref, halo_ref, w_ref, o_ref):
    i = pl.program_id(1)
    j = pl.program_id(2)
    tt = x_ref.shape[1]
    x = x_ref[0]
    halo = jnp.where(i == 0, 0.0, halo_ref[0])
    xx = jnp.concatenate([halo, x], axis=0)
    w = w_ref[...]
    y = w[3:4] * x
    for tap in range(GDN_CONV - 1):
        lo = 8 - (GDN_CONV - 1) + tap
        y = y + w[tap:tap + 1] * xx[lo:lo + tt]
    y = y * jax.nn.sigmoid(y)
    q_blocks = GDN_QK_WIDTH // x_ref.shape[2]

    @pl.when(j >= 2 * q_blocks)
    def _():
        o_ref[0] = y

    @pl.when(j < 2 * q_blocks)
    def _():
        scale = jnp.where(j < q_blocks, GDN_HEAD_DIM ** -0.5, 1.0)
        for h in range(x_ref.shape[2] // GDN_HEAD_DIM):
            yh = y[:, h * GDN_HEAD_DIM:(h + 1) * GDN_HEAD_DIM]
            inv = lax.rsqrt(jnp.sum(yh * yh, axis=-1, keepdims=True) + RMS_EPS)
            o_ref[0, :, h * GDN_HEAD_DIM:(h + 1) * GDN_HEAD_DIM] = yh * inv * scale


def gdn_prep(qkv, conv_w):
    B, T, CH = qkv.shape
    tt, tc = GDN_TILE, 512
    return pl.pallas_call(
        _gdn_prep_kernel,
        grid=(B, T // tt, CH // tc),
        in_specs=[pl.BlockSpec((1, tt, tc), lambda b, i, j: (b, i, j)),
                  pl.BlockSpec((1, 8, tc), lambda b, i, j: (b, jnp.maximum(i * (tt // 8) - 1, 0), j)),
                  pl.BlockSpec((GDN_CONV, tc), lambda b, i, j: (0, j))],
        out_specs=pl.BlockSpec((1, tt, tc), lambda b, i, j: (b, i, j)),
        out_shape=jax.ShapeDtypeStruct((B, T, CH), jnp.float32),
        compiler_params=pltpu.CompilerParams(
            dimension_semantics=("parallel", "parallel", "parallel"), vmem_limit_bytes=VMEM_LIMIT),
        name="gdn_conv_silu_norm",
    )(qkv, qkv, conv_w)


def _gdn_kernel(q_ref, k_ref, v_ref, z_ref, acol_ref, arow_ref, bcol_ref, alog_ref, dt_ref, nw_ref,
                o_ref, state_s):
    C, Dh = GDN_CHUNK, GDN_HEAD_DIM

    @pl.when(pl.program_id(2) == 0)
    def _():
        state_s[...] = jnp.zeros(state_s.shape, jnp.float32)

    row = lax.broadcasted_iota(jnp.int32, (C, C), 0)
    col = lax.broadcasted_iota(jnp.int32, (C, C), 1)
    causal = row >= col
    strict = row > col
    nw = nw_ref[...]
    head0 = pl.program_id(1) * (2 * GDN_HK_STEP)
    lane_head = lax.broadcasted_iota(jnp.int32, (1, GDN_V_HEADS), 1)
    neg_a_all = -jnp.exp(alog_ref[...])
    g_all = neg_a_all * _softplus(acol_ref[0] + dt_ref[...])
    beta_all = jax.nn.sigmoid(bcol_ref[0])
    g_rows = []
    for vh in range(2 * GDN_HK_STEP):
        mine = lane_head == head0 + vh
        neg_a = jnp.sum(jnp.where(mine, neg_a_all, 0.0), axis=1, keepdims=True)
        dt = jnp.sum(jnp.where(mine, dt_ref[...], 0.0), axis=1, keepdims=True)
        g_rows.append(neg_a * _softplus(arow_ref[0, pl.ds(head0 + vh, 1), :] + dt))
    probs = []
    for n, kh in [(n, kh) for n in range(GDN_TILE // C) for kh in range(GDN_HK_STEP)]:
        sl = slice(n * C, (n + 1) * C)
        q = q_ref[0, sl, kh * Dh:(kh + 1) * Dh]
        k = k_ref[0, sl, kh * Dh:(kh + 1) * Dh]
        kk = _bdot_nt(k, k)
        qk = _bdot_nt(q, k)
        for hh in range(2):
            vh = 2 * kh + hh
            is_head = lane_head == head0 + vh

            def pick(a):
                return jnp.sum(jnp.where(is_head, a, 0.0), axis=1, keepdims=True)

            g_col = pick(g_all[sl, :])
            g_row = g_rows[vh][:, sl]
            beta = pick(beta_all[sl, :])
            gc_col = jnp.sum(jnp.where(causal, g_row, 0.0), axis=1, keepdims=True)
            gc_row = jnp.sum(jnp.where(row <= col, g_col, 0.0), axis=0, keepdims=True)
            g_last = jnp.sum(g_row, axis=1, keepdims=True)
            decay = jnp.where(causal, jnp.exp(jnp.where(causal, gc_col - gc_row, 0.0)), 0.0)
            m = jnp.where(strict, beta * kk * decay, 0.0)
            e_col = jnp.exp(gc_col)
            v = v_ref[0, sl, vh * Dh:(vh + 1) * Dh]
            probs.append(dict(
                sl=sl, hh=vh, x=-m, p=m, a=qk * decay,
                rhs=jnp.concatenate([beta * v, beta * e_col * k], axis=1),
                qe=q * e_col, ke=k * jnp.exp(g_last - gc_col), s_decay=jnp.exp(g_last)))
    for _ in range(int(math.log2(C)) - 1):
        for pr in probs:
            pr["p"] = _bdot(pr["p"], pr["p"])
        for pr in probs:
            pr["x"] = pr["x"] + pr["p"] + _bdot(pr["x"], pr["p"])
    for pr in probs:
        pr["sol"] = pr["rhs"] + _bdot(pr["x"], pr["rhs"])
    for pr in probs:
        sl, hh = pr["sl"], pr["hh"]
        u, w = pr["sol"][:, :Dh], pr["sol"][:, Dh:]
        state = state_s[hh]
        ws = _bdot(jnp.concatenate([w, pr["qe"]], axis=0), state)
        v_new = u - ws[:C]
        o = ws[C:] + _bdot(pr["a"], v_new)
        state_s[hh] = state * pr["s_decay"] + _bdot_tn(pr["ke"], v_new)
        z = z_ref[0, sl, hh * Dh:(hh + 1) * Dh]
        o = o * lax.rsqrt(jnp.mean(o * o, axis=-1, keepdims=True) + RMS_EPS) * nw
        o_ref[0, sl, hh * Dh:(hh + 1) * Dh] = (o * (z * jax.nn.sigmoid(z))).astype(o_ref.dtype)


def gdn_core(qkv, z, a_raw, b_raw, a_log, dt_bias, norm_w):
    B, T, _ = qkv.shape
    Hk, Dh, tt = GDN_QK_HEADS, GDN_HEAD_DIM, GDN_TILE
    Hv = GDN_V_HEADS
    hs = GDN_HK_STEP
    qw, vw = hs * Dh, 2 * hs * Dh
    k_off, v_off = GDN_QK_WIDTH // qw, 2 * GDN_QK_WIDTH // vw
    col_spec = pl.BlockSpec((1, tt, Hv), lambda b, h, c: (b, c, 0))
    head_spec = pl.BlockSpec((1, Hv), lambda b, h, c: (0, 0))
    return pl.pallas_call(
        _gdn_kernel,
        grid=(B, Hk // hs, T // tt),
        in_specs=[pl.BlockSpec((1, tt, qw), lambda b, h, c: (b, c, h)),
                  pl.BlockSpec((1, tt, qw), lambda b, h, c: (b, c, k_off + h)),
                  pl.BlockSpec((1, tt, vw), lambda b, h, c: (b, c, v_off + h)),
                  pl.BlockSpec((1, tt, vw), lambda b, h, c: (b, c, h)),
                  col_spec,
                  pl.BlockSpec((1, Hv, tt), lambda b, h, c: (b, 0, c)),
                  col_spec, head_spec, head_spec,
                  pl.BlockSpec((1, Dh), lambda b, h, c: (0, 0))],
        out_specs=pl.BlockSpec((1, tt, vw), lambda b, h, c: (b, c, h)),
        out_shape=jax.ShapeDtypeStruct((B, T, GDN_V_WIDTH), jnp.bfloat16),
        scratch_shapes=[pltpu.VMEM((2 * hs, Dh, Dh), jnp.float32)],
        compiler_params=pltpu.CompilerParams(
            dimension_semantics=("parallel", "parallel", "arbitrary"), vmem_limit_bytes=VMEM_LIMIT),
        name="gated_delta_rule",
    )(qkv, qkv, qkv, z, a_raw, jnp.swapaxes(a_raw, 1, 2), b_raw,
      a_log.astype(jnp.float32).reshape(1, Hv), dt_bias.astype(jnp.float32).reshape(1, Hv),
      norm_w.astype(jnp.float32).reshape(1, Dh))


def gdn_mixer(x, w_in, conv_w, a_log, dt_bias, norm_w, w_out):
    B, T, _ = x.shape
    Hk, Hv, Dh = GDN_QK_HEADS, GDN_V_HEADS, GDN_HEAD_DIM
    qkv = dense(x, w_in, 0, GDN_CONV_CH)
    z = dense(x, w_in, GDN_CONV_CH, GDN_V_WIDTH)
    a_raw, b_raw = jnp.split(dense(x, w_in, GDN_CONV_CH + GDN_V_WIDTH), [Hv], axis=-1)
    o = gdn_core(gdn_prep(qkv, conv_w), z, a_raw, b_raw, a_log, dt_bias, norm_w)
    return dense(o, w_out)


def _rope_cast_kernel(x_ref, cos_ref, sin_ref, o_ref, *, rotate, scale):
    D = NSA_HEAD_DIM
    for h in range(x_ref.shape[2] // D):
        x = x_ref[0, :, h * D:(h + 1) * D]
        if rotate:
            x = x * cos_ref[...] + pltpu.roll(x, D // 2, axis=1) * sin_ref[...]
        if scale != 1.0:
            x = x * scale
        o_ref[0, :, h * D:(h + 1) * D] = x.astype(o_ref.dtype)


def rope_cast(x, col_start, n_cols, rotate, scale=1.0):
    B, T, _ = x.shape
    tt, tc, D = 512, 512, NSA_HEAD_DIM
    half = D // 2
    inv_freq = ROPE_THETA ** (-jnp.arange(half, dtype=jnp.float32) / half)
    ang = jnp.arange(T, dtype=jnp.float32)[:, None] * inv_freq[None, :]
    cos2 = jnp.concatenate([jnp.cos(ang), jnp.cos(ang)], axis=1)
    sin2 = jnp.concatenate([-jnp.sin(ang), jnp.sin(ang)], axis=1)
    off = col_start // tc
    return pl.pallas_call(
        functools.partial(_rope_cast_kernel, rotate=rotate, scale=scale),
        grid=(B, T // tt, n_cols // tc),
        in_specs=[pl.BlockSpec((1, tt, tc), lambda b, i, j: (b, i, off + j)),
                  pl.BlockSpec((tt, D), lambda b, i, j: (i, 0)),
                  pl.BlockSpec((tt, D), lambda b, i, j: (i, 0))],
        out_specs=pl.BlockSpec((1, tt, tc), lambda b, i, j: (b, i, j)),
        out_shape=jax.ShapeDtypeStruct((B, T, n_cols), jnp.bfloat16),
        compiler_params=pltpu.CompilerParams(
            dimension_semantics=("parallel", "parallel", "parallel"), vmem_limit_bytes=VMEM_LIMIT),
        name="rope_cast",
    )(x, cos2, sin2)


def _compress_kernel(x_ref, pe_ref, w1_ref, w2_ref, o_ref):
    S, D = CMP_STRIDE, NSA_HEAD_DIM
    n_half = x_ref.shape[1] // S
    n_cmp = (x_ref.shape[1] - CMP_BLOCK) // S + 1
    top = jnp.zeros((n_half, CMP_HIDDEN), jnp.float32)
    bot = jnp.zeros((n_half, CMP_HIDDEN), jnp.float32)
    for s in range(S):
        xs = x_ref[0, pl.ds(s, n_half, stride=S), :]
        top = top + _bdot(xs + pe_ref[0, s:s + 1, :], w1_ref[0, s * D:(s + 1) * D, :])
        bot = bot + _bdot(xs + pe_ref[0, S + s:S + s + 1, :], w1_ref[0, (S + s) * D:(S + s + 1) * D, :])
    nxt = jnp.concatenate([bot[1:], jnp.zeros((1, CMP_HIDDEN), jnp.float32)], axis=0)
    hid = top + nxt
    hid = hid * jax.nn.sigmoid(hid)
    out = _bdot(hid, w2_ref[0])
    live = lax.broadcasted_iota(jnp.int32, out.shape, 0) < n_cmp
    o_ref[0, 0] = jnp.where(live, out, 0.0).astype(o_ref.dtype)


def compress_blocks(kv, part, pe, w1, w2):
    B, T, _ = kv.shape
    G, D = NSA_KV_GROUPS, NSA_HEAD_DIM
    return pl.pallas_call(
        _compress_kernel,
        grid=(B, G),
        in_specs=[pl.BlockSpec((1, T, D), lambda b, g: (b, 0, part * G + g)),
                  pl.BlockSpec((1, CMP_BLOCK, D), lambda b, g: (part, 0, 0)),
                  pl.BlockSpec((1, CMP_BLOCK * D, CMP_HIDDEN), lambda b, g: (part, 0, 0)),
                  pl.BlockSpec((1, CMP_HIDDEN, D), lambda b, g: (part, 0, 0))],
        out_specs=pl.BlockSpec((1, 1, T // CMP_STRIDE, D), lambda b, g: (b, g, 0, 0)),
        out_shape=jax.ShapeDtypeStruct((B, G, T // CMP_STRIDE, D), jnp.bfloat16),
        compiler_params=pltpu.CompilerParams(
            dimension_semantics=("parallel", "parallel"), vmem_limit_bytes=VMEM_LIMIT),
        name="compress_blocks",
    )(kv, pe.astype(jnp.float32), w1.astype(jnp.float32), w2.astype(jnp.float32))


def nsa_shared_kv(h, kv_w, cmp_pe, cmp_w1, cmp_w2):
    W = NSA_KV_GROUPS * NSA_HEAD_DIM
    kv = dense(h, kv_w)
    k_cmp = compress_blocks(kv, 0, cmp_pe, cmp_w1, cmp_w2)
    v_cmp = compress_blocks(kv, 1, cmp_pe, cmp_w1, cmp_w2)
    k_slc = rope_cast(kv, 2 * W, W, True)
    v_slc = rope_cast(kv, 3 * W, W, False)
    k_win = rope_cast(kv, 4 * W, W, True)
    v_win = rope_cast(kv, 5 * W, W, False)
    return (k_cmp, v_cmp, k_slc, v_slc, k_win, v_win)


def _branch_gate(gate_ref, branch, r):
    row = branch * NSA_HEADS + pl.program_id(1) * NSA_HEADS_PER_GROUP + r
    return jax.nn.sigmoid(gate_ref[0, pl.ds(row, 1), :])


def _cmp_select_kernel(q_ref, kc_ref, vc_ref, ov_ref, gate_ref, o_ref, sel_ref):
    i = pl.program_id(2)
    tq, D, n_c, n_s = ATTN_TILE, NSA_HEAD_DIM, kc_ref.shape[2], T_BLOCKS
    tpos = i * tq + lax.broadcasted_iota(jnp.int32, (n_c, tq), 1)
    block_end = lax.broadcasted_iota(jnp.int32, (n_c, tq), 0) * CMP_STRIDE + (CMP_BLOCK - 1)
    visible = block_end <= tpos
    kc = kc_ref[0, 0]
    vc = vc_ref[0, 0]
    p_sum = jnp.zeros((n_c, tq), jnp.float32)
    for r in range(NSA_HEADS_PER_GROUP):
        s = lax.dot_general(kc, q_ref[0, :, r * D:(r + 1) * D], (((1,), (1,)), ((), ())),
                            preferred_element_type=jnp.float32)
        s = jnp.where(visible, s, NEG_INF)
        e = jnp.exp2(s - jnp.max(s, axis=0, keepdims=True))
        p = jnp.where(visible, e / jnp.sum(e, axis=0, keepdims=True), 0.0)
        o_t = lax.dot_general(vc, p.astype(jnp.bfloat16), (((0,), (0,)), ((), ())),
                              preferred_element_type=jnp.float32)
        o_ref[0, :, r * D:(r + 1) * D] = (o_t * _branch_gate(gate_ref, 0, r)).T
        p_sum = p_sum + p
    p_slc = jnp.dot(ov_ref[...], p_sum.astype(jnp.bfloat16), preferred_element_type=jnp.float32)
    blk = lax.broadcasted_iota(jnp.int32, (n_s, tq), 0)
    cur = (i * tq + lax.broadcasted_iota(jnp.int32, (n_s, tq), 1)) // SLC_BLOCK
    causal_blk = blk <= cur
    forced = (blk == 0) | (causal_blk & (blk > cur - SLC_LOCAL))
    score = jnp.where(causal_blk, jnp.where(forced, FORCE_SCORE, p_slc), -1.0)
    rank = jnp.zeros((n_s, tq), jnp.float32)
    for other in range(n_s):
        row = score[other:other + 1, :]
        ahead = (row > score) | ((row == score) & (blk > other))
        rank = rank + jnp.where(ahead, 1.0, 0.0)
    picked = (rank < float(min(SLC_TOPK, n_s))) & (score >= 0.0)
    sel_ref[0, 0] = jnp.where(picked, 1.0, 0.0).astype(sel_ref.dtype)


def nsa_compressed_select(q_cmp, k_cmp, v_cmp, gate_logits_t):
    B, T, _ = q_cmp.shape
    G, R, D, tq = NSA_KV_GROUPS, NSA_HEADS_PER_GROUP, NSA_HEAD_DIM, ATTN_TILE
    n_c = k_cmp.shape[2]
    n_s = T // SLC_BLOCK
    c0 = np.arange(n_c) * CMP_STRIDE
    s0 = np.arange(n_s) * SLC_BLOCK
    ov = np.clip(np.minimum(c0[None, :] + CMP_BLOCK, s0[:, None] + SLC_BLOCK)
                 - np.maximum(c0[None, :], s0[:, None]), 0, None) / CMP_BLOCK
    cmp_spec = pl.BlockSpec((1, 1, n_c, D), lambda b, g, i: (b, g, 0, 0))
    return pl.pallas_call(
        _cmp_select_kernel,
        grid=(B, G, T // tq),
        in_specs=[pl.BlockSpec((1, tq, R * D), lambda b, g, i: (b, i, g)), cmp_spec, cmp_spec,
                  pl.BlockSpec((n_s, n_c), lambda b, g, i: (0, 0)),
                  pl.BlockSpec((1, 3 * NSA_HEADS, tq), lambda b, g, i: (b, 0, i))],
        out_specs=[pl.BlockSpec((1, tq, R * D), lambda b, g, i: (b, i, g)),
                   pl.BlockSpec((1, 1, n_s, tq), lambda b, g, i: (b, g, 0, i))],
        out_shape=[jax.ShapeDtypeStruct((B, T, G * R * D), jnp.float32),
                   jax.ShapeDtypeStruct((B, G, n_s, T), jnp.bfloat16)],
        compiler_params=pltpu.CompilerParams(
            dimension_semantics=("parallel", "parallel", "parallel"), vmem_limit_bytes=VMEM_LIMIT),
        name="compressed_attention_select",
    )(q_cmp, k_cmp, v_cmp, jnp.asarray(ov, jnp.bfloat16), gate_logits_t)


def _attn_kernel(*refs, windowed):
    if windowed:
        q_ref, k_ref, v_ref, gate_ref, prev_ref, o_ref, m_s, l_s, acc_s = refs
    else:
        q_ref, k_ref, v_ref, sel_ref, gate_ref, prev_ref, o_ref, m_s, l_s, acc_s = refs
    i = pl.program_id(2)
    tq, tk, D = ATTN_TILE, ATTN_TK, NSA_HEAD_DIM
    j_hi = (i + 1) * (tq // tk) - 1
    m_s[...] = jnp.full(m_s.shape, NEG_INF, jnp.float32)
    l_s[...] = jnp.zeros(l_s.shape, jnp.float32)
    acc_s[...] = jnp.zeros(acc_s.shape, jnp.float32)
    tpos = i * tq + lax.broadcasted_iota(jnp.int32, (tk, tq), 1)

    def body(jj, carry):
        j = j_hi - jj
        start = pl.multiple_of(j * tk, tk)
        kblk = k_ref[0, pl.ds(start, tk), :]
        vblk = v_ref[0, pl.ds(start, tk), :]
        kpos = j * tk + lax.broadcasted_iota(jnp.int32, (tk, tq), 0)
        ok = kpos <= tpos
        if windowed:
            ok = ok & (kpos > tpos - WINDOW)
        else:
            blk_of_key = j * (tk // SLC_BLOCK) + lax.broadcasted_iota(
                jnp.int32, (tk, T_BLOCKS), 0) // SLC_BLOCK
            expand = (lax.broadcasted_iota(jnp.int32, (tk, T_BLOCKS), 1) == blk_of_key)
            selm = jnp.dot(expand.astype(jnp.bfloat16), sel_ref[0, 0],
                           preferred_element_type=jnp.float32)
            ok = ok & (selm > 0.5)
        def scores(r):
            return lax.dot_general(kblk, q_ref[0, :, r * D:(r + 1) * D], (((1,), (1,)), ((), ())),
                                   preferred_element_type=jnp.float32)

        s_next = scores(0)
        for r in range(NSA_HEADS_PER_GROUP):
            s = jnp.where(ok, s_next, NEG_INF)
            if r + 1 < NSA_HEADS_PER_GROUP:
                s_next = scores(r + 1)
            m_prev = m_s[r]
            m_new = jnp.maximum(m_prev, jnp.max(s, axis=0, keepdims=True))
            alpha = jnp.exp2(m_prev - m_new)
            p = jnp.exp2(s - m_new)
            l_s[r] = alpha * l_s[r] + jnp.sum(p, axis=0, keepdims=True)
            acc_s[r] = alpha * acc_s[r] + lax.dot_general(
                vblk, p.astype(jnp.bfloat16), (((0,), (0,)), ((), ())),
                preferred_element_type=jnp.float32)
            m_s[r] = m_new
        return carry

    j_lo = jnp.maximum(i * tq - WINDOW + 1, 0) // tk if windowed else 0
    lax.fori_loop(0, j_hi - j_lo + 1, body, 0)
    branch = 2 if windowed else 1
    for r in range(NSA_HEADS_PER_GROUP):
        gated = (acc_s[r] / l_s[r] * _branch_gate(gate_ref, branch, r)).T
        o_ref[0, :, r * D:(r + 1) * D] = (prev_ref[0, :, r * D:(r + 1) * D] + gated).astype(o_ref.dtype)


def masked_attention(qs, kb, vb, sel, gate_logits_t, prev, out_dtype):
    B, T, _ = qs.shape
    G, R, D = NSA_KV_GROUPS, NSA_HEADS_PER_GROUP, NSA_HEAD_DIM
    tq = ATTN_TILE
    windowed = sel is None
    q_spec = pl.BlockSpec((1, tq, R * D), lambda b, g, i: (b, i, g))
    in_specs = [q_spec,
                pl.BlockSpec((1, T, D), lambda b, g, i: (b, 0, g)),
                pl.BlockSpec((1, T, D), lambda b, g, i: (b, 0, g))]
    args = [qs, kb, vb]
    if not windowed:
        in_specs.append(pl.BlockSpec((1, 1, T // SLC_BLOCK, tq), lambda b, g, i: (b, g, 0, i)))
        args.append(sel)
    in_specs += [pl.BlockSpec((1, 3 * NSA_HEADS, tq), lambda b, g, i: (b, 0, i)), q_spec]
    args += [gate_logits_t, prev]
    return pl.pallas_call(
        functools.partial(_attn_kernel, windowed=windowed),
        grid=(B, G, T // tq),
        in_specs=in_specs,
        out_specs=pl.BlockSpec((1, tq, R * D), lambda b, g, i: (b, i, g)),
        out_shape=jax.ShapeDtypeStruct((B, T, G * R * D), out_dtype),
        scratch_shapes=[pltpu.VMEM((R, 1, tq), jnp.float32),
                        pltpu.VMEM((R, 1, tq), jnp.float32),
                        pltpu.VMEM((R, D, tq), jnp.float32)],
        compiler_params=pltpu.CompilerParams(
            dimension_semantics=("parallel", "parallel", "arbitrary"), vmem_limit_bytes=VMEM_LIMIT),
        name="window_attention" if windowed else "selected_attention",
    )(*args)


def nsa_mixer(x, w_q, w_out, k_cmp, v_cmp, k_slc, v_slc, k_win, v_win):
    B, T, _ = x.shape
    H, D = NSA_HEADS, NSA_HEAD_DIM
    q = dense(x, w_q, 0, NSA_Q_WIDTH)
    gate_t = jnp.swapaxes(dense(x, w_q, NSA_Q_WIDTH), 1, 2)
    q_scale = D ** -0.5 * math.log2(math.e)
    q_cmp = rope_cast(q, 0, NSA_Q_WIDTH, False, q_scale)
    q_rot = rope_cast(q, 0, NSA_Q_WIDTH, True, q_scale)
    o, sel = nsa_compressed_select(q_cmp, k_cmp, v_cmp, gate_t)
    o = masked_attention(q_rot, k_slc, v_slc, sel, gate_t, o, jnp.float32)
    o = masked_attention(q_rot, k_win, v_win, None, gate_t, o, jnp.bfloat16)
    return dense(o, w_out)


def kernel(x, a_w_in, a_conv_w, a_a_log, a_dt_bias, a_norm_w, a_w_out, kv_w, cmp_pe, cmp_w1, cmp_w2,
           b_w_q, b_w_out, router_w, router_bias, moe_w_gate, moe_w_up, moe_w_down, ln_g, ln_b):
    B, T, D = x.shape
    xf = x.astype(jnp.float32).reshape(B * T, D)
    x16 = xf.astype(jnp.bfloat16)
    shared_kv = None
    for layer in range(DEPTH):
        xin = x16.reshape(B, T, D)
        if layer < N_A_LAYERS:
            mix = gdn_mixer(xin, a_w_in[layer], a_conv_w[layer], a_a_log[layer], a_dt_bias[layer],
                            a_norm_w[layer], a_w_out[layer])
        else:
            if shared_kv is None:
                shared_kv = nsa_shared_kv(xin, kv_w, cmp_pe, cmp_w1, cmp_w2)
            j = layer - N_A_LAYERS
            mix = nsa_mixer(xin, b_w_q[j], b_w_out[j], *shared_kv)
        xf, x16, logits = residual_layer_norm(xf, mix.reshape(B * T, D), ln_g[layer, 0], ln_b[layer, 0],
                                              router_w)
        ffn = moe_ffn(xf, logits, router_bias, moe_w_gate, moe_w_up, moe_w_down, layer)
        xf, x16 = residual_layer_norm(xf, ffn, ln_g[layer, 1], ln_b[layer, 1], n_mix=TOP_K)
    return xf.reshape(B, T, D).astype(x.dtype)
```
